```python
import math
import jax, jax.numpy as jnp
from jax import lax
import numpy as np

D_MODEL = 1024
BATCH = 8
SEQ = 2048
DEPTH = 2
DEC_BATCH = 128
DEC_SEQ = 8
PAST_LEN = 16384
PAGE_SIZE = 128

N_MIXERS = 2
N_SSM_LAYERS = (DEPTH + 1) // 2
N_CONV_LAYERS = DEPTH // 2
GROUP_SIZE = 16
N_GROUPS = D_MODEL // GROUP_SIZE
P_STATE = 64
D_CONV = D_MODEL
CONV_W = 3
D_FF = 4 * D_MODEL
EPS = 1e-6
DT_MIN = 1e-3
DT_MAX = 1e-1

kernel_name = "s5_shortconv_macaron_decode_step"


def _rmsnorm(x, g):
    xf = x.astype(jnp.float32)
    y = xf * lax.rsqrt(jnp.mean(xf * xf, axis=-1, keepdims=True) + EPS) * g.astype(jnp.float32)
    return y.astype(x.dtype)


def _ffn_half(x, g, w_gu, w_down):
    h = _rmsnorm(x, g)
    gate, up = jnp.split(h @ w_gu, 2, axis=-1)
    return (jax.nn.silu(gate) * up) @ w_down


def _ssm_combine(e1, e2):
    a1r, a1i, b1r, b1i = e1
    a2r, a2i, b2r, b2i = e2
    ar = a2r * a1r - a2i * a1i
    ai = a2r * a1i + a2i * a1r
    br = a2r * b1r - a2i * b1i + b2r
    bi = a2r * b1i + a2i * b1r + b2i
    return (ar, ai, br, bi)


def _s5_mixer(h, s0_re, s0_im, lam_re, lam_im, log_dt, b_re, b_im, c_re, c_im, d_skip, w_glu):
    f32 = jnp.float32
    lam_re = lam_re.astype(f32); lam_im = lam_im.astype(f32)
    dt = jnp.exp(log_dt.astype(f32))[:, None]
    mag = jnp.exp(lam_re * dt)
    lb_re = mag * jnp.cos(lam_im * dt)
    lb_im = mag * jnp.sin(lam_im * dt)
    den = lam_re * lam_re + lam_im * lam_im
    nr = lb_re - 1.0
    ni = lb_im
    f_re = (nr * lam_re + ni * lam_im) / den
    f_im = (ni * lam_re - nr * lam_im) / den
    b_re = b_re.astype(f32); b_im = b_im.astype(f32)
    bb_re = f_re[..., None] * b_re - f_im[..., None] * b_im
    bb_im = f_re[..., None] * b_im + f_im[..., None] * b_re
    c_re = c_re.astype(f32); c_im = c_im.astype(f32)
    d_skip = d_skip.astype(f32)

    def one_seq(args):
        u, sr, si = args
        L = u.shape[0]
        u = u.astype(f32)
        ug = u.reshape(L, N_GROUPS, GROUP_SIZE)
        bu_re = jnp.einsum('gpc,lgc->lgp', bb_re, ug)
        bu_im = jnp.einsum('gpc,lgc->lgp', bb_im, ug)
        sr = sr.astype(f32); si = si.astype(f32)
        bu_re = bu_re.at[0].add(lb_re * sr - lb_im * si)
        bu_im = bu_im.at[0].add(lb_re * si + lb_im * sr)
        a_re = jnp.broadcast_to(lb_re, bu_re.shape)
        a_im = jnp.broadcast_to(lb_im, bu_im.shape)
        _, _, hs_re, hs_im = lax.associative_scan(_ssm_combine, (a_re, a_im, bu_re, bu_im), axis=0)
        y = (jnp.einsum('gcp,lgp->lgc', c_re, hs_re) - jnp.einsum('gcp,lgp->lgc', c_im, hs_im))
        y = y.reshape(L, D_MODEL) + d_skip * u
        return y, hs_re[-1], hs_im[-1]

    y, new_re, new_im = lax.map(one_seq, (h, s0_re, s0_im))
    g = jax.nn.gelu(y)
    ga, gb = jnp.split(g @ w_glu.astype(f32), 2, axis=-1)
    return ga * jax.nn.sigmoid(gb), new_re, new_im


def _conv_mixer(h, buf, w_in, conv_w, w_out):
    gb, gc, v = jnp.split(h @ w_in, 3, axis=-1)
    z = gc * v
    zp = jnp.concatenate([buf.astype(z.dtype), z], axis=1)
    L = h.shape[1]
    conv = conv_w[0] * zp[:, 0:L]
    for k in range(1, CONV_W):
        conv = conv + conv_w[k] * zp[:, k:k + L]
    return (gb * conv) @ w_out, zp[:, zp.shape[1] - (CONV_W - 1):]


def setup_inputs(seed: int = 0) -> dict:
    key = jax.random.key(seed)
    ks = jax.random.split(key, 24)
    f32 = jnp.float32
    x_prompt = jax.random.normal(ks[0], (BATCH, SEQ, D_MODEL), f32)
    x_sample = jax.random.normal(ks[1], (DEC_BATCH, DEC_SEQ, D_MODEL), f32)
    state_ssm_re = 0.5 * jax.random.normal(ks[2], (N_SSM_LAYERS, DEC_BATCH, N_GROUPS, P_STATE), f32)
    state_ssm_im = 0.5 * jax.random.normal(ks[3], (N_SSM_LAYERS, DEC_BATCH, N_GROUPS, P_STATE), f32)
    cache_conv = 0.5 * jax.random.normal(ks[4], (N_CONV_LAYERS, DEC_BATCH, CONV_W - 1, D_CONV), f32)

    norm_g = 1.0 + 0.05 * jax.random.normal(ks[5], (DEPTH, 3, D_MODEL), f32)
    final_norm_g = 1.0 + 0.05 * jax.random.normal(ks[6], (D_MODEL,), f32)
    ffn_w_gate_up = jax.random.normal(ks[7], (DEPTH, 2, D_MODEL, 2 * D_FF), f32) * D_MODEL ** -0.5
    ffn_w_down = jax.random.normal(ks[8], (DEPTH, 2, D_FF, D_MODEL), f32) * D_FF ** -0.5

    n = jnp.arange(P_STATE, dtype=f32)
    ssm_lam_re = -0.5 + 0.01 * jax.random.normal(ks[9], (N_SSM_LAYERS, N_GROUPS, P_STATE), f32)
    ssm_lam_im = math.pi * n + 0.01 * jax.random.normal(ks[10], (N_SSM_LAYERS, N_GROUPS, P_STATE), f32)
    ssm_log_dt = jax.random.uniform(ks[11], (N_SSM_LAYERS, N_GROUPS), f32,
                                    minval=math.log(DT_MIN), maxval=math.log(DT_MAX))
    bscale = (2.0 * GROUP_SIZE) ** -0.5
    ssm_b_re = jax.random.normal(ks[12], (N_SSM_LAYERS, N_GROUPS, P_STATE, GROUP_SIZE), f32) * bscale
    ssm_b_im = jax.random.normal(ks[13], (N_SSM_LAYERS, N_GROUPS, P_STATE, GROUP_SIZE), f32) * bscale
    cscale = (2.0 * P_STATE) ** -0.5
    ssm_c_re = jax.random.normal(ks[14], (N_SSM_LAYERS, N_GROUPS, GROUP_SIZE, P_STATE), f32) * cscale
    ssm_c_im = jax.random.normal(ks[15], (N_SSM_LAYERS, N_GROUPS, GROUP_SIZE, P_STATE), f32) * cscale
    ssm_d = jax.random.normal(ks[16], (N_SSM_LAYERS, D_MODEL), f32)
    ssm_w_glu = jax.random.normal(ks[17], (N_SSM_LAYERS, D_MODEL, 2 * D_MODEL), f32) * D_MODEL ** -0.5

    conv_w_in = jax.random.normal(ks[18], (N_CONV_LAYERS, D_MODEL, 3 * D_CONV), f32) * D_MODEL ** -0.5
    conv_w = jax.random.normal(ks[19], (N_CONV_LAYERS, CONV_W, D_CONV), f32) * CONV_W ** -0.5
    conv_w_out = jax.random.normal(ks[20], (N_CONV_LAYERS, D_CONV, D_MODEL), f32) * D_CONV ** -0.5
    return {"x_prompt": x_prompt, "x_sample": x_sample,
            "state_ssm_re": state_ssm_re, "state_ssm_im": state_ssm_im, "cache_conv": cache_conv,
            "norm_g": norm_g, "final_norm_g": final_norm_g,
            "ffn_w_gate_up": ffn_w_gate_up, "ffn_w_down": ffn_w_down,
            "ssm_lam_re": ssm_lam_re, "ssm_lam_im": ssm_lam_im, "ssm_log_dt": ssm_log_dt,
            "ssm_b_re": ssm_b_re, "ssm_b_im": ssm_b_im, "ssm_c_re": ssm_c_re, "ssm_c_im": ssm_c_im,
            "ssm_d": ssm_d, "ssm_w_glu": ssm_w_glu,
            "conv_w_in": conv_w_in, "conv_w": conv_w, "conv_w_out": conv_w_out}


def reference(x_prompt, x_sample, state_ssm_re, state_ssm_im, cache_conv,
              norm_g, final_norm_g, ffn_w_gate_up, ffn_w_down,
              ssm_lam_re, ssm_lam_im, ssm_log_dt, ssm_b_re, ssm_b_im, ssm_c_re, ssm_c_im,
              ssm_d, ssm_w_glu, conv_w_in, conv_w, conv_w_out):
    nb = x_prompt.shape[0]
    zero_re = jnp.zeros((N_SSM_LAYERS, nb, N_GROUPS, P_STATE), jnp.float32)
    zero_im = jnp.zeros((N_SSM_LAYERS, nb, N_GROUPS, P_STATE), jnp.float32)
    zero_buf = jnp.zeros((N_CONV_LAYERS, nb, CONV_W - 1, D_CONV), x_prompt.dtype)
    groups = ((x_prompt, zero_re, zero_im, zero_buf),
              (x_sample, state_ssm_re, state_ssm_im, cache_conv))
    outs = []
    for x, s_re, s_im, buf in groups:
        new_re, new_im, new_buf = [], [], []
        for i in range(DEPTH):
            x = x + 0.5 * _ffn_half(x, norm_g[i, 0], ffn_w_gate_up[i, 0], ffn_w_down[i, 0])
            h = _rmsnorm(x, norm_g[i, 1])
            j = i // N_MIXERS
            if i % N_MIXERS == 0:
                m, r, im = _s5_mixer(h, s_re[j], s_im[j], ssm_lam_re[j], ssm_lam_im[j], ssm_log_dt[j],
                                     ssm_b_re[j], ssm_b_im[j], ssm_c_re[j], ssm_c_im[j],
                                     ssm_d[j], ssm_w_glu[j])
                new_re.append(r)
                new_im.append(im)
            else:
                m, b = _conv_mixer(h, buf[j], conv_w_in[j], conv_w[j], conv_w_out[j])
                new_buf.append(b)
            x = x + m.astype(x.dtype)
            x = x + 0.5 * _ffn_half(x, norm_g[i, 2], ffn_w_gate_up[i, 1], ffn_w_down[i, 1])
        y = _rmsnorm(x, final_norm_g)
        outs.append((y, jnp.stack(new_re), jnp.stack(new_im), jnp.stack(new_buf)))
    (y_prompt, sre_p, sim_p, conv_p), (y_sample, sre_s, sim_s, conv_s) = outs
    return (y_prompt, y_sample, sre_p, sim_p, conv_p, sre_s, sim_s, conv_s)
```

```python
import functools

import jax
import jax.numpy as jnp
from jax import lax
from jax.experimental import pallas as pl
from jax.experimental.pallas import tpu as pltpu

F32 = jnp.float32
BF16 = jnp.bfloat16

D_MODEL = 1024
D_FF = 4 * D_MODEL
GROUP_SIZE = 16
N_GROUPS = D_MODEL // GROUP_SIZE
P_STATE = 64
S_DIM = N_GROUPS * P_STATE
CONV_W = 3
EPS = 1e-6

SUBLANES = 8
FFN_ROWS = 1024
FFN_COLS = 512
SSM_BLOCKS = 4
SSM_BLOCK_CH = D_MODEL // SSM_BLOCKS
SSM_BLOCK_ST = S_DIM // SSM_BLOCKS
SCAN_LANES = 512
VMEM_LIMIT = 56 * 1024 * 1024


def _rmsnorm(x, g):
    return x * lax.rsqrt(jnp.mean(x * x, axis=-1, keepdims=True) + EPS) * g


def _dot(a, b):
    return jnp.dot(a, b, preferred_element_type=F32)


def _ffn_kernel(x_ref, g_ref, wg_ref, wu_ref, wd_ref, *rest, chunk_axis, n_chunks, final_norm):
    if final_norm:
        gf_ref, o_ref, h_ref, acc_ref = rest
    else:
        o_ref, h_ref, acc_ref = rest
    j = pl.program_id(chunk_axis)

    @pl.when(j == 0)
    def _():
        h_ref[...] = _rmsnorm(x_ref[...], g_ref[...]).astype(BF16)

    h = h_ref[...]
    gate = _dot(h, wg_ref[...])
    up = _dot(h, wu_ref[...])
    act = (jax.nn.silu(gate) * up).astype(BF16)
    contrib = _dot(act, wd_ref[...])

    @pl.when(j == 0)
    def _():
        acc_ref[...] = contrib

    @pl.when(j > 0)
    def _():
        acc_ref[...] += contrib

    @pl.when(j == n_chunks - 1)
    def _():
        out = x_ref[...] + 0.5 * acc_ref[...]
        if final_norm:
            out = _rmsnorm(out, gf_ref[...])
        o_ref[...] = out


def _ffn_call(x, x_block, x_map, out_shape, out_block, out_map, outer_grid,
              g, w_gu, w_down, final_g=None):
    n_chunks = D_FF // FFN_COLS
    n_outer = len(outer_grid)

    body = functools.partial(_ffn_kernel, chunk_axis=n_outer, n_chunks=n_chunks,
                             final_norm=final_g is not None)

    def outer(fn):
        return lambda *idx: fn(*idx[:n_outer])

    const2 = lambda *idx: (0, 0)
    in_specs = [
        pl.BlockSpec(x_block, outer(x_map)),
        pl.BlockSpec((1, D_MODEL), const2),
        pl.BlockSpec((D_MODEL, FFN_COLS), lambda *idx: (0, idx[n_outer])),
        pl.BlockSpec((D_MODEL, FFN_COLS), lambda *idx: (0, idx[n_outer] + n_chunks)),
        pl.BlockSpec((FFN_COLS, D_MODEL), lambda *idx: (idx[n_outer], 0)),
    ]
    args = [x, g.reshape(1, D_MODEL), w_gu, w_gu, w_down]
    if final_g is not None:
        in_specs.append(pl.BlockSpec((1, D_MODEL), const2))
        args.append(final_g.reshape(1, D_MODEL))
    return pl.pallas_call(
        body,
        grid=(*outer_grid, n_chunks),
        in_specs=in_specs,
        out_specs=pl.BlockSpec(out_block, outer(out_map)),
        out_shape=jax.ShapeDtypeStruct(out_shape, F32),
        scratch_shapes=[pltpu.VMEM((FFN_ROWS, D_MODEL), BF16),
                        pltpu.VMEM((FFN_ROWS, D_MODEL), F32)],
        compiler_params=pltpu.CompilerParams(
            dimension_semantics=("arbitrary",) * (n_outer + 1),
            vmem_limit_bytes=VMEM_LIMIT),
        name="ffn",
    )(*args)


def _ssm_prep_kernel(lam_re_ref, lam_im_ref, ldt_ref, lam_re_rep_ref, lam_im_rep_ref,
                     b_re_ref, b_im_ref, lb_re_ref, lb_im_ref, bb_re_ref, bb_im_ref):
    dt = jnp.exp(ldt_ref[...])

    def discretise(lam_re, lam_im):
        mag = jnp.exp(lam_re * dt)
        lb_re = mag * jnp.cos(lam_im * dt)
        lb_im = mag * jnp.sin(lam_im * dt)
        return lb_re, lb_im

    lb_re, lb_im = discretise(lam_re_ref[...], lam_im_ref[...])
    lb_re_ref[...] = lb_re
    lb_im_ref[...] = lb_im
    lam_re = lam_re_rep_ref[...]
    lam_im = lam_im_rep_ref[...]
    lbr, lbi = discretise(lam_re, lam_im)
    den = lam_re * lam_re + lam_im * lam_im
    nr = lbr - 1.0
    ni = lbi
    f_re = (nr * lam_re + ni * lam_im) / den
    f_im = (ni * lam_re - nr * lam_im) / den
    b_re = b_re_ref[...]
    b_im = b_im_ref[...]
    bb_re_ref[...] = f_re * b_re - f_im * b_im
    bb_im_ref[...] = f_re * b_im + f_im * b_re


def _ssm_prep(lam_re, lam_im, log_dt, b_re, b_im):
    pc = P_STATE * GROUP_SIZE
    rep = lambda a: jnp.repeat(a, GROUP_SIZE, axis=-1)
    shp = lambda n: jax.ShapeDtypeStruct((N_GROUPS, n), F32)
    return pl.pallas_call(
        _ssm_prep_kernel,
        out_shape=(shp(P_STATE), shp(P_STATE), shp(pc), shp(pc)),
        name="ssm_prep",
    )(lam_re, lam_im, log_dt.reshape(N_GROUPS, 1), rep(lam_re), rep(lam_im),
      b_re.reshape(N_GROUPS, pc), b_im.reshape(N_GROUPS, pc))


def _block_diag(w):
    gpb = N_GROUPS // SSM_BLOCKS
    a, b = w.shape[1], w.shape[2]
    w = w.reshape(SSM_BLOCKS, gpb, a, 1, b)
    eye = jnp.eye(gpb, dtype=bool)[None, :, None, :, None]
    return jnp.where(eye, w, 0.0).reshape(SSM_BLOCKS, gpb * a, gpb * b)


def _s5_kernel(x_ref, g_ref, s0_re_ref, s0_im_ref, lb_re_ref, lb_im_ref,
               wb_re_ref, wb_im_ref, wc_re_ref, wc_im_ref, d_ref, wglu_ref,
               o_ref, new_re_ref, new_im_ref,
               st_re, st_im, bu_re, bu_im, *, steps, n_seq, carry_over_grid):
    rows = steps * n_seq
    if carry_over_grid:
        @pl.when(pl.program_id(0) == 0)
        def _():
            st_re[...] = s0_re_ref[...]
            st_im[...] = s0_im_ref[...]
    else:
        st_re[...] = s0_re_ref[...]
        st_im[...] = s0_im_ref[...]

    x = x_ref[...].reshape(rows, D_MODEL)
    u = _rmsnorm(x, g_ref[...])
    ub = u.astype(BF16)

    def scan_block(k):
        for c in range(SSM_BLOCK_ST // SCAN_LANES):
            lo = c * SCAN_LANES
            glo = k * SSM_BLOCK_ST + lo
            lr = jnp.broadcast_to(lb_re_ref[:, glo:glo + SCAN_LANES], (SUBLANES, SCAN_LANES))
            li = jnp.broadcast_to(lb_im_ref[:, glo:glo + SCAN_LANES], (SUBLANES, SCAN_LANES))

            def seq_tile(nb, carry):
                r_state = pl.multiple_of(nb * SUBLANES, SUBLANES)
                hr0 = st_re[pl.ds(r_state, SUBLANES), glo:glo + SCAN_LANES]
                hi0 = st_im[pl.ds(r_state, SUBLANES), glo:glo + SCAN_LANES]

                def step(t, h):
                    hr, hi = h
                    r = pl.multiple_of(t * n_seq + nb * SUBLANES, SUBLANES)
                    br = bu_re[pl.ds(r, SUBLANES), lo:lo + SCAN_LANES]
                    bi = bu_im[pl.ds(r, SUBLANES), lo:lo + SCAN_LANES]
                    nhr = lr * hr - li * hi + br
                    nhi = lr * hi + li * hr + bi
                    bu_re[pl.ds(r, SUBLANES), lo:lo + SCAN_LANES] = nhr
                    bu_im[pl.ds(r, SUBLANES), lo:lo + SCAN_LANES] = nhi
                    return nhr, nhi

                hr, hi = lax.fori_loop(0, steps, step, (hr0, hi0), unroll=8)
                st_re[pl.ds(r_state, SUBLANES), glo:glo + SCAN_LANES] = hr
                st_im[pl.ds(r_state, SUBLANES), glo:glo + SCAN_LANES] = hi
                return carry

            lax.fori_loop(0, n_seq // SUBLANES, seq_tile, 0)

    ys = []
    for k in range(SSM_BLOCKS):
        uk = ub[:, k * SSM_BLOCK_CH:(k + 1) * SSM_BLOCK_CH]
        bu_re[...] = _dot(uk, wb_re_ref[k])
        bu_im[...] = _dot(uk, wb_im_ref[k])
        scan_block(k)
        hr = bu_re[...].astype(BF16)
        hi = bu_im[...].astype(BF16)
        ys.append(_dot(hr, wc_re_ref[k]) - _dot(hi, wc_im_ref[k]))
    y = jnp.concatenate(ys, axis=-1) + d_ref[...] * u
    z = _dot(jax.nn.gelu(y).astype(BF16), wglu_ref[...])
    m = z[:, :D_MODEL] * jax.nn.sigmoid(z[:, D_MODEL:])
    o_ref[...] = (x + m).reshape(o_ref.shape)
    new_re_ref[...] = st_re[...]
    new_im_ref[...] = st_im[...]


def _s5_call(x, x_block, x_map, grid, s0_re, s0_im, state_map, g, lb_re, lb_im,
             wb_re, wb_im, wc_re, wc_im, d_skip, wglu, *, steps, n_seq, carry_over_grid):
    rows = steps * n_seq
    n_state = s0_re.shape[0]
    whole = lambda a: pl.BlockSpec(a.shape, lambda i: (0,) * a.ndim)
    state_spec = pl.BlockSpec((n_seq, S_DIM), state_map)
    g2 = g.reshape(1, D_MODEL)
    d2 = d_skip.reshape(1, D_MODEL)
    lbr = lb_re.reshape(1, S_DIM)
    lbi = lb_im.reshape(1, S_DIM)
    body = functools.partial(_s5_kernel, steps=steps, n_seq=n_seq, carry_over_grid=carry_over_grid)
    return pl.pallas_call(
        body,
        grid=grid,
        in_specs=[pl.BlockSpec(x_block, x_map), whole(g2), state_spec, state_spec,
                  whole(lbr), whole(lbi), whole(wb_re), whole(wb_im), whole(wc_re), whole(wc_im),
                  whole(d2), whole(wglu)],
        out_specs=(pl.BlockSpec(x_block, x_map), state_spec, state_spec),
        out_shape=(jax.ShapeDtypeStruct(x.shape, F32),
                   jax.ShapeDtypeStruct((n_state, S_DIM), F32),
                   jax.ShapeDtypeStruct((n_state, S_DIM), F32)),
        scratch_shapes=[pltpu.VMEM((n_seq, S_DIM), F32), pltpu.VMEM((n_seq, S_DIM), F32),
                        pltpu.VMEM((rows, SSM_BLOCK_ST), F32), pltpu.VMEM((rows, SSM_BLOCK_ST), F32)],
        compiler_params=pltpu.CompilerParams(
            dimension_semantics=("arbitrary",), vmem_limit_bytes=VMEM_LIMIT),
        name="s5_mixer",
    )(x, g2, s0_re, s0_im, lbr, lbi, wb_re, wb_im, wc_re, wc_im, d2, wglu)


def _conv_kernel(x_ref, g_ref, buf0_ref, win_ref, cw_ref, wout_ref, o_ref, newbuf_ref, zp,
                 *, steps, n_seq, carry_over_grid):
    rows = steps * n_seq
    halo = (CONV_W - 1) * n_seq
    if carry_over_grid:
        @pl.when(pl.program_id(0) == 0)
        def _():
            zp[0:halo, :] = buf0_ref[...].reshape(halo, D_MODEL)
    else:
        zp[0:halo, :] = buf0_ref[...].reshape(halo, D_MODEL)

    x = x_ref[...].reshape(rows, D_MODEL)
    h = _rmsnorm(x, g_ref[...]).astype(BF16)
    p = _dot(h, win_ref[...])
    gb = p[:, :D_MODEL]
    zp[halo:halo + rows, :] = p[:, D_MODEL:2 * D_MODEL] * p[:, 2 * D_MODEL:]
    conv = cw_ref[0:1, :] * zp[0:rows, :]
    for k in range(1, CONV_W):
        conv = conv + cw_ref[k:k + 1, :] * zp[k * n_seq:k * n_seq + rows, :]
    m = _dot((gb * conv).astype(BF16), wout_ref[...])
    o_ref[...] = (x + m).reshape(o_ref.shape)
    tail = zp[rows:rows + halo, :]
    zp[0:halo, :] = tail
    newbuf_ref[...] = tail.reshape(newbuf_ref.shape)


def _conv_call(x, x_block, x_map, grid, buf0, buf_block, buf_map, g, w_in, cw, w_out,
               *, steps, n_seq, carry_over_grid):
    rows = steps * n_seq
    halo = (CONV_W - 1) * n_seq
    whole = lambda a: pl.BlockSpec(a.shape, lambda i: (0,) * a.ndim)
    g2 = g.reshape(1, D_MODEL)
    body = functools.partial(_conv_kernel, steps=steps, n_seq=n_seq, carry_over_grid=carry_over_grid)
    return pl.pallas_call(
        body,
        grid=grid,
        in_specs=[pl.BlockSpec(x_block, x_map), whole(g2), pl.BlockSpec(buf_block, buf_map),
                  whole(w_in), whole(cw), whole(w_out)],
        out_specs=(pl.BlockSpec(x_block, x_map), pl.BlockSpec(buf_block, buf_map)),
        out_shape=(jax.ShapeDtypeStruct(x.shape, F32), jax.ShapeDtypeStruct(buf0.shape, F32)),
        scratch_shapes=[pltpu.VMEM((halo + rows, D_MODEL), F32)],
        compiler_params=pltpu.CompilerParams(
            dimension_semantics=("arbitrary",), vmem_limit_bytes=VMEM_LIMIT),
        name="conv_mixer",
    )(x, g2, buf0, w_in, cw, w_out)


S5_PROMPT_STEPS = 64
CONV_PROMPT_STEPS = 64
SAMPLE_SEQ_BLOCK = 32


def kernel(x_prompt, x_sample, state_ssm_re, state_ssm_im, cache_conv, norm_g, final_norm_g, ffn_w_gate_up, ffn_w_down, ssm_lam_re, ssm_lam_im, ssm_log_dt, ssm_b_re, ssm_b_im, ssm_c_re, ssm_c_im, ssm_d, ssm_w_glu, conv_w_in, conv_w, conv_w_out):
    nb_p, len_p, _ = x_prompt.shape
    nb_s, len_s, _ = x_sample.shape
    rows_p = nb_p * len_p
    rows_s = nb_s * len_s
    assert nb_p == SUBLANES and len_p % FFN_ROWS == 0 and rows_s == FFN_ROWS
    assert nb_s % SAMPLE_SEQ_BLOCK == 0

    w_gu = ffn_w_gate_up.astype(BF16)
    w_dn = ffn_w_down.astype(BF16)

    lb_re, lb_im, bb_re, bb_im = _ssm_prep(ssm_lam_re[0], ssm_lam_im[0], ssm_log_dt[0],
                                           ssm_b_re[0], ssm_b_im[0])
    to_gcp = lambda a: jnp.swapaxes(a.reshape(N_GROUPS, P_STATE, GROUP_SIZE), 1, 2)
    wb_re = _block_diag(to_gcp(bb_re)).astype(BF16)
    wb_im = _block_diag(to_gcp(bb_im)).astype(BF16)
    wc_re = _block_diag(jnp.swapaxes(ssm_c_re[0], 1, 2)).astype(BF16)
    wc_im = _block_diag(jnp.swapaxes(ssm_c_im[0], 1, 2)).astype(BF16)
    wglu = ssm_w_glu[0].astype(BF16)
    w_in = conv_w_in[0].astype(BF16)
    w_out = conv_w_out[0].astype(BF16)

    t_tiles = len_p // FFN_ROWS
    row_tiles_p = rows_p // FFN_ROWS
    flat_block = (FFN_ROWS, D_MODEL)
    flat_map = lambda i: (i, 0)

    xp = _ffn_call(x_prompt, (None, FFN_ROWS, D_MODEL), lambda n, t: (n, t, 0),
                   (len_p, nb_p * D_MODEL), flat_block, lambda n, t: (t, n), (nb_p, t_tiles),
                   norm_g[0, 0], w_gu[0, 0], w_dn[0, 0]).reshape(rows_p, D_MODEL)
    zero_state = jnp.zeros((nb_p, S_DIM), F32)
    xp, sre_p, sim_p = _s5_call(
        xp, (S5_PROMPT_STEPS * nb_p, D_MODEL), flat_map, (len_p // S5_PROMPT_STEPS,),
        zero_state, zero_state, lambda i: (0, 0), norm_g[0, 1], lb_re, lb_im,
        wb_re, wb_im, wc_re, wc_im, ssm_d[0], wglu,
        steps=S5_PROMPT_STEPS, n_seq=nb_p, carry_over_grid=True)
    xp = _ffn_call(xp, flat_block, lambda i: (i, 0), (rows_p, D_MODEL), flat_block, lambda i: (i, 0),
                   (row_tiles_p,), norm_g[0, 2], w_gu[0, 1], w_dn[0, 1])
    xp = _ffn_call(xp, flat_block, lambda i: (i, 0), (rows_p, D_MODEL), flat_block, lambda i: (i, 0),
                   (row_tiles_p,), norm_g[1, 0], w_gu[1, 0], w_dn[1, 0])
    halo_p = (CONV_W - 1) * nb_p
    xp, buf_p = _conv_call(
        xp, (CONV_PROMPT_STEPS * nb_p, D_MODEL), flat_map, (len_p // CONV_PROMPT_STEPS,),
        jnp.zeros((halo_p, D_MODEL), F32), (halo_p, D_MODEL), lambda i: (0, 0),
        norm_g[1, 1], w_in, conv_w[0], w_out,
        steps=CONV_PROMPT_STEPS, n_seq=nb_p, carry_over_grid=True)
    y_prompt = _ffn_call(xp.reshape(len_p, nb_p * D_MODEL), flat_block, lambda n, t: (t, n),
                         (nb_p, len_p, D_MODEL), (None, FFN_ROWS, D_MODEL), lambda n, t: (n, t, 0),
                         (nb_p, t_tiles), norm_g[1, 2], w_gu[1, 1], w_dn[1, 1], final_g=final_norm_g)

    xs = jnp.swapaxes(x_sample, 0, 1).reshape(rows_s, D_MODEL)
    one_tile = dict(x_block=flat_block, x_map=lambda i: (i, 0), out_shape=(rows_s, D_MODEL),
                    out_block=flat_block, out_map=lambda i: (i, 0), outer_grid=(1,))
    xs = _ffn_call(xs, g=norm_g[0, 0], w_gu=w_gu[0, 0], w_down=w_dn[0, 0], **one_tile)
    seq_block = (len_s, SAMPLE_SEQ_BLOCK, D_MODEL)
    seq_map = lambda i: (0, i, 0)
    seq_grid = (nb_s // SAMPLE_SEQ_BLOCK,)
    xs, sre_s, sim_s = _s5_call(
        xs.reshape(len_s, nb_s, D_MODEL), seq_block, seq_map, seq_grid,
        state_ssm_re[0].reshape(nb_s, S_DIM), state_ssm_im[0].reshape(nb_s, S_DIM), lambda i: (i, 0),
        norm_g[0, 1], lb_re, lb_im, wb_re, wb_im, wc_re, wc_im, ssm_d[0], wglu,
        steps=len_s, n_seq=SAMPLE_SEQ_BLOCK, carry_over_grid=False)
    xs = xs.reshape(rows_s, D_MODEL)
    xs = _ffn_call(xs, g=norm_g[0, 2], w_gu=w_gu[0, 1], w_down=w_dn[0, 1], **one_tile)
    xs = _ffn_call(xs, g=norm_g[1, 0], w_gu=w_gu[1, 0], w_down=w_dn[1, 0], **one_tile)
    buf0_s = jnp.swapaxes(cache_conv[0], 0, 1)
    xs, buf_s = _conv_call(
        xs.reshape(len_s, nb_s, D_MODEL), seq_block, seq_map, seq_grid,
        buf0_s, (CONV_W - 1, SAMPLE_SEQ_BLOCK, D_MODEL), seq_map,
        norm_g[1, 1], w_in, conv_w[0], w_out,
        steps=len_s, n_seq=SAMPLE_SEQ_BLOCK, carry_over_grid=False)
    xs = xs.reshape(rows_s, D_MODEL)
    ys = _ffn_call(xs, g=norm_g[1, 2], w_gu=w_gu[1, 1], w_down=w_dn[1, 1], final_g=final_norm_g, **one_tile)
    y_sample = jnp.swapaxes(ys.reshape(len_s, nb_s, D_MODEL), 0, 1)

    state4 = lambda s: s.reshape(1, -1, N_GROUPS, P_STATE)
    new_conv_p = jnp.swapaxes(buf_p.reshape(CONV_W - 1, nb_p, D_MODEL), 0, 1)[None]
    new_conv_s = jnp.swapaxes(buf_s, 0, 1)[None]
    return (y_prompt, y_sample, state4(sre_p), state4(sim_p), new_conv_p,
            state4(sre_s), state4(sim_s), new_conv_s)
```

```python
import functools

import jax
import jax.numpy as jnp
from jax import lax
from jax.experimental import pallas as pl
from jax.experimental.pallas import tpu as pltpu

F32 = jnp.float32
BF16 = jnp.bfloat16

D_MODEL = 1024
D_FF = 4 * D_MODEL
GROUP_SIZE = 16
N_GROUPS = D_MODEL // GROUP_SIZE
P_STATE = 64
S_DIM = N_GROUPS * P_STATE
CONV_W = 3
EPS = 1e-6

SUBLANES = 8
FFN_ROWS = 1024
FFN_COLS = 512
SSM_BLOCKS = 4
SSM_BLOCK_CH = D_MODEL // SSM_BLOCKS
SSM_BLOCK_ST = S_DIM // SSM_BLOCKS
SCAN_LANES = 512
VMEM_LIMIT = 56 * 1024 * 1024


def _rmsnorm(x, g):
    return x * lax.rsqrt(jnp.mean(x * x, axis=-1, keepdims=True) + EPS) * g


def _dot(a, b):
    return jnp.dot(a, b, preferred_element_type=F32)


def _ffn_kernel(x_ref, g_ref, wg_ref, wu_ref, wd_ref, *rest, chunk_axis, n_chunks, final_norm,
                x_seq_major, out_seq_major):
    rest = list(rest)
    gf_ref = rest.pop(0) if final_norm else None
    o_ref, h_ref, acc_ref = rest[:3]
    xt_ref = rest[3] if x_seq_major else x_ref
    j = pl.program_id(chunk_axis)

    @pl.when(j == 0)
    def _():
        if x_seq_major:
            for t in range(x_ref.shape[1]):
                xt_ref[t * SUBLANES:(t + 1) * SUBLANES, :] = x_ref[:, t, :]
        h_ref[...] = _rmsnorm(xt_ref[...], g_ref[...]).astype(BF16)

    h = h_ref[...]
    gate = _dot(h, wg_ref[...])
    up = _dot(h, wu_ref[...])
    act = (jax.nn.silu(gate) * up).astype(BF16)
    contrib = _dot(act, wd_ref[...])

    @pl.when(j == 0)
    def _():
        acc_ref[...] = contrib

    @pl.when(j > 0)
    def _():
        acc_ref[...] += contrib

    @pl.when(j == n_chunks - 1)
    def _():
        out = xt_ref[...] + 0.5 * acc_ref[...]
        if final_norm:
            out = _rmsnorm(out, gf_ref[...])
        if out_seq_major:
            for t in range(o_ref.shape[1]):
                o_ref[:, t, :] = out[t * SUBLANES:(t + 1) * SUBLANES, :]
        else:
            o_ref[...] = out


def _ffn_call(x, x_block, x_map, out_shape, out_block, out_map, outer_grid,
              g, w_gu, w_down, which, final_g=None):
    n_chunks = D_FF // FFN_COLS
    n_outer = len(outer_grid)
    layer, half = which

    x_seq_major = len(x_block) == 3
    out_seq_major = len(out_block) == 3
    body = functools.partial(_ffn_kernel, chunk_axis=n_outer, n_chunks=n_chunks,
                             final_norm=final_g is not None,
                             x_seq_major=x_seq_major, out_seq_major=out_seq_major)
    scratch = [pltpu.VMEM((FFN_ROWS, D_MODEL), BF16), pltpu.VMEM((FFN_ROWS, D_MODEL), F32)]
    if x_seq_major:
        scratch.append(pltpu.VMEM((FFN_ROWS, D_MODEL), F32))

    def outer(fn):
        return lambda *idx: fn(*idx[:n_outer])

    const2 = lambda *idx: (0, 0)
    in_specs = [
        pl.BlockSpec(x_block, outer(x_map)),
        pl.BlockSpec((1, D_MODEL), const2),
        pl.BlockSpec((None, None, D_MODEL, FFN_COLS), lambda *idx: (layer, half, 0, idx[n_outer])),
        pl.BlockSpec((None, None, D_MODEL, FFN_COLS),
                     lambda *idx: (layer, half, 0, idx[n_outer] + n_chunks)),
        pl.BlockSpec((None, None, FFN_COLS, D_MODEL), lambda *idx: (layer, half, idx[n_outer], 0)),
    ]
    args = [x, g.reshape(1, D_MODEL), w_gu, w_gu, w_down]
    if final_g is not None:
        in_specs.append(pl.BlockSpec((1, D_MODEL), const2))
        args.append(final_g.reshape(1, D_MODEL))
    return pl.pallas_call(
        body,
        grid=(*outer_grid, n_chunks),
        in_specs=in_specs,
        out_specs=pl.BlockSpec(out_block, outer(out_map)),
        out_shape=jax.ShapeDtypeStruct(out_shape, F32),
        scratch_shapes=scratch,
        compiler_params=pltpu.CompilerParams(
            dimension_semantics=("arbitrary",) * (n_outer + 1),
            vmem_limit_bytes=VMEM_LIMIT),
        name="ffn",
    )(*args)


def _ssm_prep_kernel(lam_re_ref, lam_im_ref, ldt_ref, lam_re_rep_ref, lam_im_rep_ref,
                     b_re_ref, b_im_ref, lb_re_ref, lb_im_ref, bb_re_ref, bb_im_ref):
    dt = jnp.exp(ldt_ref[...])

    def discretise(lam_re, lam_im):
        mag = jnp.exp(lam_re * dt)
        lb_re = mag * jnp.cos(lam_im * dt)
        lb_im = mag * jnp.sin(lam_im * dt)
        return lb_re, lb_im

    lb_re, lb_im = discretise(lam_re_ref[...], lam_im_ref[...])
    lb_re_ref[...] = lb_re
    lb_im_ref[...] = lb_im
    lam_re = lam_re_rep_ref[...]
    lam_im = lam_im_rep_ref[...]
    lbr, lbi = discretise(lam_re, lam_im)
    den = lam_re * lam_re + lam_im * lam_im
    nr = lbr - 1.0
    ni = lbi
    f_re = (nr * lam_re + ni * lam_im) / den
    f_im = (ni * lam_re - nr * lam_im) / den
    b_re = b_re_ref[...]
    b_im = b_im_ref[...]
    bb_re_ref[...] = f_re * b_re - f_im * b_im
    bb_im_ref[...] = f_re * b_im + f_im * b_re


def _ssm_prep(lam_re, lam_im, log_dt, b_re, b_im):
    pc = P_STATE * GROUP_SIZE
    rep = lambda a: jnp.repeat(a, GROUP_SIZE, axis=-1)
    shp = lambda n: jax.ShapeDtypeStruct((N_GROUPS, n), F32)
    return pl.pallas_call(
        _ssm_prep_kernel,
        out_shape=(shp(P_STATE), shp(P_STATE), shp(pc), shp(pc)),
        name="ssm_prep",
    )(lam_re, lam_im, log_dt.reshape(N_GROUPS, 1), rep(lam_re), rep(lam_im),
      b_re.reshape(N_GROUPS, pc), b_im.reshape(N_GROUPS, pc))


def _block_diag(w):
    gpb = N_GROUPS // SSM_BLOCKS
    a, b = w.shape[1], w.shape[2]
    w = w.reshape(SSM_BLOCKS, gpb, a, 1, b)
    eye = jnp.eye(gpb, dtype=bool)[None, :, None, :, None]
    return jnp.where(eye, w, 0.0).reshape(SSM_BLOCKS, gpb * a, gpb * b)


def _s5_kernel(x_ref, g_ref, s0_re_ref, s0_im_ref, lb_re_ref, lb_im_ref,
               wb_re_ref, wb_im_ref, wc_re_ref, wc_im_ref, d_ref, wglu_ref,
               o_ref, new_re_ref, new_im_ref,
               st_re, st_im, bu_re, bu_im, *, steps, n_seq, carry_over_grid):
    rows = steps * n_seq
    if carry_over_grid:
        @pl.when(pl.program_id(0) == 0)
        def _():
            st_re[...] = s0_re_ref[...]
            st_im[...] = s0_im_ref[...]
    else:
        st_re[...] = s0_re_ref[...]
        st_im[...] = s0_im_ref[...]

    x = x_ref[...].reshape(rows, D_MODEL)
    u = _rmsnorm(x, g_ref[...])
    ub = u.astype(BF16)

    def scan_block(k):
        for c in range(SSM_BLOCK_ST // SCAN_LANES):
            lo = c * SCAN_LANES
            glo = k * SSM_BLOCK_ST + lo
            lr = jnp.broadcast_to(lb_re_ref[:, glo:glo + SCAN_LANES], (SUBLANES, SCAN_LANES))
            li = jnp.broadcast_to(lb_im_ref[:, glo:glo + SCAN_LANES], (SUBLANES, SCAN_LANES))

            def seq_tile(nb, carry):
                r_state = pl.multiple_of(nb * SUBLANES, SUBLANES)
                hr0 = st_re[pl.ds(r_state, SUBLANES), glo:glo + SCAN_LANES]
                hi0 = st_im[pl.ds(r_state, SUBLANES), glo:glo + SCAN_LANES]

                def step(t, h):
                    hr, hi = h
                    r = pl.multiple_of(t * n_seq + nb * SUBLANES, SUBLANES)
                    br = bu_re[pl.ds(r, SUBLANES), lo:lo + SCAN_LANES]
                    bi = bu_im[pl.ds(r, SUBLANES), lo:lo + SCAN_LANES]
                    nhr = lr * hr - li * hi + br
                    nhi = lr * hi + li * hr + bi
                    bu_re[pl.ds(r, SUBLANES), lo:lo + SCAN_LANES] = nhr
                    bu_im[pl.ds(r, SUBLANES), lo:lo + SCAN_LANES] = nhi
                    return nhr, nhi

                hr, hi = lax.fori_loop(0, steps, step, (hr0, hi0), unroll=8)
                st_re[pl.ds(r_state, SUBLANES), glo:glo + SCAN_LANES] = hr
                st_im[pl.ds(r_state, SUBLANES), glo:glo + SCAN_LANES] = hi
                return carry

            lax.fori_loop(0, n_seq // SUBLANES, seq_tile, 0)

    ys = []
    for k in range(SSM_BLOCKS):
        uk = ub[:, k * SSM_BLOCK_CH:(k + 1) * SSM_BLOCK_CH]
        bu_re[...] = _dot(uk, wb_re_ref[k])
        bu_im[...] = _dot(uk, wb_im_ref[k])
        scan_block(k)
        hr = bu_re[...].astype(BF16)
        hi = bu_im[...].astype(BF16)
        ys.append(_dot(hr, wc_re_ref[k]) - _dot(hi, wc_im_ref[k]))
    y = jnp.concatenate(ys, axis=-1) + d_ref[...] * u
    z = _dot(jax.nn.gelu(y).astype(BF16), wglu_ref[...])
    m = z[:, :D_MODEL] * jax.nn.sigmoid(z[:, D_MODEL:])
    o_ref[...] = (x + m).reshape(o_ref.shape)
    new_re_ref[...] = st_re[...]
    new_im_ref[...] = st_im[...]


def _s5_call(x, x_block, x_map, grid, s0_re, s0_im, state_map, g, lb_re, lb_im,
             wb_re, wb_im, wc_re, wc_im, d_skip, wglu, *, steps, n_seq, carry_over_grid):
    rows = steps * n_seq
    n_state = s0_re.shape[0]
    whole = lambda a: pl.BlockSpec(a.shape, lambda i: (0,) * a.ndim)
    state_spec = pl.BlockSpec((n_seq, S_DIM), state_map)
    g2 = g.reshape(1, D_MODEL)
    d2 = d_skip.reshape(1, D_MODEL)
    lbr = lb_re.reshape(1, S_DIM)
    lbi = lb_im.reshape(1, S_DIM)
    body = functools.partial(_s5_kernel, steps=steps, n_seq=n_seq, carry_over_grid=carry_over_grid)
    return pl.pallas_call(
        body,
        grid=grid,
        in_specs=[pl.BlockSpec(x_block, x_map), whole(g2), state_spec, state_spec,
                  whole(lbr), whole(lbi), whole(wb_re), whole(wb_im), whole(wc_re), whole(wc_im),
                  whole(d2), whole(wglu)],
        out_specs=(pl.BlockSpec(x_block, x_map), state_spec, state_spec),
        out_shape=(jax.ShapeDtypeStruct(x.shape, F32),
                   jax.ShapeDtypeStruct((n_state, S_DIM), F32),
                   jax.ShapeDtypeStruct((n_state, S_DIM), F32)),
        scratch_shapes=[pltpu.VMEM((n_seq, S_DIM), F32), pltpu.VMEM((n_seq, S_DIM), F32),
                        pltpu.VMEM((rows, SSM_BLOCK_ST), F32), pltpu.VMEM((rows, SSM_BLOCK_ST), F32)],
        compiler_params=pltpu.CompilerParams(
            dimension_semantics=("arbitrary",), vmem_limit_bytes=VMEM_LIMIT),
        name="s5_mixer",
    )(x, g2, s0_re, s0_im, lbr, lbi, wb_re, wb_im, wc_re, wc_im, d2, wglu)


def _conv_kernel(x_ref, g_ref, buf0_ref, win_ref, cw_ref, wout_ref, o_ref, newbuf_ref, zp,
                 *, steps, n_seq, carry_over_grid):
    rows = steps * n_seq
    halo = (CONV_W - 1) * n_seq
    if carry_over_grid:
        @pl.when(pl.program_id(0) == 0)
        def _():
            zp[0:halo, :] = buf0_ref[...].reshape(halo, D_MODEL)
    else:
        zp[0:halo, :] = buf0_ref[...].reshape(halo, D_MODEL)

    x = x_ref[...].reshape(rows, D_MODEL)
    h = _rmsnorm(x, g_ref[...]).astype(BF16)
    p = _dot(h, win_ref[...])
    gb = p[:, :D_MODEL]
    zp[halo:halo + rows, :] = p[:, D_MODEL:2 * D_MODEL] * p[:, 2 * D_MODEL:]
    conv = cw_ref[0:1, :] * zp[0:rows, :]
    for k in range(1, CONV_W):
        conv = conv + cw_ref[k:k + 1, :] * zp[k * n_seq:k * n_seq + rows, :]
    m = _dot((gb * conv).astype(BF16), wout_ref[...])
    o_ref[...] = (x + m).reshape(o_ref.shape)
    tail = zp[rows:rows + halo, :]
    zp[0:halo, :] = tail
    newbuf_ref[...] = tail.reshape(newbuf_ref.shape)


def _conv_call(x, x_block, x_map, grid, buf0, buf_block, buf_map, g, w_in, cw, w_out,
               *, steps, n_seq, carry_over_grid):
    rows = steps * n_seq
    halo = (CONV_W - 1) * n_seq
    whole = lambda a: pl.BlockSpec(a.shape, lambda i: (0,) * a.ndim)
    g2 = g.reshape(1, D_MODEL)
    body = functools.partial(_conv_kernel, steps=steps, n_seq=n_seq, carry_over_grid=carry_over_grid)
    return pl.pallas_call(
        body,
        grid=grid,
        in_specs=[pl.BlockSpec(x_block, x_map), whole(g2), pl.BlockSpec(buf_block, buf_map),
                  whole(w_in), whole(cw), whole(w_out)],
        out_specs=(pl.BlockSpec(x_block, x_map), pl.BlockSpec(buf_block, buf_map)),
        out_shape=(jax.ShapeDtypeStruct(x.shape, F32), jax.ShapeDtypeStruct(buf0.shape, F32)),
        scratch_shapes=[pltpu.VMEM((halo + rows, D_MODEL), F32)],
        compiler_params=pltpu.CompilerParams(
            dimension_semantics=("arbitrary",), vmem_limit_bytes=VMEM_LIMIT),
        name="conv_mixer",
    )(x, g2, buf0, w_in, cw, w_out)


S5_PROMPT_STEPS = 64
CONV_PROMPT_STEPS = 64
SAMPLE_SEQ_BLOCK = 32


def kernel(x_prompt, x_sample, state_ssm_re, state_ssm_im, cache_conv, norm_g, final_norm_g, ffn_w_gate_up, ffn_w_down, ssm_lam_re, ssm_lam_im, ssm_log_dt, ssm_b_re, ssm_b_im, ssm_c_re, ssm_c_im, ssm_d, ssm_w_glu, conv_w_in, conv_w, conv_w_out):
    nb_p, len_p, _ = x_prompt.shape
    nb_s, len_s, _ = x_sample.shape
    rows_p = nb_p * len_p
    rows_s = nb_s * len_s
    assert nb_p == SUBLANES and len_p % FFN_ROWS == 0 and rows_s == FFN_ROWS
    assert nb_s % SAMPLE_SEQ_BLOCK == 0

    w_gu = ffn_w_gate_up.astype(BF16)
    w_dn = ffn_w_down.astype(BF16)

    lb_re, lb_im, bb_re, bb_im = _ssm_prep(ssm_lam_re[0], ssm_lam_im[0], ssm_log_dt[0],
                                           ssm_b_re[0], ssm_b_im[0])
    to_gcp = lambda a: jnp.swapaxes(a.reshape(N_GROUPS, P_STATE, GROUP_SIZE), 1, 2)
    wb_re = _block_diag(to_gcp(bb_re)).astype(BF16)
    wb_im = _block_diag(to_gcp(bb_im)).astype(BF16)
    wc_re = _block_diag(jnp.swapaxes(ssm_c_re[0], 1, 2)).astype(BF16)
    wc_im = _block_diag(jnp.swapaxes(ssm_c_im[0], 1, 2)).astype(BF16)
    wglu = ssm_w_glu[0].astype(BF16)
    w_in = conv_w_in[0].astype(BF16)
    w_out = conv_w_out[0].astype(BF16)

    t_tiles = len_p // FFN_ROWS
    row_tiles_p = rows_p // FFN_ROWS
    flat_block = (FFN_ROWS, D_MODEL)
    flat_map = lambda i: (i, 0)

    ffn = functools.partial(_ffn_call, w_gu=w_gu, w_down=w_dn)
    seq_major_block = (nb_p, FFN_ROWS // nb_p, D_MODEL)
    seq_major_map = lambda i: (0, i, 0)
    xp = ffn(x_prompt, seq_major_block, seq_major_map, (rows_p, D_MODEL), flat_block, flat_map,
             (row_tiles_p,), g=norm_g[0, 0], which=(0, 0))
    zero_state = jnp.zeros((nb_p, S_DIM), F32)
    xp, sre_p, sim_p = _s5_call(
        xp, (S5_PROMPT_STEPS * nb_p, D_MODEL), flat_map, (len_p // S5_PROMPT_STEPS,),
        zero_state, zero_state, lambda i: (0, 0), norm_g[0, 1], lb_re, lb_im,
        wb_re, wb_im, wc_re, wc_im, ssm_d[0], wglu,
        steps=S5_PROMPT_STEPS, n_seq=nb_p, carry_over_grid=True)
    xp = ffn(xp, flat_block, flat_map, (rows_p, D_MODEL), flat_block, flat_map,
             (row_tiles_p,), g=norm_g[0, 2], which=(0, 1))
    xp = ffn(xp, flat_block, flat_map, (rows_p, D_MODEL), flat_block, flat_map,
             (row_tiles_p,), g=norm_g[1, 0], which=(1, 0))
    halo_p = (CONV_W - 1) * nb_p
    xp, buf_p = _conv_call(
        xp, (CONV_PROMPT_STEPS * nb_p, D_MODEL), flat_map, (len_p // CONV_PROMPT_STEPS,),
        jnp.zeros((halo_p, D_MODEL), F32), (halo_p, D_MODEL), lambda i: (0, 0),
        norm_g[1, 1], w_in, conv_w[0], w_out,
        steps=CONV_PROMPT_STEPS, n_seq=nb_p, carry_over_grid=True)
    y_prompt = ffn(xp, flat_block, flat_map, (nb_p, len_p, D_MODEL), seq_major_block, seq_major_map,
                   (row_tiles_p,), g=norm_g[1, 2], which=(1, 1), final_g=final_norm_g)

    xs = jnp.swapaxes(x_sample, 0, 1).reshape(rows_s, D_MODEL)
    one_tile = dict(x_block=flat_block, x_map=lambda i: (i, 0), out_shape=(rows_s, D_MODEL),
                    out_block=flat_block, out_map=lambda i: (i, 0), outer_grid=(1,))
    xs = ffn(xs, g=norm_g[0, 0], which=(0, 0), **one_tile)
    seq_block = (len_s, SAMPLE_SEQ_BLOCK, D_MODEL)
    seq_map = lambda i: (0, i, 0)
    seq_grid = (nb_s // SAMPLE_SEQ_BLOCK,)
    xs, sre_s, sim_s = _s5_call(
        xs.reshape(len_s, nb_s, D_MODEL), seq_block, seq_map, seq_grid,
        state_ssm_re[0].reshape(nb_s, S_DIM), state_ssm_im[0].reshape(nb_s, S_DIM), lambda i: (i, 0),
        norm_g[0, 1], lb_re, lb_im, wb_re, wb_im, wc_re, wc_im, ssm_d[0], wglu,
        steps=len_s, n_seq=SAMPLE_SEQ_BLOCK, carry_over_grid=False)
    xs = xs.reshape(rows_s, D_MODEL)
    xs = ffn(xs, g=norm_g[0, 2], which=(0, 1), **one_tile)
    xs = ffn(xs, g=norm_g[1, 0], which=(1, 0), **one_tile)
    buf0_s = jnp.swapaxes(cache_conv[0], 0, 1)
    xs, buf_s = _conv_call(
        xs.reshape(len_s, nb_s, D_MODEL), seq_block, seq_map, seq_grid,
        buf0_s, (CONV_W - 1, SAMPLE_SEQ_BLOCK, D_MODEL), seq_map,
        norm_g[1, 1], w_in, conv_w[0], w_out,
        steps=len_s, n_seq=SAMPLE_SEQ_BLOCK, carry_over_grid=False)
    xs = xs.reshape(rows_s, D_MODEL)
    ys = ffn(xs, g=norm_g[1, 2], which=(1, 1), final_g=final_norm_g, **one_tile)
    y_sample = jnp.swapaxes(ys.reshape(len_s, nb_s, D_MODEL), 0, 1)

    state4 = lambda s: s.reshape(1, -1, N_GROUPS, P_STATE)
    new_conv_p = jnp.swapaxes(buf_p.reshape(CONV_W - 1, nb_p, D_MODEL), 0, 1)[None]
    new_conv_s = jnp.swapaxes(buf_s, 0, 1)[None]
    return (y_prompt, y_sample, state4(sre_p), state4(sim_p), new_conv_p,
            state4(sre_s), state4(sim_s), new_conv_s)
```

```python
import functools

import jax
import jax.numpy as jnp
from jax import lax
from jax.experimental import pallas as pl
from jax.experimental.pallas import tpu as pltpu

F32 = jnp.float32
BF16 = jnp.bfloat16

D_MODEL = 1024
D_FF = 4 * D_MODEL
GROUP_SIZE = 16
N_GROUPS = D_MODEL // GROUP_SIZE
P_STATE = 64
S_DIM = N_GROUPS * P_STATE
CONV_W = 3
EPS = 1e-6

SUBLANES = 8
FFN_ROWS = 1024
FFN_COLS = 1024
FFN_ROW_BLOCK = 256
SSM_BLOCKS = 4
SSM_BLOCK_CH = D_MODEL // SSM_BLOCKS
SSM_BLOCK_ST = S_DIM // SSM_BLOCKS
SCAN_LANES = 512
VMEM_LIMIT = 56 * 1024 * 1024


def _rmsnorm(x, g):
    return x * lax.rsqrt(jnp.mean(x * x, axis=-1, keepdims=True) + EPS) * g


def _dot(a, b):
    return jnp.dot(a, b, preferred_element_type=F32)


def _ffn_kernel(x_ref, g_ref, wg_ref, wu_ref, wd_ref, *rest, chunk_axis, n_chunks, final_norm,
                x_seq_major, out_seq_major):
    rest = list(rest)
    gf_ref = rest.pop(0) if final_norm else None
    o_ref, h_ref, acc_ref = rest[:3]
    xt_ref = rest[3] if x_seq_major else x_ref
    j = pl.program_id(chunk_axis)

    @pl.when(j == 0)
    def _():
        if x_seq_major:
            for t in range(x_ref.shape[1]):
                xt_ref[t * SUBLANES:(t + 1) * SUBLANES, :] = x_ref[:, t, :]
        h_ref[...] = _rmsnorm(xt_ref[...], g_ref[...]).astype(BF16)
        acc_ref[...] = jnp.zeros_like(acc_ref)

    for r in range(0, FFN_ROWS, FFN_ROW_BLOCK):
        h = h_ref[r:r + FFN_ROW_BLOCK, :]
        gate = _dot(h, wg_ref[...])
        up = _dot(h, wu_ref[...])
        act = (jax.nn.silu(gate) * up).astype(BF16)
        acc_ref[r:r + FFN_ROW_BLOCK, :] += _dot(act, wd_ref[...])

    @pl.when(j == n_chunks - 1)
    def _():
        out = xt_ref[...] + 0.5 * acc_ref[...]
        if final_norm:
            out = _rmsnorm(out, gf_ref[...])
        if out_seq_major:
            for t in range(o_ref.shape[1]):
                o_ref[:, t, :] = out[t * SUBLANES:(t + 1) * SUBLANES, :]
        else:
            o_ref[...] = out


def _ffn_call(x, x_block, x_map, out_shape, out_block, out_map, outer_grid,
              g, w_gu, w_down, which, final_g=None):
    n_chunks = D_FF // FFN_COLS
    n_outer = len(outer_grid)
    layer, half = which

    x_seq_major = len(x_block) == 3
    out_seq_major = len(out_block) == 3
    body = functools.partial(_ffn_kernel, chunk_axis=n_outer, n_chunks=n_chunks,
                             final_norm=final_g is not None,
                             x_seq_major=x_seq_major, out_seq_major=out_seq_major)
    scratch = [pltpu.VMEM((FFN_ROWS, D_MODEL), BF16), pltpu.VMEM((FFN_ROWS, D_MODEL), F32)]
    if x_seq_major:
        scratch.append(pltpu.VMEM((FFN_ROWS, D_MODEL), F32))

    def outer(fn):
        return lambda *idx: fn(*idx[:n_outer])

    const2 = lambda *idx: (0, 0)
    in_specs = [
        pl.BlockSpec(x_block, outer(x_map)),
        pl.BlockSpec((1, D_MODEL), const2),
        pl.BlockSpec((None, None, D_MODEL, FFN_COLS), lambda *idx: (layer, half, 0, idx[n_outer])),
        pl.BlockSpec((None, None, D_MODEL, FFN_COLS),
                     lambda *idx: (layer, half, 0, idx[n_outer] + n_chunks)),
        pl.BlockSpec((None, None, FFN_COLS, D_MODEL), lambda *idx: (layer, half, idx[n_outer], 0)),
    ]
    args = [x, g.reshape(1, D_MODEL), w_gu, w_gu, w_down]
    if final_g is not None:
        in_specs.append(pl.BlockSpec((1, D_MODEL), const2))
        args.append(final_g.reshape(1, D_MODEL))
    return pl.pallas_call(
        body,
        grid=(*outer_grid, n_chunks),
        in_specs=in_specs,
        out_specs=pl.BlockSpec(out_block, outer(out_map)),
        out_shape=jax.ShapeDtypeStruct(out_shape, F32),
        scratch_shapes=scratch,
        compiler_params=pltpu.CompilerParams(
            dimension_semantics=("arbitrary",) * (n_outer + 1),
            vmem_limit_bytes=VMEM_LIMIT),
        name="ffn",
    )(*args)


def _ssm_prep_kernel(lam_re_ref, lam_im_ref, ldt_ref, lam_re_rep_ref, lam_im_rep_ref,
                     b_re_ref, b_im_ref, lb_re_ref, lb_im_ref, bb_re_ref, bb_im_ref):
    dt = jnp.exp(ldt_ref[...])

    def discretise(lam_re, lam_im):
        mag = jnp.exp(lam_re * dt)
        lb_re = mag * jnp.cos(lam_im * dt)
        lb_im = mag * jnp.sin(lam_im * dt)
        return lb_re, lb_im

    lb_re, lb_im = discretise(lam_re_ref[...], lam_im_ref[...])
    lb_re_ref[...] = lb_re
    lb_im_ref[...] = lb_im
    lam_re = lam_re_rep_ref[...]
    lam_im = lam_im_rep_ref[...]
    lbr, lbi = discretise(lam_re, lam_im)
    den = lam_re * lam_re + lam_im * lam_im
    nr = lbr - 1.0
    ni = lbi
    f_re = (nr * lam_re + ni * lam_im) / den
    f_im = (ni * lam_re - nr * lam_im) / den
    b_re = b_re_ref[...]
    b_im = b_im_ref[...]
    bb_re_ref[...] = f_re * b_re - f_im * b_im
    bb_im_ref[...] = f_re * b_im + f_im * b_re


def _ssm_prep(lam_re, lam_im, log_dt, b_re, b_im):
    pc = P_STATE * GROUP_SIZE
    rep = lambda a: jnp.repeat(a, GROUP_SIZE, axis=-1)
    shp = lambda n: jax.ShapeDtypeStruct((N_GROUPS, n), F32)
    return pl.pallas_call(
        _ssm_prep_kernel,
        out_shape=(shp(P_STATE), shp(P_STATE), shp(pc), shp(pc)),
        name="ssm_prep",
    )(lam_re, lam_im, log_dt.reshape(N_GROUPS, 1), rep(lam_re), rep(lam_im),
      b_re.reshape(N_GROUPS, pc), b_im.reshape(N_GROUPS, pc))


def _block_diag(w):
    gpb = N_GROUPS // SSM_BLOCKS
    a, b = w.shape[1], w.shape[2]
    w = w.reshape(SSM_BLOCKS, gpb, a, 1, b)
    eye = jnp.eye(gpb, dtype=bool)[None, :, None, :, None]
    return jnp.where(eye, w, 0.0).reshape(SSM_BLOCKS, gpb * a, gpb * b)


def _s5_kernel(x_ref, g_ref, s0_re_ref, s0_im_ref, lb_re_ref, lb_im_ref,
               wb_re_ref, wb_im_ref, wc_re_ref, wc_im_ref, d_ref, wglu_ref,
               o_ref, new_re_ref, new_im_ref,
               st_re, st_im, bu_re, bu_im, *, steps, n_seq, carry_over_grid):
    rows = steps * n_seq
    if carry_over_grid:
        @pl.when(pl.program_id(0) == 0)
        def _():
            st_re[...] = s0_re_ref[...]
            st_im[...] = s0_im_ref[...]
    else:
        st_re[...] = s0_re_ref[...]
        st_im[...] = s0_im_ref[...]

    x = x_ref[...].reshape(rows, D_MODEL)
    u = _rmsnorm(x, g_ref[...])
    ub = u.astype(BF16)

    def scan_block(k):
        for c in range(SSM_BLOCK_ST // SCAN_LANES):
            lo = c * SCAN_LANES
            glo = k * SSM_BLOCK_ST + lo
            lr = jnp.broadcast_to(lb_re_ref[:, glo:glo + SCAN_LANES], (SUBLANES, SCAN_LANES))
            li = jnp.broadcast_to(lb_im_ref[:, glo:glo + SCAN_LANES], (SUBLANES, SCAN_LANES))

            def seq_tile(nb, carry):
                r_state = pl.multiple_of(nb * SUBLANES, SUBLANES)
                hr0 = st_re[pl.ds(r_state, SUBLANES), glo:glo + SCAN_LANES]
                hi0 = st_im[pl.ds(r_state, SUBLANES), glo:glo + SCAN_LANES]

                def step(t, h):
                    hr, hi = h
                    r = pl.multiple_of(t * n_seq + nb * SUBLANES, SUBLANES)
                    br = bu_re[pl.ds(r, SUBLANES), lo:lo + SCAN_LANES]
                    bi = bu_im[pl.ds(r, SUBLANES), lo:lo + SCAN_LANES]
                    nhr = lr * hr - li * hi + br
                    nhi = lr * hi + li * hr + bi
                    bu_re[pl.ds(r, SUBLANES), lo:lo + SCAN_LANES] = nhr
                    bu_im[pl.ds(r, SUBLANES), lo:lo + SCAN_LANES] = nhi
                    return nhr, nhi

                hr, hi = lax.fori_loop(0, steps, step, (hr0, hi0), unroll=8)
                st_re[pl.ds(r_state, SUBLANES), glo:glo + SCAN_LANES] = hr
                st_im[pl.ds(r_state, SUBLANES), glo:glo + SCAN_LANES] = hi
                return carry

            lax.fori_loop(0, n_seq // SUBLANES, seq_tile, 0)

    ys = []
    for k in range(SSM_BLOCKS):
        uk = ub[:, k * SSM_BLOCK_CH:(k + 1) * SSM_BLOCK_CH]
        bu_re[...] = _dot(uk, wb_re_ref[k])
        bu_im[...] = _dot(uk, wb_im_ref[k])
        scan_block(k)
        hr = bu_re[...].astype(BF16)
        hi = bu_im[...].astype(BF16)
        ys.append(_dot(hr, wc_re_ref[k]) - _dot(hi, wc_im_ref[k]))
    y = jnp.concatenate(ys, axis=-1) + d_ref[...] * u
    z = _dot(jax.nn.gelu(y).astype(BF16), wglu_ref[...])
    m = z[:, :D_MODEL] * jax.nn.sigmoid(z[:, D_MODEL:])
    o_ref[...] = (x + m).reshape(o_ref.shape)
    new_re_ref[...] = st_re[...]
    new_im_ref[...] = st_im[...]


def _s5_call(x, x_block, x_map, grid, s0_re, s0_im, state_map, g, lb_re, lb_im,
             wb_re, wb_im, wc_re, wc_im, d_skip, wglu, *, steps, n_seq, carry_over_grid):
    rows = steps * n_seq
    n_state = s0_re.shape[0]
    whole = lambda a: pl.BlockSpec(a.shape, lambda i: (0,) * a.ndim)
    state_spec = pl.BlockSpec((n_seq, S_DIM), state_map)
    g2 = g.reshape(1, D_MODEL)
    d2 = d_skip.reshape(1, D_MODEL)
    lbr = lb_re.reshape(1, S_DIM)
    lbi = lb_im.reshape(1, S_DIM)
    body = functools.partial(_s5_kernel, steps=steps, n_seq=n_seq, carry_over_grid=carry_over_grid)
    return pl.pallas_call(
        body,
        grid=grid,
        in_specs=[pl.BlockSpec(x_block, x_map), whole(g2), state_spec, state_spec,
                  whole(lbr), whole(lbi), whole(wb_re), whole(wb_im), whole(wc_re), whole(wc_im),
                  whole(d2), whole(wglu)],
        out_specs=(pl.BlockSpec(x_block, x_map), state_spec, state_spec),
        out_shape=(jax.ShapeDtypeStruct(x.shape, F32),
                   jax.ShapeDtypeStruct((n_state, S_DIM), F32),
                   jax.ShapeDtypeStruct((n_state, S_DIM), F32)),
        scratch_shapes=[pltpu.VMEM((n_seq, S_DIM), F32), pltpu.VMEM((n_seq, S_DIM), F32),
                        pltpu.VMEM((rows, SSM_BLOCK_ST), F32), pltpu.VMEM((rows, SSM_BLOCK_ST), F32)],
        compiler_params=pltpu.CompilerParams(
            dimension_semantics=("arbitrary",), vmem_limit_bytes=VMEM_LIMIT),
        name="s5_mixer",
    )(x, g2, s0_re, s0_im, lbr, lbi, wb_re, wb_im, wc_re, wc_im, d2, wglu)


def _conv_kernel(x_ref, g_ref, buf0_ref, win_ref, cw_ref, wout_ref, o_ref, newbuf_ref, zp,
                 *, steps, n_seq, carry_over_grid):
    rows = steps * n_seq
    halo = (CONV_W - 1) * n_seq
    if carry_over_grid:
        @pl.when(pl.program_id(0) == 0)
        def _():
            zp[0:halo, :] = buf0_ref[...].reshape(halo, D_MODEL)
    else:
        zp[0:halo, :] = buf0_ref[...].reshape(halo, D_MODEL)

    x = x_ref[...].reshape(rows, D_MODEL)
    h = _rmsnorm(x, g_ref[...]).astype(BF16)
    p = _dot(h, win_ref[...])
    gb = p[:, :D_MODEL]
    zp[halo:halo + rows, :] = p[:, D_MODEL:2 * D_MODEL] * p[:, 2 * D_MODEL:]
    conv = cw_ref[0:1, :] * zp[0:rows, :]
    for k in range(1, CONV_W):
        conv = conv + cw_ref[k:k + 1, :] * zp[k * n_seq:k * n_seq + rows, :]
    m = _dot((gb * conv).astype(BF16), wout_ref[...])
    o_ref[...] = (x + m).reshape(o_ref.shape)
    tail = zp[rows:rows + halo, :]
    zp[0:halo, :] = tail
    newbuf_ref[...] = tail.reshape(newbuf_ref.shape)


def _conv_call(x, x_block, x_map, grid, buf0, buf_block, buf_map, g, w_in, cw, w_out,
               *, steps, n_seq, carry_over_grid):
    rows = steps * n_seq
    halo = (CONV_W - 1) * n_seq
    whole = lambda a: pl.BlockSpec(a.shape, lambda i: (0,) * a.ndim)
    g2 = g.reshape(1, D_MODEL)
    body = functools.partial(_conv_kernel, steps=steps, n_seq=n_seq, carry_over_grid=carry_over_grid)
    return pl.pallas_call(
        body,
        grid=grid,
        in_specs=[pl.BlockSpec(x_block, x_map), whole(g2), pl.BlockSpec(buf_block, buf_map),
                  whole(w_in), whole(cw), whole(w_out)],
        out_specs=(pl.BlockSpec(x_block, x_map), pl.BlockSpec(buf_block, buf_map)),
        out_shape=(jax.ShapeDtypeStruct(x.shape, F32), jax.ShapeDtypeStruct(buf0.shape, F32)),
        scratch_shapes=[pltpu.VMEM((halo + rows, D_MODEL), F32)],
        compiler_params=pltpu.CompilerParams(
            dimension_semantics=("arbitrary",), vmem_limit_bytes=VMEM_LIMIT),
        name="conv_mixer",
    )(x, g2, buf0, w_in, cw, w_out)


S5_PROMPT_STEPS = 64
CONV_PROMPT_STEPS = 64
SAMPLE_SEQ_BLOCK = 32


def kernel(x_prompt, x_sample, state_ssm_re, state_ssm_im, cache_conv, norm_g, final_norm_g, ffn_w_gate_up, ffn_w_down, ssm_lam_re, ssm_lam_im, ssm_log_dt, ssm_b_re, ssm_b_im, ssm_c_re, ssm_c_im, ssm_d, ssm_w_glu, conv_w_in, conv_w, conv_w_out):
    nb_p, len_p, _ = x_prompt.shape
    nb_s, len_s, _ = x_sample.shape
    rows_p = nb_p * len_p
    rows_s = nb_s * len_s
    assert nb_p == SUBLANES and len_p % FFN_ROWS == 0 and rows_s == FFN_ROWS
    assert nb_s % SAMPLE_SEQ_BLOCK == 0

    w_gu = ffn_w_gate_up.astype(BF16)
    w_dn = ffn_w_down.astype(BF16)

    lb_re, lb_im, bb_re, bb_im = _ssm_prep(ssm_lam_re[0], ssm_lam_im[0], ssm_log_dt[0],
                                           ssm_b_re[0], ssm_b_im[0])
    to_gcp = lambda a: jnp.swapaxes(a.reshape(N_GROUPS, P_STATE, GROUP_SIZE), 1, 2)
    wb_re = _block_diag(to_gcp(bb_re)).astype(BF16)
    wb_im = _block_diag(to_gcp(bb_im)).astype(BF16)
    wc_re = _block_diag(jnp.swapaxes(ssm_c_re[0], 1, 2)).astype(BF16)
    wc_im = _block_diag(jnp.swapaxes(ssm_c_im[0], 1, 2)).astype(BF16)
    wglu = ssm_w_glu[0].astype(BF16)
    w_in = conv_w_in[0].astype(BF16)
    w_out = conv_w_out[0].astype(BF16)

    t_tiles = len_p // FFN_ROWS
    row_tiles_p = rows_p // FFN_ROWS
    flat_block = (FFN_ROWS, D_MODEL)
    flat_map = lambda i: (i, 0)

    ffn = functools.partial(_ffn_call, w_gu=w_gu, w_down=w_dn)
    seq_major_block = (nb_p, FFN_ROWS // nb_p, D_MODEL)
    seq_major_map = lambda i: (0, i, 0)
    xp = ffn(x_prompt, seq_major_block, seq_major_map, (rows_p, D_MODEL), flat_block, flat_map,
             (row_tiles_p,), g=norm_g[0, 0], which=(0, 0))
    zero_state = jnp.zeros((nb_p, S_DIM), F32)
    xp, sre_p, sim_p = _s5_call(
        xp, (S5_PROMPT_STEPS * nb_p, D_MODEL), flat_map, (len_p // S5_PROMPT_STEPS,),
        zero_state, zero_state, lambda i: (0, 0), norm_g[0, 1], lb_re, lb_im,
        wb_re, wb_im, wc_re, wc_im, ssm_d[0], wglu,
        steps=S5_PROMPT_STEPS, n_seq=nb_p, carry_over_grid=True)
    xp = ffn(xp, flat_block, flat_map, (rows_p, D_MODEL), flat_block, flat_map,
             (row_tiles_p,), g=norm_g[0, 2], which=(0, 1))
    xp = ffn(xp, flat_block, flat_map, (rows_p, D_MODEL), flat_block, flat_map,
             (row_tiles_p,), g=norm_g[1, 0], which=(1, 0))
    halo_p = (CONV_W - 1) * nb_p
    xp, buf_p = _conv_call(
        xp, (CONV_PROMPT_STEPS * nb_p, D_MODEL), flat_map, (len_p // CONV_PROMPT_STEPS,),
        jnp.zeros((halo_p, D_MODEL), F32), (halo_p, D_MODEL), lambda i: (0, 0),
        norm_g[1, 1], w_in, conv_w[0], w_out,
        steps=CONV_PROMPT_STEPS, n_seq=nb_p, carry_over_grid=True)
    y_prompt = ffn(xp, flat_block, flat_map, (nb_p, len_p, D_MODEL), seq_major_block, seq_major_map,
                   (row_tiles_p,), g=norm_g[1, 2], which=(1, 1), final_g=final_norm_g)

    xs = jnp.swapaxes(x_sample, 0, 1).reshape(rows_s, D_MODEL)
    one_tile = dict(x_block=flat_block, x_map=lambda i: (i, 0), out_shape=(rows_s, D_MODEL),
                    out_block=flat_block, out_map=lambda i: (i, 0), outer_grid=(1,))
    xs = ffn(xs, g=norm_g[0, 0], which=(0, 0), **one_tile)
    seq_block = (len_s, SAMPLE_SEQ_BLOCK, D_MODEL)
    seq_map = lambda i: (0, i, 0)
    seq_grid = (nb_s // SAMPLE_SEQ_BLOCK,)
    xs, sre_s, sim_s = _s5_call(
        xs.reshape(len_s, nb_s, D_MODEL), seq_block, seq_map, seq_grid,
        state_ssm_re[0].reshape(nb_s, S_DIM), state_ssm_im[0].reshape(nb_s, S_DIM), lambda i: (i, 0),
        norm_g[0, 1], lb_re, lb_im, wb_re, wb_im, wc_re, wc_im, ssm_d[0], wglu,
        steps=len_s, n_seq=SAMPLE_SEQ_BLOCK, carry_over_grid=False)
    xs = xs.reshape(rows_s, D_MODEL)
    xs = ffn(xs, g=norm_g[0, 2], which=(0, 1), **one_tile)
    xs = ffn(xs, g=norm_g[1, 0], which=(1, 0), **one_tile)
    buf0_s = jnp.swapaxes(cache_conv[0], 0, 1)
    xs, buf_s = _conv_call(
        xs.reshape(len_s, nb_s, D_MODEL), seq_block, seq_map, seq_grid,
        buf0_s, (CONV_W - 1, SAMPLE_SEQ_BLOCK, D_MODEL), seq_map,
        norm_g[1, 1], w_in, conv_w[0], w_out,
        steps=len_s, n_seq=SAMPLE_SEQ_BLOCK, carry_over_grid=False)
    xs = xs.reshape(rows_s, D_MODEL)
    ys = ffn(xs, g=norm_g[1, 2], which=(1, 1), final_g=final_norm_g, **one_tile)
    y_sample = jnp.swapaxes(ys.reshape(len_s, nb_s, D_MODEL), 0, 1)

    state4 = lambda s: s.reshape(1, -1, N_GROUPS, P_STATE)
    new_conv_p = jnp.swapaxes(buf_p.reshape(CONV_W - 1, nb_p, D_MODEL), 0, 1)[None]
    new_conv_s = jnp.swapaxes(buf_s, 0, 1)[None]
    return (y_prompt, y_sample, state4(sre_p), state4(sim_p), new_conv_p,
            state4(sre_s), state4(sim_s), new_conv_s)
```

```python
import functools

import jax
import jax.numpy as jnp
from jax import lax
from jax.experimental import pallas as pl
from jax.experimental.pallas import tpu as pltpu

F32 = jnp.float32
BF16 = jnp.bfloat16

D_MODEL = 1024
D_FF = 4 * D_MODEL
GROUP_SIZE = 16
N_GROUPS = D_MODEL // GROUP_SIZE
P_STATE = 64
S_DIM = N_GROUPS * P_STATE
CONV_W = 3
EPS = 1e-6

SUBLANES = 8
FFN_ROWS = 1024
FFN_COLS = 1024
FFN_ROW_BLOCK = 256
SSM_BLOCKS = 4
SSM_BLOCK_CH = D_MODEL // SSM_BLOCKS
SSM_BLOCK_ST = S_DIM // SSM_BLOCKS
SCAN_LANES = 512
VMEM_LIMIT = 56 * 1024 * 1024


def _rmsnorm(x, g):
    return x * lax.rsqrt(jnp.mean(x * x, axis=-1, keepdims=True) + EPS) * g


def _dot(a, b):
    return jnp.dot(a, b, preferred_element_type=F32)


def _ffn_kernel(x_ref, g_ref, wg_ref, wu_ref, wd_ref, *rest, chunk_axis, n_chunks, final_norm,
                x_seq_major, out_seq_major):
    rest = list(rest)
    gf_ref = rest.pop(0) if final_norm else None
    o_ref, h_ref, acc_ref = rest[:3]
    xt_ref = rest[3] if x_seq_major else x_ref
    j = pl.program_id(chunk_axis)

    @pl.when(j == 0)
    def _():
        if x_seq_major:
            for t in range(x_ref.shape[1]):
                xt_ref[t * SUBLANES:(t + 1) * SUBLANES, :] = x_ref[:, t, :]
        h_ref[...] = _rmsnorm(xt_ref[...], g_ref[...]).astype(BF16)
        acc_ref[...] = jnp.zeros_like(acc_ref)

    for r in range(0, FFN_ROWS, FFN_ROW_BLOCK):
        h = h_ref[r:r + FFN_ROW_BLOCK, :]
        gate = _dot(h, wg_ref[...])
        up = _dot(h, wu_ref[...])
        act = (jax.nn.silu(gate) * up).astype(BF16)
        acc_ref[r:r + FFN_ROW_BLOCK, :] += _dot(act, wd_ref[...])

    @pl.when(j == n_chunks - 1)
    def _():
        out = xt_ref[...] + 0.5 * acc_ref[...]
        if final_norm:
            out = _rmsnorm(out, gf_ref[...])
        if out_seq_major:
            for t in range(o_ref.shape[1]):
                o_ref[:, t, :] = out[t * SUBLANES:(t + 1) * SUBLANES, :]
        else:
            o_ref[...] = out


def _ffn_call(x, x_block, x_map, out_shape, out_block, out_map, outer_grid,
              g, w_gu, w_down, which, final_g=None):
    n_chunks = D_FF // FFN_COLS
    n_outer = len(outer_grid)
    layer, half = which

    x_seq_major = len(x_block) == 3
    out_seq_major = len(out_block) == 3
    body = functools.partial(_ffn_kernel, chunk_axis=n_outer, n_chunks=n_chunks,
                             final_norm=final_g is not None,
                             x_seq_major=x_seq_major, out_seq_major=out_seq_major)
    scratch = [pltpu.VMEM((FFN_ROWS, D_MODEL), BF16), pltpu.VMEM((FFN_ROWS, D_MODEL), F32)]
    if x_seq_major:
        scratch.append(pltpu.VMEM((FFN_ROWS, D_MODEL), F32))

    def outer(fn):
        return lambda *idx: fn(*idx[:n_outer])

    const2 = lambda *idx: (0, 0)
    in_specs = [
        pl.BlockSpec(x_block, outer(x_map)),
        pl.BlockSpec((1, D_MODEL), const2),
        pl.BlockSpec((None, None, D_MODEL, FFN_COLS), lambda *idx: (layer, half, 0, idx[n_outer])),
        pl.BlockSpec((None, None, D_MODEL, FFN_COLS),
                     lambda *idx: (layer, half, 0, idx[n_outer] + n_chunks)),
        pl.BlockSpec((None, None, FFN_COLS, D_MODEL), lambda *idx: (layer, half, idx[n_outer], 0)),
    ]
    args = [x, g.reshape(1, D_MODEL), w_gu, w_gu, w_down]
    if final_g is not None:
        in_specs.append(pl.BlockSpec((1, D_MODEL), const2))
        args.append(final_g.reshape(1, D_MODEL))
    return pl.pallas_call(
        body,
        grid=(*outer_grid, n_chunks),
        in_specs=in_specs,
        out_specs=pl.BlockSpec(out_block, outer(out_map)),
        out_shape=jax.ShapeDtypeStruct(out_shape, F32),
        scratch_shapes=scratch,
        compiler_params=pltpu.CompilerParams(
            dimension_semantics=("arbitrary",) * (n_outer + 1),
            vmem_limit_bytes=VMEM_LIMIT),
        name="ffn",
    )(*args)


def _ssm_prep_kernel(lam_re_ref, lam_im_ref, ldt_ref, lam_re_rep_ref, lam_im_rep_ref,
                     b_re_ref, b_im_ref, lb_re_ref, lb_im_ref, bb_re_ref, bb_im_ref):
    dt = jnp.exp(ldt_ref[...])

    def discretise(lam_re, lam_im):
        mag = jnp.exp(lam_re * dt)
        lb_re = mag * jnp.cos(lam_im * dt)
        lb_im = mag * jnp.sin(lam_im * dt)
        return lb_re, lb_im

    lb_re, lb_im = discretise(lam_re_ref[...], lam_im_ref[...])
    lb_re_ref[...] = lb_re
    lb_im_ref[...] = lb_im
    lam_re = lam_re_rep_ref[...]
    lam_im = lam_im_rep_ref[...]
    lbr, lbi = discretise(lam_re, lam_im)
    den = lam_re * lam_re + lam_im * lam_im
    nr = lbr - 1.0
    ni = lbi
    f_re = (nr * lam_re + ni * lam_im) / den
    f_im = (ni * lam_re - nr * lam_im) / den
    b_re = b_re_ref[...]
    b_im = b_im_ref[...]
    bb_re_ref[...] = f_re * b_re - f_im * b_im
    bb_im_ref[...] = f_re * b_im + f_im * b_re


def _ssm_prep(lam_re, lam_im, log_dt, b_re, b_im):
    pc = P_STATE * GROUP_SIZE
    rep = lambda a: jnp.repeat(a, GROUP_SIZE, axis=-1)
    shp = lambda n: jax.ShapeDtypeStruct((N_GROUPS, n), F32)
    return pl.pallas_call(
        _ssm_prep_kernel,
        out_shape=(shp(P_STATE), shp(P_STATE), shp(pc), shp(pc)),
        name="ssm_prep",
    )(lam_re, lam_im, log_dt.reshape(N_GROUPS, 1), rep(lam_re), rep(lam_im),
      b_re.reshape(N_GROUPS, pc), b_im.reshape(N_GROUPS, pc))


def _block_diag(w):
    gpb = N_GROUPS // SSM_BLOCKS
    a, b = w.shape[1], w.shape[2]
    w = w.reshape(SSM_BLOCKS, gpb, a, 1, b)
    eye = jnp.eye(gpb, dtype=bool)[None, :, None, :, None]
    return jnp.where(eye, w, 0.0).reshape(SSM_BLOCKS, gpb * a, gpb * b)


def _s5_kernel(x_ref, g_ref, s0_re_ref, s0_im_ref, lb_re_ref, lb_im_ref,
               wb_re_ref, wb_im_ref, wc_re_ref, wc_im_ref, d_ref, wglu_ref,
               o_ref, new_re_ref, new_im_ref,
               st_re, st_im, bu_re, bu_im, *, steps, n_seq, carry_over_grid):
    rows = steps * n_seq
    if carry_over_grid:
        @pl.when(pl.program_id(0) == 0)
        def _():
            st_re[...] = s0_re_ref[...]
            st_im[...] = s0_im_ref[...]
    else:
        st_re[...] = s0_re_ref[...]
        st_im[...] = s0_im_ref[...]

    x = x_ref[...].reshape(rows, D_MODEL)
    u = _rmsnorm(x, g_ref[...])
    ub = u.astype(BF16)

    def project_in(k):
        uk = ub[:, k * SSM_BLOCK_CH:(k + 1) * SSM_BLOCK_CH]
        bu_re[k % 2] = _dot(uk, wb_re_ref[k])
        bu_im[k % 2] = _dot(uk, wb_im_ref[k])

    def scan_block(k):
        b_re, b_im = bu_re.at[k % 2], bu_im.at[k % 2]
        for c in range(SSM_BLOCK_ST // SCAN_LANES):
            lo = c * SCAN_LANES
            glo = k * SSM_BLOCK_ST + lo
            lr = jnp.broadcast_to(lb_re_ref[:, glo:glo + SCAN_LANES], (SUBLANES, SCAN_LANES))
            li = jnp.broadcast_to(lb_im_ref[:, glo:glo + SCAN_LANES], (SUBLANES, SCAN_LANES))
            for r_state in range(0, n_seq, SUBLANES):
                hr = st_re[r_state:r_state + SUBLANES, glo:glo + SCAN_LANES]
                hi = st_im[r_state:r_state + SUBLANES, glo:glo + SCAN_LANES]
                for t in range(steps):
                    r = t * n_seq + r_state
                    br = b_re[r:r + SUBLANES, lo:lo + SCAN_LANES]
                    bi = b_im[r:r + SUBLANES, lo:lo + SCAN_LANES]
                    hr, hi = lr * hr - li * hi + br, lr * hi + li * hr + bi
                    b_re[r:r + SUBLANES, lo:lo + SCAN_LANES] = hr
                    b_im[r:r + SUBLANES, lo:lo + SCAN_LANES] = hi
                st_re[r_state:r_state + SUBLANES, glo:glo + SCAN_LANES] = hr
                st_im[r_state:r_state + SUBLANES, glo:glo + SCAN_LANES] = hi

    ys = []
    project_in(0)
    for k in range(SSM_BLOCKS):
        if k + 1 < SSM_BLOCKS:
            project_in(k + 1)
        scan_block(k)
        hr = bu_re[k % 2].astype(BF16)
        hi = bu_im[k % 2].astype(BF16)
        ys.append(_dot(hr, wc_re_ref[k]) - _dot(hi, wc_im_ref[k]))
    y = jnp.concatenate(ys, axis=-1) + d_ref[...] * u
    z = _dot(jax.nn.gelu(y).astype(BF16), wglu_ref[...])
    m = z[:, :D_MODEL] * jax.nn.sigmoid(z[:, D_MODEL:])
    o_ref[...] = (x + m).reshape(o_ref.shape)
    new_re_ref[...] = st_re[...]
    new_im_ref[...] = st_im[...]


def _s5_call(x, x_block, x_map, grid, s0_re, s0_im, state_map, g, lb_re, lb_im,
             wb_re, wb_im, wc_re, wc_im, d_skip, wglu, *, steps, n_seq, carry_over_grid):
    rows = steps * n_seq
    n_state = s0_re.shape[0]
    whole = lambda a: pl.BlockSpec(a.shape, lambda i: (0,) * a.ndim)
    state_spec = pl.BlockSpec((n_seq, S_DIM), state_map)
    g2 = g.reshape(1, D_MODEL)
    d2 = d_skip.reshape(1, D_MODEL)
    lbr = lb_re.reshape(1, S_DIM)
    lbi = lb_im.reshape(1, S_DIM)
    body = functools.partial(_s5_kernel, steps=steps, n_seq=n_seq, carry_over_grid=carry_over_grid)
    return pl.pallas_call(
        body,
        grid=grid,
        in_specs=[pl.BlockSpec(x_block, x_map), whole(g2), state_spec, state_spec,
                  whole(lbr), whole(lbi), whole(wb_re), whole(wb_im), whole(wc_re), whole(wc_im),
                  whole(d2), whole(wglu)],
        out_specs=(pl.BlockSpec(x_block, x_map), state_spec, state_spec),
        out_shape=(jax.ShapeDtypeStruct(x.shape, F32),
                   jax.ShapeDtypeStruct((n_state, S_DIM), F32),
                   jax.ShapeDtypeStruct((n_state, S_DIM), F32)),
        scratch_shapes=[pltpu.VMEM((n_seq, S_DIM), F32), pltpu.VMEM((n_seq, S_DIM), F32),
                        pltpu.VMEM((2, rows, SSM_BLOCK_ST), F32), pltpu.VMEM((2, rows, SSM_BLOCK_ST), F32)],
        compiler_params=pltpu.CompilerParams(
            dimension_semantics=("arbitrary",), vmem_limit_bytes=VMEM_LIMIT),
        name="s5_mixer",
    )(x, g2, s0_re, s0_im, lbr, lbi, wb_re, wb_im, wc_re, wc_im, d2, wglu)


def _conv_kernel(x_ref, g_ref, buf0_ref, win_ref, cw_ref, wout_ref, o_ref, newbuf_ref, zp,
                 *, steps, n_seq, carry_over_grid):
    rows = steps * n_seq
    halo = (CONV_W - 1) * n_seq
    if carry_over_grid:
        @pl.when(pl.program_id(0) == 0)
        def _():
            zp[0:halo, :] = buf0_ref[...].reshape(halo, D_MODEL)
    else:
        zp[0:halo, :] = buf0_ref[...].reshape(halo, D_MODEL)

    x = x_ref[...].reshape(rows, D_MODEL)
    h = _rmsnorm(x, g_ref[...]).astype(BF16)
    p = _dot(h, win_ref[...])
    gb = p[:, :D_MODEL]
    zp[halo:halo + rows, :] = p[:, D_MODEL:2 * D_MODEL] * p[:, 2 * D_MODEL:]
    conv = cw_ref[0:1, :] * zp[0:rows, :]
    for k in range(1, CONV_W):
        conv = conv + cw_ref[k:k + 1, :] * zp[k * n_seq:k * n_seq + rows, :]
    m = _dot((gb * conv).astype(BF16), wout_ref[...])
    o_ref[...] = (x + m).reshape(o_ref.shape)
    tail = zp[rows:rows + halo, :]
    zp[0:halo, :] = tail
    newbuf_ref[...] = tail.reshape(newbuf_ref.shape)


def _conv_call(x, x_block, x_map, grid, buf0, buf_block, buf_map, g, w_in, cw, w_out,
               *, steps, n_seq, carry_over_grid):
    rows = steps * n_seq
    halo = (CONV_W - 1) * n_seq
    whole = lambda a: pl.BlockSpec(a.shape, lambda i: (0,) * a.ndim)
    g2 = g.reshape(1, D_MODEL)
    body = functools.partial(_conv_kernel, steps=steps, n_seq=n_seq, carry_over_grid=carry_over_grid)
    return pl.pallas_call(
        body,
        grid=grid,
        in_specs=[pl.BlockSpec(x_block, x_map), whole(g2), pl.BlockSpec(buf_block, buf_map),
                  whole(w_in), whole(cw), whole(w_out)],
        out_specs=(pl.BlockSpec(x_block, x_map), pl.BlockSpec(buf_block, buf_map)),
        out_shape=(jax.ShapeDtypeStruct(x.shape, F32), jax.ShapeDtypeStruct(buf0.shape, F32)),
        scratch_shapes=[pltpu.VMEM((halo + rows, D_MODEL), F32)],
        compiler_params=pltpu.CompilerParams(
            dimension_semantics=("arbitrary",), vmem_limit_bytes=VMEM_LIMIT),
        name="conv_mixer",
    )(x, g2, buf0, w_in, cw, w_out)


S5_PROMPT_STEPS = 64
CONV_PROMPT_STEPS = 64
SAMPLE_SEQ_BLOCK = 32


def kernel(x_prompt, x_sample, state_ssm_re, state_ssm_im, cache_conv, norm_g, final_norm_g, ffn_w_gate_up, ffn_w_down, ssm_lam_re, ssm_lam_im, ssm_log_dt, ssm_b_re, ssm_b_im, ssm_c_re, ssm_c_im, ssm_d, ssm_w_glu, conv_w_in, conv_w, conv_w_out):
    nb_p, len_p, _ = x_prompt.shape
    nb_s, len_s, _ = x_sample.shape
    rows_p = nb_p * len_p
    rows_s = nb_s * len_s
    assert nb_p == SUBLANES and len_p % FFN_ROWS == 0 and rows_s == FFN_ROWS
    assert nb_s % SAMPLE_SEQ_BLOCK == 0

    w_gu = ffn_w_gate_up.astype(BF16)
    w_dn = ffn_w_down.astype(BF16)

    lb_re, lb_im, bb_re, bb_im = _ssm_prep(ssm_lam_re[0], ssm_lam_im[0], ssm_log_dt[0],
                                           ssm_b_re[0], ssm_b_im[0])
    to_gcp = lambda a: jnp.swapaxes(a.reshape(N_GROUPS, P_STATE, GROUP_SIZE), 1, 2)
    wb_re = _block_diag(to_gcp(bb_re)).astype(BF16)
    wb_im = _block_diag(to_gcp(bb_im)).astype(BF16)
    wc_re = _block_diag(jnp.swapaxes(ssm_c_re[0], 1, 2)).astype(BF16)
    wc_im = _block_diag(jnp.swapaxes(ssm_c_im[0], 1, 2)).astype(BF16)
    wglu = ssm_w_glu[0].astype(BF16)
    w_in = conv_w_in[0].astype(BF16)
    w_out = conv_w_out[0].astype(BF16)

    t_tiles = len_p // FFN_ROWS
    row_tiles_p = rows_p // FFN_ROWS
    flat_block = (FFN_ROWS, D_MODEL)
    flat_map = lambda i: (i, 0)

    ffn = functools.partial(_ffn_call, w_gu=w_gu, w_down=w_dn)
    seq_major_block = (nb_p, FFN_ROWS // nb_p, D_MODEL)
    seq_major_map = lambda i: (0, i, 0)
    xp = ffn(x_prompt, seq_major_block, seq_major_map, (rows_p, D_MODEL), flat_block, flat_map,
             (row_tiles_p,), g=norm_g[0, 0], which=(0, 0))
    zero_state = jnp.zeros((nb_p, S_DIM), F32)
    xp, sre_p, sim_p = _s5_call(
        xp, (S5_PROMPT_STEPS * nb_p, D_MODEL), flat_map, (len_p // S5_PROMPT_STEPS,),
        zero_state, zero_state, lambda i: (0, 0), norm_g[0, 1], lb_re, lb_im,
        wb_re, wb_im, wc_re, wc_im, ssm_d[0], wglu,
        steps=S5_PROMPT_STEPS, n_seq=nb_p, carry_over_grid=True)
    xp = ffn(xp, flat_block, flat_map, (rows_p, D_MODEL), flat_block, flat_map,
             (row_tiles_p,), g=norm_g[0, 2], which=(0, 1))
    xp = ffn(xp, flat_block, flat_map, (rows_p, D_MODEL), flat_block, flat_map,
             (row_tiles_p,), g=norm_g[1, 0], which=(1, 0))
    halo_p = (CONV_W - 1) * nb_p
    xp, buf_p = _conv_call(
        xp, (CONV_PROMPT_STEPS * nb_p, D_MODEL), flat_map, (len_p // CONV_PROMPT_STEPS,),
        jnp.zeros((halo_p, D_MODEL), F32), (halo_p, D_MODEL), lambda i: (0, 0),
        norm_g[1, 1], w_in, conv_w[0], w_out,
        steps=CONV_PROMPT_STEPS, n_seq=nb_p, carry_over_grid=True)
    y_prompt = ffn(xp, flat_block, flat_map, (nb_p, len_p, D_MODEL), seq_major_block, seq_major_map,
                   (row_tiles_p,), g=norm_g[1, 2], which=(1, 1), final_g=final_norm_g)

    xs = jnp.swapaxes(x_sample, 0, 1).reshape(rows_s, D_MODEL)
    one_tile = dict(x_block=flat_block, x_map=lambda i: (i, 0), out_shape=(rows_s, D_MODEL),
                    out_block=flat_block, out_map=lambda i: (i, 0), outer_grid=(1,))
    xs = ffn(xs, g=norm_g[0, 0], which=(0, 0), **one_tile)
    seq_block = (len_s, SAMPLE_SEQ_BLOCK, D_MODEL)
    seq_map = lambda i: (0, i, 0)
    seq_grid = (nb_s // SAMPLE_SEQ_BLOCK,)
    xs, sre_s, sim_s = _s5_call(
        xs.reshape(len_s, nb_s, D_MODEL), seq_block, seq_map, seq_grid,
        state_ssm_re[0].reshape(nb_s, S_DIM), state_ssm_im[0].reshape(nb_s, S_DIM), lambda i: (i, 0),
        norm_g[0, 1], lb_re, lb_im, wb_re, wb_im, wc_re, wc_im, ssm_d[0], wglu,
        steps=len_s, n_seq=SAMPLE_SEQ_BLOCK, carry_over_grid=False)
    xs = xs.reshape(rows_s, D_MODEL)
    xs = ffn(xs, g=norm_g[0, 2], which=(0, 1), **one_tile)
    xs = ffn(xs, g=norm_g[1, 0], which=(1, 0), **one_tile)
    buf0_s = jnp.swapaxes(cache_conv[0], 0, 1)
    xs, buf_s = _conv_call(
        xs.reshape(len_s, nb_s, D_MODEL), seq_block, seq_map, seq_grid,
        buf0_s, (CONV_W - 1, SAMPLE_SEQ_BLOCK, D_MODEL), seq_map,
        norm_g[1, 1], w_in, conv_w[0], w_out,
        steps=len_s, n_seq=SAMPLE_SEQ_BLOCK, carry_over_grid=False)
    xs = xs.reshape(rows_s, D_MODEL)
    ys = ffn(xs, g=norm_g[1, 2], which=(1, 1), final_g=final_norm_g, **one_tile)
    y_sample = jnp.swapaxes(ys.reshape(len_s, nb_s, D_MODEL), 0, 1)

    state4 = lambda s: s.reshape(1, -1, N_GROUPS, P_STATE)
    new_conv_p = jnp.swapaxes(buf_p.reshape(CONV_W - 1, nb_p, D_MODEL), 0, 1)[None]
    new_conv_s = jnp.swapaxes(buf_s, 0, 1)[None]
    return (y_prompt, y_sample, state4(sre_p), state4(sim_p), new_conv_p,
            state4(sre_s), state4(sim_s), new_conv_s)
```

```python
import functools

import jax
import jax.numpy as jnp
from jax import lax
from jax.experimental import pallas as pl
from jax.experimental.pallas import tpu as pltpu

F32 = jnp.float32
BF16 = jnp.bfloat16

D_MODEL = 1024
D_FF = 4 * D_MODEL
GROUP_SIZE = 16
N_GROUPS = D_MODEL // GROUP_SIZE
P_STATE = 64
S_DIM = N_GROUPS * P_STATE
CONV_W = 3
EPS = 1e-6

SUBLANES = 8
FFN_ROWS = 1024
FFN_COLS = 1024
FFN_COLS_F32 = 512
FFN_ROW_BLOCK = 256
SSM_BLOCKS = 4
SSM_BLOCK_CH = D_MODEL // SSM_BLOCKS
SSM_BLOCK_ST = S_DIM // SSM_BLOCKS
SCAN_LANES = 512
VMEM_LIMIT = 56 * 1024 * 1024


def _rmsnorm(x, g):
    return x * lax.rsqrt(jnp.mean(x * x, axis=-1, keepdims=True) + EPS) * g


def _dot(a, b):
    return jnp.dot(a, b, preferred_element_type=F32)


def _ffn_kernel(x_ref, g_ref, wg_ref, wu_ref, wd_ref, *rest, n_chunks, final_norm,
                x_seq_major, out_seq_major, emit_bf16):
    rest = list(rest)
    gf_ref = rest.pop(0) if final_norm else None
    o_ref = rest.pop(0)
    if emit_bf16:
        wg_bf, wu_bf, wd_bf = rest[:3]
        del rest[:3]
    h_ref, acc_ref = rest[:2]
    xt_ref = rest[2] if x_seq_major else x_ref
    j = pl.program_id(1)

    @pl.when(j == 0)
    def _():
        if x_seq_major:
            for t in range(x_ref.shape[1]):
                xt_ref[t * SUBLANES:(t + 1) * SUBLANES, :] = x_ref[:, t, :]
        h_ref[...] = _rmsnorm(xt_ref[...], g_ref[...]).astype(BF16)
        acc_ref[...] = jnp.zeros_like(acc_ref)

    if emit_bf16:
        wg_bf[...] = wg_ref[...].astype(BF16)
        wu_bf[...] = wu_ref[...].astype(BF16)
        wd_bf[...] = wd_ref[...].astype(BF16)
        wg_ref, wu_ref, wd_ref = wg_bf, wu_bf, wd_bf

    for r in range(0, FFN_ROWS, FFN_ROW_BLOCK):
        h = h_ref[r:r + FFN_ROW_BLOCK, :]
        gate = _dot(h, wg_ref[...])
        up = _dot(h, wu_ref[...])
        act = (jax.nn.silu(gate) * up).astype(BF16)
        acc_ref[r:r + FFN_ROW_BLOCK, :] += _dot(act, wd_ref[...])

    @pl.when(j == n_chunks - 1)
    def _():
        out = xt_ref[...] + 0.5 * acc_ref[...]
        if final_norm:
            out = _rmsnorm(out, gf_ref[...])
        if out_seq_major:
            for t in range(o_ref.shape[1]):
                o_ref[:, t, :] = out[t * SUBLANES:(t + 1) * SUBLANES, :]
        else:
            o_ref[...] = out


def _ffn_call(x, x_block, x_map, out_shape, out_block, out_map, n_tiles, g, weights, cols,
              final_g=None):
    n_chunks = D_FF // cols
    emit_bf16 = isinstance(weights[-1], tuple)
    x_seq_major = len(x_block) == 3
    out_seq_major = len(out_block) == 3
    body = functools.partial(_ffn_kernel, n_chunks=n_chunks, final_norm=final_g is not None,
                             x_seq_major=x_seq_major, out_seq_major=out_seq_major, emit_bf16=emit_bf16)
    scratch = [pltpu.VMEM((FFN_ROWS, D_MODEL), BF16), pltpu.VMEM((FFN_ROWS, D_MODEL), F32)]
    if x_seq_major:
        scratch.append(pltpu.VMEM((FFN_ROWS, D_MODEL), F32))

    const2 = lambda i, j: (0, 0)
    gate_bf = pl.BlockSpec((D_MODEL, cols), lambda i, j: (0, j))
    down_bf = pl.BlockSpec((cols, D_MODEL), lambda i, j: (j, 0))
    out_specs = pl.BlockSpec(out_block, lambda i, j: out_map(i))
    out_shapes = jax.ShapeDtypeStruct(out_shape, F32)
    if emit_bf16:
        w_gu, w_down, (layer, half) = weights
        w_specs = [
            pl.BlockSpec((None, None, D_MODEL, cols), lambda i, j: (layer, half, 0, j)),
            pl.BlockSpec((None, None, D_MODEL, cols), lambda i, j: (layer, half, 0, j + n_chunks)),
            pl.BlockSpec((None, None, cols, D_MODEL), lambda i, j: (layer, half, j, 0)),
        ]
        w_args = [w_gu, w_gu, w_down]
        out_specs = (out_specs, gate_bf, gate_bf, down_bf)
        out_shapes = (out_shapes,
                      jax.ShapeDtypeStruct((D_MODEL, D_FF), BF16),
                      jax.ShapeDtypeStruct((D_MODEL, D_FF), BF16),
                      jax.ShapeDtypeStruct((D_FF, D_MODEL), BF16))
    else:
        w_specs = [gate_bf, gate_bf, down_bf]
        w_args = list(weights)
    in_specs = [pl.BlockSpec(x_block, lambda i, j: x_map(i)), pl.BlockSpec((1, D_MODEL), const2)] + w_specs
    args = [x, g.reshape(1, D_MODEL)] + w_args
    if final_g is not None:
        in_specs.append(pl.BlockSpec((1, D_MODEL), const2))
        args.append(final_g.reshape(1, D_MODEL))
    res = pl.pallas_call(
        body,
        grid=(n_tiles, n_chunks),
        in_specs=in_specs,
        out_specs=out_specs,
        out_shape=out_shapes,
        scratch_shapes=scratch,
        compiler_params=pltpu.CompilerParams(
            dimension_semantics=("arbitrary", "arbitrary"), vmem_limit_bytes=VMEM_LIMIT),
        name="ffn",
    )(*args)
    return (res[0], tuple(res[1:])) if emit_bf16 else res


def _ssm_prep_kernel(lam_re_ref, lam_im_ref, ldt_ref, lam_re_rep_ref, lam_im_rep_ref,
                     b_re_ref, b_im_ref, lb_re_ref, lb_im_ref, bb_re_ref, bb_im_ref):
    dt = jnp.exp(ldt_ref[...])

    def discretise(lam_re, lam_im):
        mag = jnp.exp(lam_re * dt)
        lb_re = mag * jnp.cos(lam_im * dt)
        lb_im = mag * jnp.sin(lam_im * dt)
        return lb_re, lb_im

    lb_re, lb_im = discretise(lam_re_ref[...], lam_im_ref[...])
    lb_re_ref[...] = lb_re
    lb_im_ref[...] = lb_im
    lam_re = lam_re_rep_ref[...]
    lam_im = lam_im_rep_ref[...]
    lbr, lbi = discretise(lam_re, lam_im)
    den = lam_re * lam_re + lam_im * lam_im
    nr = lbr - 1.0
    ni = lbi
    f_re = (nr * lam_re + ni * lam_im) / den
    f_im = (ni * lam_re - nr * lam_im) / den
    b_re = b_re_ref[...]
    b_im = b_im_ref[...]
    bb_re_ref[...] = f_re * b_re - f_im * b_im
    bb_im_ref[...] = f_re * b_im + f_im * b_re


def _ssm_prep(lam_re, lam_im, log_dt, b_re, b_im):
    pc = P_STATE * GROUP_SIZE
    rep = lambda a: jnp.repeat(a, GROUP_SIZE, axis=-1)
    shp = lambda n: jax.ShapeDtypeStruct((N_GROUPS, n), F32)
    return pl.pallas_call(
        _ssm_prep_kernel,
        out_shape=(shp(P_STATE), shp(P_STATE), shp(pc), shp(pc)),
        name="ssm_prep",
    )(lam_re, lam_im, log_dt.reshape(N_GROUPS, 1), rep(lam_re), rep(lam_im),
      b_re.reshape(N_GROUPS, pc), b_im.reshape(N_GROUPS, pc))


def _block_diag(w):
    gpb = N_GROUPS // SSM_BLOCKS
    a, b = w.shape[1], w.shape[2]
    w = w.reshape(SSM_BLOCKS, gpb, a, 1, b)
    eye = jnp.eye(gpb, dtype=bool)[None, :, None, :, None]
    return jnp.where(eye, w, 0.0).reshape(SSM_BLOCKS, gpb * a, gpb * b)


def _s5_kernel(x_ref, g_ref, s0_re_ref, s0_im_ref, lb_re_ref, lb_im_ref,
               wb_re_ref, wb_im_ref, wc_re_ref, wc_im_ref, d_ref, wglu_ref,
               o_ref, new_re_ref, new_im_ref,
               st_re, st_im, bu_re, bu_im, *, steps, n_seq, carry_over_grid):
    rows = steps * n_seq
    if carry_over_grid:
        @pl.when(pl.program_id(0) == 0)
        def _():
            st_re[...] = s0_re_ref[...]
            st_im[...] = s0_im_ref[...]
    else:
        st_re[...] = s0_re_ref[...]
        st_im[...] = s0_im_ref[...]

    x = x_ref[...].reshape(rows, D_MODEL)
    u = _rmsnorm(x, g_ref[...])
    ub = u.astype(BF16)

    def project_in(k):
        uk = ub[:, k * SSM_BLOCK_CH:(k + 1) * SSM_BLOCK_CH]
        bu_re[k % 2] = _dot(uk, wb_re_ref[k])
        bu_im[k % 2] = _dot(uk, wb_im_ref[k])

    def scan_block(k):
        b_re, b_im = bu_re.at[k % 2], bu_im.at[k % 2]
        for c in range(SSM_BLOCK_ST // SCAN_LANES):
            lo = c * SCAN_LANES
            glo = k * SSM_BLOCK_ST + lo
            lr = jnp.broadcast_to(lb_re_ref[:, glo:glo + SCAN_LANES], (SUBLANES, SCAN_LANES))
            li = jnp.broadcast_to(lb_im_ref[:, glo:glo + SCAN_LANES], (SUBLANES, SCAN_LANES))
            for r_state in range(0, n_seq, SUBLANES):
                hr = st_re[r_state:r_state + SUBLANES, glo:glo + SCAN_LANES]
                hi = st_im[r_state:r_state + SUBLANES, glo:glo + SCAN_LANES]
                for t in range(steps):
                    r = t * n_seq + r_state
                    br = b_re[r:r + SUBLANES, lo:lo + SCAN_LANES]
                    bi = b_im[r:r + SUBLANES, lo:lo + SCAN_LANES]
                    hr, hi = lr * hr - li * hi + br, lr * hi + li * hr + bi
                    b_re[r:r + SUBLANES, lo:lo + SCAN_LANES] = hr
                    b_im[r:r + SUBLANES, lo:lo + SCAN_LANES] = hi
                st_re[r_state:r_state + SUBLANES, glo:glo + SCAN_LANES] = hr
                st_im[r_state:r_state + SUBLANES, glo:glo + SCAN_LANES] = hi

    ys = []
    project_in(0)
    for k in range(SSM_BLOCKS):
        if k + 1 < SSM_BLOCKS:
            project_in(k + 1)
        scan_block(k)
        hr = bu_re[k % 2].astype(BF16)
        hi = bu_im[k % 2].astype(BF16)
        ys.append(_dot(hr, wc_re_ref[k]) - _dot(hi, wc_im_ref[k]))
    y = jnp.concatenate(ys, axis=-1) + d_ref[...] * u
    z = _dot(jax.nn.gelu(y).astype(BF16), wglu_ref[...])
    m = z[:, :D_MODEL] * jax.nn.sigmoid(z[:, D_MODEL:])
    o_ref[...] = (x + m).reshape(o_ref.shape)
    new_re_ref[...] = st_re[...]
    new_im_ref[...] = st_im[...]


def _s5_call(x, x_block, x_map, grid, s0_re, s0_im, state_map, g, lb_re, lb_im,
             wb_re, wb_im, wc_re, wc_im, d_skip, wglu, *, steps, n_seq, carry_over_grid):
    rows = steps * n_seq
    n_state = s0_re.shape[0]
    whole = lambda a: pl.BlockSpec(a.shape, lambda i: (0,) * a.ndim)
    state_spec = pl.BlockSpec((n_seq, S_DIM), state_map)
    g2 = g.reshape(1, D_MODEL)
    d2 = d_skip.reshape(1, D_MODEL)
    lbr = lb_re.reshape(1, S_DIM)
    lbi = lb_im.reshape(1, S_DIM)
    body = functools.partial(_s5_kernel, steps=steps, n_seq=n_seq, carry_over_grid=carry_over_grid)
    return pl.pallas_call(
        body,
        grid=grid,
        in_specs=[pl.BlockSpec(x_block, x_map), whole(g2), state_spec, state_spec,
                  whole(lbr), whole(lbi), whole(wb_re), whole(wb_im), whole(wc_re), whole(wc_im),
                  whole(d2), whole(wglu)],
        out_specs=(pl.BlockSpec(x_block, x_map), state_spec, state_spec),
        out_shape=(jax.ShapeDtypeStruct(x.shape, F32),
                   jax.ShapeDtypeStruct((n_state, S_DIM), F32),
                   jax.ShapeDtypeStruct((n_state, S_DIM), F32)),
        scratch_shapes=[pltpu.VMEM((n_seq, S_DIM), F32), pltpu.VMEM((n_seq, S_DIM), F32),
                        pltpu.VMEM((2, rows, SSM_BLOCK_ST), F32), pltpu.VMEM((2, rows, SSM_BLOCK_ST), F32)],
        compiler_params=pltpu.CompilerParams(
            dimension_semantics=("arbitrary",), vmem_limit_bytes=VMEM_LIMIT),
        name="s5_mixer",
    )(x, g2, s0_re, s0_im, lbr, lbi, wb_re, wb_im, wc_re, wc_im, d2, wglu)


def _conv_kernel(x_ref, g_ref, buf0_ref, win_ref, cw_ref, wout_ref, o_ref, newbuf_ref, zp,
                 *, steps, n_seq, carry_over_grid):
    rows = steps * n_seq
    halo = (CONV_W - 1) * n_seq
    if carry_over_grid:
        @pl.when(pl.program_id(0) == 0)
        def _():
            zp[0:halo, :] = buf0_ref[...].reshape(halo, D_MODEL)
    else:
        zp[0:halo, :] = buf0_ref[...].reshape(halo, D_MODEL)

    x = x_ref[...].reshape(rows, D_MODEL)
    h = _rmsnorm(x, g_ref[...]).astype(BF16)
    p = _dot(h, win_ref[...])
    gb = p[:, :D_MODEL]
    zp[halo:halo + rows, :] = p[:, D_MODEL:2 * D_MODEL] * p[:, 2 * D_MODEL:]
    conv = cw_ref[0:1, :] * zp[0:rows, :]
    for k in range(1, CONV_W):
        conv = conv + cw_ref[k:k + 1, :] * zp[k * n_seq:k * n_seq + rows, :]
    m = _dot((gb * conv).astype(BF16), wout_ref[...])
    o_ref[...] = (x + m).reshape(o_ref.shape)
    tail = zp[rows:rows + halo, :]
    zp[0:halo, :] = tail
    newbuf_ref[...] = tail.reshape(newbuf_ref.shape)


def _conv_call(x, x_block, x_map, grid, buf0, buf_block, buf_map, g, w_in, cw, w_out,
               *, steps, n_seq, carry_over_grid):
    rows = steps * n_seq
    halo = (CONV_W - 1) * n_seq
    whole = lambda a: pl.BlockSpec(a.shape, lambda i: (0,) * a.ndim)
    g2 = g.reshape(1, D_MODEL)
    body = functools.partial(_conv_kernel, steps=steps, n_seq=n_seq, carry_over_grid=carry_over_grid)
    return pl.pallas_call(
        body,
        grid=grid,
        in_specs=[pl.BlockSpec(x_block, x_map), whole(g2), pl.BlockSpec(buf_block, buf_map),
                  whole(w_in), whole(cw), whole(w_out)],
        out_specs=(pl.BlockSpec(x_block, x_map), pl.BlockSpec(buf_block, buf_map)),
        out_shape=(jax.ShapeDtypeStruct(x.shape, F32), jax.ShapeDtypeStruct(buf0.shape, F32)),
        scratch_shapes=[pltpu.VMEM((halo + rows, D_MODEL), F32)],
        compiler_params=pltpu.CompilerParams(
            dimension_semantics=("arbitrary",), vmem_limit_bytes=VMEM_LIMIT),
        name="conv_mixer",
    )(x, g2, buf0, w_in, cw, w_out)


S5_PROMPT_STEPS = 64
CONV_PROMPT_STEPS = 64
SAMPLE_SEQ_BLOCK = 32


def kernel(x_prompt, x_sample, state_ssm_re, state_ssm_im, cache_conv, norm_g, final_norm_g, ffn_w_gate_up, ffn_w_down, ssm_lam_re, ssm_lam_im, ssm_log_dt, ssm_b_re, ssm_b_im, ssm_c_re, ssm_c_im, ssm_d, ssm_w_glu, conv_w_in, conv_w, conv_w_out):
    nb_p, len_p, _ = x_prompt.shape
    nb_s, len_s, _ = x_sample.shape
    rows_p = nb_p * len_p
    rows_s = nb_s * len_s
    assert nb_p == SUBLANES and len_p % FFN_ROWS == 0 and rows_s == FFN_ROWS
    assert nb_s % SAMPLE_SEQ_BLOCK == 0

    lb_re, lb_im, bb_re, bb_im = _ssm_prep(ssm_lam_re[0], ssm_lam_im[0], ssm_log_dt[0],
                                           ssm_b_re[0], ssm_b_im[0])
    to_gcp = lambda a: jnp.swapaxes(a.reshape(N_GROUPS, P_STATE, GROUP_SIZE), 1, 2)
    wb_re = _block_diag(to_gcp(bb_re)).astype(BF16)
    wb_im = _block_diag(to_gcp(bb_im)).astype(BF16)
    wc_re = _block_diag(jnp.swapaxes(ssm_c_re[0], 1, 2)).astype(BF16)
    wc_im = _block_diag(jnp.swapaxes(ssm_c_im[0], 1, 2)).astype(BF16)
    wglu = ssm_w_glu[0].astype(BF16)
    w_in = conv_w_in[0].astype(BF16)
    w_out = conv_w_out[0].astype(BF16)

    t_tiles = len_p // FFN_ROWS
    row_tiles_p = rows_p // FFN_ROWS
    flat_block = (FFN_ROWS, D_MODEL)
    flat_map = lambda i: (i, 0)

    def ffn_sample(xs, layer, half, final_g=None):
        return _ffn_call(xs, flat_block, flat_map, (rows_s, D_MODEL), flat_block, flat_map, 1,
                         norm_g[layer, 2 * half], (ffn_w_gate_up, ffn_w_down, (layer, half)),
                         FFN_COLS_F32, final_g=final_g)

    xs = jnp.swapaxes(x_sample, 0, 1).reshape(rows_s, D_MODEL)
    xs, w_bf_00 = ffn_sample(xs, 0, 0)
    seq_block = (len_s, SAMPLE_SEQ_BLOCK, D_MODEL)
    seq_map = lambda i: (0, i, 0)
    seq_grid = (nb_s // SAMPLE_SEQ_BLOCK,)
    xs, sre_s, sim_s = _s5_call(
        xs.reshape(len_s, nb_s, D_MODEL), seq_block, seq_map, seq_grid,
        state_ssm_re[0].reshape(nb_s, S_DIM), state_ssm_im[0].reshape(nb_s, S_DIM), lambda i: (i, 0),
        norm_g[0, 1], lb_re, lb_im, wb_re, wb_im, wc_re, wc_im, ssm_d[0], wglu,
        steps=len_s, n_seq=SAMPLE_SEQ_BLOCK, carry_over_grid=False)
    xs = xs.reshape(rows_s, D_MODEL)
    xs, w_bf_01 = ffn_sample(xs, 0, 1)
    xs, w_bf_10 = ffn_sample(xs, 1, 0)
    buf0_s = jnp.swapaxes(cache_conv[0], 0, 1)
    xs, buf_s = _conv_call(
        xs.reshape(len_s, nb_s, D_MODEL), seq_block, seq_map, seq_grid,
        buf0_s, (CONV_W - 1, SAMPLE_SEQ_BLOCK, D_MODEL), seq_map,
        norm_g[1, 1], w_in, conv_w[0], w_out,
        steps=len_s, n_seq=SAMPLE_SEQ_BLOCK, carry_over_grid=False)
    xs = xs.reshape(rows_s, D_MODEL)
    ys, w_bf_11 = ffn_sample(xs, 1, 1, final_g=final_norm_g)
    y_sample = jnp.swapaxes(ys.reshape(len_s, nb_s, D_MODEL), 0, 1)

    def ffn(x, x_block, x_map, out_shape, out_block, out_map, g, w_bf, final_g=None):
        return _ffn_call(x, x_block, x_map, out_shape, out_block, out_map, row_tiles_p, g, w_bf,
                         FFN_COLS, final_g=final_g)

    seq_major_block = (nb_p, FFN_ROWS // nb_p, D_MODEL)
    seq_major_map = lambda i: (0, i, 0)
    xp = ffn(x_prompt, seq_major_block, seq_major_map, (rows_p, D_MODEL), flat_block, flat_map,
             norm_g[0, 0], w_bf_00)
    zero_state = jnp.zeros((nb_p, S_DIM), F32)
    xp, sre_p, sim_p = _s5_call(
        xp, (S5_PROMPT_STEPS * nb_p, D_MODEL), flat_map, (len_p // S5_PROMPT_STEPS,),
        zero_state, zero_state, lambda i: (0, 0), norm_g[0, 1], lb_re, lb_im,
        wb_re, wb_im, wc_re, wc_im, ssm_d[0], wglu,
        steps=S5_PROMPT_STEPS, n_seq=nb_p, carry_over_grid=True)
    xp = ffn(xp, flat_block, flat_map, (rows_p, D_MODEL), flat_block, flat_map, norm_g[0, 2], w_bf_01)
    xp = ffn(xp, flat_block, flat_map, (rows_p, D_MODEL), flat_block, flat_map, norm_g[1, 0], w_bf_10)
    halo_p = (CONV_W - 1) * nb_p
    xp, buf_p = _conv_call(
        xp, (CONV_PROMPT_STEPS * nb_p, D_MODEL), flat_map, (len_p // CONV_PROMPT_STEPS,),
        jnp.zeros((halo_p, D_MODEL), F32), (halo_p, D_MODEL), lambda i: (0, 0),
        norm_g[1, 1], w_in, conv_w[0], w_out,
        steps=CONV_PROMPT_STEPS, n_seq=nb_p, carry_over_grid=True)
    y_prompt = ffn(xp, flat_block, flat_map, (nb_p, len_p, D_MODEL), seq_major_block, seq_major_map,
                   norm_g[1, 2], w_bf_11, final_g=final_norm_g)

    state4 = lambda s: s.reshape(1, -1, N_GROUPS, P_STATE)
    new_conv_p = jnp.swapaxes(buf_p.reshape(CONV_W - 1, nb_p, D_MODEL), 0, 1)[None]
    new_conv_s = jnp.swapaxes(buf_s, 0, 1)[None]
    return (y_prompt, y_sample, state4(sre_p), state4(sim_p), new_conv_p,
            state4(sre_s), state4(sim_s), new_conv_s)
```

```python
import functools

import jax
import jax.numpy as jnp
from jax import lax
from jax.experimental import pallas as pl
from jax.experimental.pallas import tpu as pltpu

F32 = jnp.float32
BF16 = jnp.bfloat16

D_MODEL = 1024
D_FF = 4 * D_MODEL
GROUP_SIZE = 16
N_GROUPS = D_MODEL // GROUP_SIZE
P_STATE = 64
S_DIM = N_GROUPS * P_STATE
CONV_W = 3
EPS = 1e-6

SUBLANES = 8
FFN_ROWS = 1024
FFN_COLS = 1024
FFN_COLS_F32 = 512
FFN_ROW_BLOCK = 256
SSM_BLOCKS = 4
SSM_BLOCK_CH = D_MODEL // SSM_BLOCKS
SSM_BLOCK_ST = S_DIM // SSM_BLOCKS
SCAN_LANES = 512
VMEM_LIMIT = 56 * 1024 * 1024


def _rmsnorm(x, g):
    return x * lax.rsqrt(jnp.mean(x * x, axis=-1, keepdims=True) + EPS) * g


def _dot(a, b):
    return jnp.dot(a, b, preferred_element_type=F32)


def _ffn_kernel(x_ref, g_ref, wg_ref, wu_ref, wd_ref, *rest, n_chunks, final_norm,
                x_seq_major, out_seq_major, emit_bf16):
    rest = list(rest)
    gf_ref = rest.pop(0) if final_norm else None
    o_ref = rest.pop(0)
    if emit_bf16:
        wg_bf, wu_bf, wd_bf = rest[:3]
        del rest[:3]
    h_ref, acc_ref = rest[:2]
    xt_ref = rest[2] if x_seq_major else x_ref
    j = pl.program_id(1)

    @pl.when(j == 0)
    def _():
        if x_seq_major:
            for t in range(x_ref.shape[1]):
                xt_ref[t * SUBLANES:(t + 1) * SUBLANES, :] = x_ref[:, t, :]
        h_ref[...] = _rmsnorm(xt_ref[...], g_ref[...]).astype(BF16)
        acc_ref[...] = jnp.zeros_like(acc_ref)

    if emit_bf16:
        wg_bf[...] = wg_ref[...].astype(BF16)
        wu_bf[...] = wu_ref[...].astype(BF16)
        wd_bf[...] = wd_ref[...].astype(BF16)
        wg_ref, wu_ref, wd_ref = wg_bf, wu_bf, wd_bf

    for r in range(0, FFN_ROWS, FFN_ROW_BLOCK):
        h = h_ref[r:r + FFN_ROW_BLOCK, :]
        gate = _dot(h, wg_ref[...])
        up = _dot(h, wu_ref[...])
        act = (jax.nn.silu(gate) * up).astype(BF16)
        acc_ref[r:r + FFN_ROW_BLOCK, :] += _dot(act, wd_ref[...])

    @pl.when(j == n_chunks - 1)
    def _():
        out = xt_ref[...] + 0.5 * acc_ref[...]
        if final_norm:
            out = _rmsnorm(out, gf_ref[...])
        if out_seq_major:
            for t in range(o_ref.shape[1]):
                o_ref[:, t, :] = out[t * SUBLANES:(t + 1) * SUBLANES, :]
        else:
            o_ref[...] = out


def _ffn_call(x, x_block, x_map, out_shape, out_block, out_map, n_tiles, g, weights, cols,
              final_g=None):
    n_chunks = D_FF // cols
    emit_bf16 = isinstance(weights[-1], tuple)
    x_seq_major = len(x_block) == 3
    out_seq_major = len(out_block) == 3
    body = functools.partial(_ffn_kernel, n_chunks=n_chunks, final_norm=final_g is not None,
                             x_seq_major=x_seq_major, out_seq_major=out_seq_major, emit_bf16=emit_bf16)
    scratch = [pltpu.VMEM((FFN_ROWS, D_MODEL), BF16), pltpu.VMEM((FFN_ROWS, D_MODEL), F32)]
    if x_seq_major:
        scratch.append(pltpu.VMEM((FFN_ROWS, D_MODEL), F32))

    const2 = lambda i, j: (0, 0)
    gate_bf = pl.BlockSpec((D_MODEL, cols), lambda i, j: (0, j))
    down_bf = pl.BlockSpec((cols, D_MODEL), lambda i, j: (j, 0))
    out_specs = pl.BlockSpec(out_block, lambda i, j: out_map(i))
    out_shapes = jax.ShapeDtypeStruct(out_shape, F32)
    if emit_bf16:
        w_gu, w_down, (layer, half) = weights
        w_specs = [
            pl.BlockSpec((None, None, D_MODEL, cols), lambda i, j: (layer, half, 0, j)),
            pl.BlockSpec((None, None, D_MODEL, cols), lambda i, j: (layer, half, 0, j + n_chunks)),
            pl.BlockSpec((None, None, cols, D_MODEL), lambda i, j: (layer, half, j, 0)),
        ]
        w_args = [w_gu, w_gu, w_down]
        out_specs = (out_specs, gate_bf, gate_bf, down_bf)
        out_shapes = (out_shapes,
                      jax.ShapeDtypeStruct((D_MODEL, D_FF), BF16),
                      jax.ShapeDtypeStruct((D_MODEL, D_FF), BF16),
                      jax.ShapeDtypeStruct((D_FF, D_MODEL), BF16))
    else:
        w_specs = [gate_bf, gate_bf, down_bf]
        w_args = list(weights)
    in_specs = [pl.BlockSpec(x_block, lambda i, j: x_map(i)), pl.BlockSpec((1, D_MODEL), const2)] + w_specs
    args = [x, g.reshape(1, D_MODEL)] + w_args
    if final_g is not None:
        in_specs.append(pl.BlockSpec((1, D_MODEL), const2))
        args.append(final_g.reshape(1, D_MODEL))
    res = pl.pallas_call(
        body,
        grid=(n_tiles, n_chunks),
        in_specs=in_specs,
        out_specs=out_specs,
        out_shape=out_shapes,
        scratch_shapes=scratch,
        compiler_params=pltpu.CompilerParams(
            dimension_semantics=("arbitrary", "arbitrary"), vmem_limit_bytes=VMEM_LIMIT),
        name="ffn",
    )(*args)
    return (res[0], tuple(res[1:])) if emit_bf16 else res


def _ffn_resident_kernel(x_ref, g_ref, wg_ref, wu_ref, wd_ref, *rest, cols, final_norm,
                         x_seq_major, out_seq_major):
    rest = list(rest)
    gf_ref = rest.pop(0) if final_norm else None
    o_ref = rest.pop(0)
    steps_per_block = FFN_ROW_BLOCK // SUBLANES
    for r in range(0, FFN_ROWS, FFN_ROW_BLOCK):
        t0 = r // SUBLANES
        if x_seq_major:
            x = jnp.concatenate([x_ref[:, t0 + t, :] for t in range(steps_per_block)], axis=0)
        else:
            x = x_ref[r:r + FFN_ROW_BLOCK, :]
        h = _rmsnorm(x, g_ref[...]).astype(BF16)
        acc = None
        for c in range(0, D_FF, cols):
            gate = _dot(h, wg_ref[:, c:c + cols])
            up = _dot(h, wu_ref[:, c:c + cols])
            act = (jax.nn.silu(gate) * up).astype(BF16)
            part = _dot(act, wd_ref[c:c + cols, :])
            acc = part if acc is None else acc + part
        out = x + 0.5 * acc
        if final_norm:
            out = _rmsnorm(out, gf_ref[...])
        if out_seq_major:
            for t in range(steps_per_block):
                o_ref[:, t0 + t, :] = out[t * SUBLANES:(t + 1) * SUBLANES, :]
        else:
            o_ref[r:r + FFN_ROW_BLOCK, :] = out


def _ffn_resident_call(x, x_block, x_map, out_shape, out_block, out_map, n_tiles, g, weights, cols,
                       final_g=None):
    body = functools.partial(_ffn_resident_kernel, cols=cols, final_norm=final_g is not None,
                             x_seq_major=len(x_block) == 3, out_seq_major=len(out_block) == 3)
    whole = lambda a: pl.BlockSpec(a.shape, lambda i: (0,) * a.ndim)
    g2 = g.reshape(1, D_MODEL)
    in_specs = [pl.BlockSpec(x_block, x_map), whole(g2)] + [whole(w) for w in weights]
    args = [x, g2, *weights]
    if final_g is not None:
        gf2 = final_g.reshape(1, D_MODEL)
        in_specs.append(whole(gf2))
        args.append(gf2)
    return pl.pallas_call(
        body,
        grid=(n_tiles,),
        in_specs=in_specs,
        out_specs=pl.BlockSpec(out_block, out_map),
        out_shape=jax.ShapeDtypeStruct(out_shape, F32),
        compiler_params=pltpu.CompilerParams(
            dimension_semantics=("arbitrary",), vmem_limit_bytes=VMEM_LIMIT),
        name="ffn_resident",
    )(*args)


def _ssm_prep_kernel(lam_re_ref, lam_im_ref, ldt_ref, lam_re_rep_ref, lam_im_rep_ref,
                     b_re_ref, b_im_ref, lb_re_ref, lb_im_ref, bb_re_ref, bb_im_ref):
    dt = jnp.exp(ldt_ref[...])

    def discretise(lam_re, lam_im):
        mag = jnp.exp(lam_re * dt)
        lb_re = mag * jnp.cos(lam_im * dt)
        lb_im = mag * jnp.sin(lam_im * dt)
        return lb_re, lb_im

    lb_re, lb_im = discretise(lam_re_ref[...], lam_im_ref[...])
    lb_re_ref[...] = lb_re
    lb_im_ref[...] = lb_im
    lam_re = lam_re_rep_ref[...]
    lam_im = lam_im_rep_ref[...]
    lbr, lbi = discretise(lam_re, lam_im)
    den = lam_re * lam_re + lam_im * lam_im
    nr = lbr - 1.0
    ni = lbi
    f_re = (nr * lam_re + ni * lam_im) / den
    f_im = (ni * lam_re - nr * lam_im) / den
    b_re = b_re_ref[...]
    b_im = b_im_ref[...]
    bb_re_ref[...] = f_re * b_re - f_im * b_im
    bb_im_ref[...] = f_re * b_im + f_im * b_re


def _ssm_prep(lam_re, lam_im, log_dt, b_re, b_im):
    pc = P_STATE * GROUP_SIZE
    rep = lambda a: jnp.repeat(a, GROUP_SIZE, axis=-1)
    shp = lambda n: jax.ShapeDtypeStruct((N_GROUPS, n), F32)
    return pl.pallas_call(
        _ssm_prep_kernel,
        out_shape=(shp(P_STATE), shp(P_STATE), shp(pc), shp(pc)),
        name="ssm_prep",
    )(lam_re, lam_im, log_dt.reshape(N_GROUPS, 1), rep(lam_re), rep(lam_im),
      b_re.reshape(N_GROUPS, pc), b_im.reshape(N_GROUPS, pc))


def _block_diag(w):
    gpb = N_GROUPS // SSM_BLOCKS
    a, b = w.shape[1], w.shape[2]
    w = w.reshape(SSM_BLOCKS, gpb, a, 1, b)
    eye = jnp.eye(gpb, dtype=bool)[None, :, None, :, None]
    return jnp.where(eye, w, 0.0).reshape(SSM_BLOCKS, gpb * a, gpb * b)


def _s5_kernel(x_ref, g_ref, s0_re_ref, s0_im_ref, lb_re_ref, lb_im_ref,
               wb_re_ref, wb_im_ref, wc_re_ref, wc_im_ref, d_ref, wglu_ref,
               o_ref, new_re_ref, new_im_ref,
               st_re, st_im, bu_re, bu_im, *, steps, n_seq, carry_over_grid):
    rows = steps * n_seq
    if carry_over_grid:
        @pl.when(pl.program_id(0) == 0)
        def _():
            st_re[...] = s0_re_ref[...]
            st_im[...] = s0_im_ref[...]
    else:
        st_re[...] = s0_re_ref[...]
        st_im[...] = s0_im_ref[...]

    x = x_ref[...].reshape(rows, D_MODEL)
    u = _rmsnorm(x, g_ref[...])
    ub = u.astype(BF16)

    def project_in(k):
        uk = ub[:, k * SSM_BLOCK_CH:(k + 1) * SSM_BLOCK_CH]
        bu_re[k % 2] = _dot(uk, wb_re_ref[k])
        bu_im[k % 2] = _dot(uk, wb_im_ref[k])

    def scan_block(k):
        b_re, b_im = bu_re.at[k % 2], bu_im.at[k % 2]
        for c in range(SSM_BLOCK_ST // SCAN_LANES):
            lo = c * SCAN_LANES
            glo = k * SSM_BLOCK_ST + lo
            lr = jnp.broadcast_to(lb_re_ref[:, glo:glo + SCAN_LANES], (SUBLANES, SCAN_LANES))
            li = jnp.broadcast_to(lb_im_ref[:, glo:glo + SCAN_LANES], (SUBLANES, SCAN_LANES))
            for r_state in range(0, n_seq, SUBLANES):
                hr = st_re[r_state:r_state + SUBLANES, glo:glo + SCAN_LANES]
                hi = st_im[r_state:r_state + SUBLANES, glo:glo + SCAN_LANES]
                for t in range(steps):
                    r = t * n_seq + r_state
                    br = b_re[r:r + SUBLANES, lo:lo + SCAN_LANES]
                    bi = b_im[r:r + SUBLANES, lo:lo + SCAN_LANES]
                    hr, hi = lr * hr - li * hi + br, lr * hi + li * hr + bi
                    b_re[r:r + SUBLANES, lo:lo + SCAN_LANES] = hr
                    b_im[r:r + SUBLANES, lo:lo + SCAN_LANES] = hi
                st_re[r_state:r_state + SUBLANES, glo:glo + SCAN_LANES] = hr
                st_im[r_state:r_state + SUBLANES, glo:glo + SCAN_LANES] = hi

    ys = []
    project_in(0)
    for k in range(SSM_BLOCKS):
        if k + 1 < SSM_BLOCKS:
            project_in(k + 1)
        scan_block(k)
        hr = bu_re[k % 2].astype(BF16)
        hi = bu_im[k % 2].astype(BF16)
        ys.append(_dot(hr, wc_re_ref[k]) - _dot(hi, wc_im_ref[k]))
    y = jnp.concatenate(ys, axis=-1) + d_ref[...] * u
    z = _dot(jax.nn.gelu(y).astype(BF16), wglu_ref[...])
    m = z[:, :D_MODEL] * jax.nn.sigmoid(z[:, D_MODEL:])
    o_ref[...] = (x + m).reshape(o_ref.shape)
    new_re_ref[...] = st_re[...]
    new_im_ref[...] = st_im[...]


def _s5_call(x, x_block, x_map, grid, s0_re, s0_im, state_map, g, lb_re, lb_im,
             wb_re, wb_im, wc_re, wc_im, d_skip, wglu, *, steps, n_seq, carry_over_grid):
    rows = steps * n_seq
    n_state = s0_re.shape[0]
    whole = lambda a: pl.BlockSpec(a.shape, lambda i: (0,) * a.ndim)
    state_spec = pl.BlockSpec((n_seq, S_DIM), state_map)
    g2 = g.reshape(1, D_MODEL)
    d2 = d_skip.reshape(1, D_MODEL)
    lbr = lb_re.reshape(1, S_DIM)
    lbi = lb_im.reshape(1, S_DIM)
    body = functools.partial(_s5_kernel, steps=steps, n_seq=n_seq, carry_over_grid=carry_over_grid)
    return pl.pallas_call(
        body,
        grid=grid,
        in_specs=[pl.BlockSpec(x_block, x_map), whole(g2), state_spec, state_spec,
                  whole(lbr), whole(lbi), whole(wb_re), whole(wb_im), whole(wc_re), whole(wc_im),
                  whole(d2), whole(wglu)],
        out_specs=(pl.BlockSpec(x_block, x_map), state_spec, state_spec),
        out_shape=(jax.ShapeDtypeStruct(x.shape, F32),
                   jax.ShapeDtypeStruct((n_state, S_DIM), F32),
                   jax.ShapeDtypeStruct((n_state, S_DIM), F32)),
        scratch_shapes=[pltpu.VMEM((n_seq, S_DIM), F32), pltpu.VMEM((n_seq, S_DIM), F32),
                        pltpu.VMEM((2, rows, SSM_BLOCK_ST), F32), pltpu.VMEM((2, rows, SSM_BLOCK_ST), F32)],
        compiler_params=pltpu.CompilerParams(
            dimension_semantics=("arbitrary",), vmem_limit_bytes=VMEM_LIMIT),
        name="s5_mixer",
    )(x, g2, s0_re, s0_im, lbr, lbi, wb_re, wb_im, wc_re, wc_im, d2, wglu)


def _conv_kernel(x_ref, g_ref, buf0_ref, win_ref, cw_ref, wout_ref, o_ref, newbuf_ref, zp,
                 *, steps, n_seq, carry_over_grid):
    rows = steps * n_seq
    halo = (CONV_W - 1) * n_seq
    if carry_over_grid:
        @pl.when(pl.program_id(0) == 0)
        def _():
            zp[0:halo, :] = buf0_ref[...].reshape(halo, D_MODEL)
    else:
        zp[0:halo, :] = buf0_ref[...].reshape(halo, D_MODEL)

    x = x_ref[...].reshape(rows, D_MODEL)
    h = _rmsnorm(x, g_ref[...]).astype(BF16)
    p = _dot(h, win_ref[...])
    gb = p[:, :D_MODEL]
    zp[halo:halo + rows, :] = p[:, D_MODEL:2 * D_MODEL] * p[:, 2 * D_MODEL:]
    conv = cw_ref[0:1, :] * zp[0:rows, :]
    for k in range(1, CONV_W):
        conv = conv + cw_ref[k:k + 1, :] * zp[k * n_seq:k * n_seq + rows, :]
    m = _dot((gb * conv).astype(BF16), wout_ref[...])
    o_ref[...] = (x + m).reshape(o_ref.shape)
    tail = zp[rows:rows + halo, :]
    zp[0:halo, :] = tail
    newbuf_ref[...] = tail.reshape(newbuf_ref.shape)


def _conv_call(x, x_block, x_map, grid, buf0, buf_block, buf_map, g, w_in, cw, w_out,
               *, steps, n_seq, carry_over_grid):
    rows = steps * n_seq
    halo = (CONV_W - 1) * n_seq
    whole = lambda a: pl.BlockSpec(a.shape, lambda i: (0,) * a.ndim)
    g2 = g.reshape(1, D_MODEL)
    body = functools.partial(_conv_kernel, steps=steps, n_seq=n_seq, carry_over_grid=carry_over_grid)
    return pl.pallas_call(
        body,
        grid=grid,
        in_specs=[pl.BlockSpec(x_block, x_map), whole(g2), pl.BlockSpec(buf_block, buf_map),
                  whole(w_in), whole(cw), whole(w_out)],
        out_specs=(pl.BlockSpec(x_block, x_map), pl.BlockSpec(buf_block, buf_map)),
        out_shape=(jax.ShapeDtypeStruct(x.shape, F32), jax.ShapeDtypeStruct(buf0.shape, F32)),
        scratch_shapes=[pltpu.VMEM((halo + rows, D_MODEL), F32)],
        compiler_params=pltpu.CompilerParams(
            dimension_semantics=("arbitrary",), vmem_limit_bytes=VMEM_LIMIT),
        name="conv_mixer",
    )(x, g2, buf0, w_in, cw, w_out)


S5_PROMPT_STEPS = 64
CONV_PROMPT_STEPS = 64
SAMPLE_SEQ_BLOCK = 32


def kernel(x_prompt, x_sample, state_ssm_re, state_ssm_im, cache_conv, norm_g, final_norm_g, ffn_w_gate_up, ffn_w_down, ssm_lam_re, ssm_lam_im, ssm_log_dt, ssm_b_re, ssm_b_im, ssm_c_re, ssm_c_im, ssm_d, ssm_w_glu, conv_w_in, conv_w, conv_w_out):
    nb_p, len_p, _ = x_prompt.shape
    nb_s, len_s, _ = x_sample.shape
    rows_p = nb_p * len_p
    rows_s = nb_s * len_s
    assert nb_p == SUBLANES and len_p % FFN_ROWS == 0 and rows_s == FFN_ROWS
    assert nb_s % SAMPLE_SEQ_BLOCK == 0

    lb_re, lb_im, bb_re, bb_im = _ssm_prep(ssm_lam_re[0], ssm_lam_im[0], ssm_log_dt[0],
                                           ssm_b_re[0], ssm_b_im[0])
    to_gcp = lambda a: jnp.swapaxes(a.reshape(N_GROUPS, P_STATE, GROUP_SIZE), 1, 2)
    wb_re = _block_diag(to_gcp(bb_re)).astype(BF16)
    wb_im = _block_diag(to_gcp(bb_im)).astype(BF16)
    wc_re = _block_diag(jnp.swapaxes(ssm_c_re[0], 1, 2)).astype(BF16)
    wc_im = _block_diag(jnp.swapaxes(ssm_c_im[0], 1, 2)).astype(BF16)
    wglu = ssm_w_glu[0].astype(BF16)
    w_in = conv_w_in[0].astype(BF16)
    w_out = conv_w_out[0].astype(BF16)

    t_tiles = len_p // FFN_ROWS
    row_tiles_p = rows_p // FFN_ROWS
    flat_block = (FFN_ROWS, D_MODEL)
    flat_map = lambda i: (i, 0)

    def ffn_sample(xs, layer, half, final_g=None):
        return _ffn_call(xs, flat_block, flat_map, (rows_s, D_MODEL), flat_block, flat_map, 1,
                         norm_g[layer, 2 * half], (ffn_w_gate_up, ffn_w_down, (layer, half)),
                         FFN_COLS_F32, final_g=final_g)

    xs = jnp.swapaxes(x_sample, 0, 1).reshape(rows_s, D_MODEL)
    xs, w_bf_00 = ffn_sample(xs, 0, 0)
    seq_block = (len_s, SAMPLE_SEQ_BLOCK, D_MODEL)
    seq_map = lambda i: (0, i, 0)
    seq_grid = (nb_s // SAMPLE_SEQ_BLOCK,)
    xs, sre_s, sim_s = _s5_call(
        xs.reshape(len_s, nb_s, D_MODEL), seq_block, seq_map, seq_grid,
        state_ssm_re[0].reshape(nb_s, S_DIM), state_ssm_im[0].reshape(nb_s, S_DIM), lambda i: (i, 0),
        norm_g[0, 1], lb_re, lb_im, wb_re, wb_im, wc_re, wc_im, ssm_d[0], wglu,
        steps=len_s, n_seq=SAMPLE_SEQ_BLOCK, carry_over_grid=False)
    xs = xs.reshape(rows_s, D_MODEL)
    xs, w_bf_01 = ffn_sample(xs, 0, 1)
    xs, w_bf_10 = ffn_sample(xs, 1, 0)
    buf0_s = jnp.swapaxes(cache_conv[0], 0, 1)
    xs, buf_s = _conv_call(
        xs.reshape(len_s, nb_s, D_MODEL), seq_block, seq_map, seq_grid,
        buf0_s, (CONV_W - 1, SAMPLE_SEQ_BLOCK, D_MODEL), seq_map,
        norm_g[1, 1], w_in, conv_w[0], w_out,
        steps=len_s, n_seq=SAMPLE_SEQ_BLOCK, carry_over_grid=False)
    xs = xs.reshape(rows_s, D_MODEL)
    ys, w_bf_11 = ffn_sample(xs, 1, 1, final_g=final_norm_g)
    y_sample = jnp.swapaxes(ys.reshape(len_s, nb_s, D_MODEL), 0, 1)

    def ffn(x, x_block, x_map, out_shape, out_block, out_map, g, w_bf, final_g=None):
        return _ffn_resident_call(x, x_block, x_map, out_shape, out_block, out_map, row_tiles_p, g, w_bf,
                                  FFN_COLS, final_g=final_g)

    seq_major_block = (nb_p, FFN_ROWS // nb_p, D_MODEL)
    seq_major_map = lambda i: (0, i, 0)
    xp = ffn(x_prompt, seq_major_block, seq_major_map, (rows_p, D_MODEL), flat_block, flat_map,
             norm_g[0, 0], w_bf_00)
    zero_state = jnp.zeros((nb_p, S_DIM), F32)
    xp, sre_p, sim_p = _s5_call(
        xp, (S5_PROMPT_STEPS * nb_p, D_MODEL), flat_map, (len_p // S5_PROMPT_STEPS,),
        zero_state, zero_state, lambda i: (0, 0), norm_g[0, 1], lb_re, lb_im,
        wb_re, wb_im, wc_re, wc_im, ssm_d[0], wglu,
        steps=S5_PROMPT_STEPS, n_seq=nb_p, carry_over_grid=True)
    xp = ffn(xp, flat_block, flat_map, (rows_p, D_MODEL), flat_block, flat_map, norm_g[0, 2], w_bf_01)
    xp = ffn(xp, flat_block, flat_map, (rows_p, D_MODEL), flat_block, flat_map, norm_g[1, 0], w_bf_10)
    halo_p = (CONV_W - 1) * nb_p
    xp, buf_p = _conv_call(
        xp, (CONV_PROMPT_STEPS * nb_p, D_MODEL), flat_map, (len_p // CONV_PROMPT_STEPS,),
        jnp.zeros((halo_p, D_MODEL), F32), (halo_p, D_MODEL), lambda i: (0, 0),
        norm_g[1, 1], w_in, conv_w[0], w_out,
        steps=CONV_PROMPT_STEPS, n_seq=nb_p, carry_over_grid=True)
    y_prompt = ffn(xp, flat_block, flat_map, (nb_p, len_p, D_MODEL), seq_major_block, seq_major_map,
                   norm_g[1, 2], w_bf_11, final_g=final_norm_g)

    state4 = lambda s: s.reshape(1, -1, N_GROUPS, P_STATE)
    new_conv_p = jnp.swapaxes(buf_p.reshape(CONV_W - 1, nb_p, D_MODEL), 0, 1)[None]
    new_conv_s = jnp.swapaxes(buf_s, 0, 1)[None]
    return (y_prompt, y_sample, state4(sre_p), state4(sim_p), new_conv_p,
            state4(sre_s), state4(sim_s), new_conv_s)
```

```python
import functools

import jax
import jax.numpy as jnp
from jax import lax
from jax.experimental import pallas as pl
from jax.experimental.pallas import tpu as pltpu

F32 = jnp.float32
BF16 = jnp.bfloat16

D_MODEL = 1024
D_FF = 4 * D_MODEL
GROUP_SIZE = 16
N_GROUPS = D_MODEL // GROUP_SIZE
P_STATE = 64
S_DIM = N_GROUPS * P_STATE
CONV_W = 3
EPS = 1e-6

SUBLANES = 8
FFN_ROWS = 1024
FFN_COLS = 1024
FFN_COLS_F32 = 512
FFN_ROW_BLOCK = 256
SSM_BLOCKS = 4
SSM_BLOCK_CH = D_MODEL // SSM_BLOCKS
SSM_BLOCK_ST = S_DIM // SSM_BLOCKS
SCAN_LANES = 256
VMEM_LIMIT = 56 * 1024 * 1024


def _rmsnorm(x, g):
    return x * lax.rsqrt(jnp.mean(x * x, axis=-1, keepdims=True) + EPS) * g


def _dot(a, b):
    return jnp.dot(a, b, preferred_element_type=F32)


def _ffn_kernel(x_ref, g_ref, wg_ref, wu_ref, wd_ref, *rest, n_chunks, final_norm,
                x_seq_major, out_seq_major, emit_bf16):
    rest = list(rest)
    gf_ref = rest.pop(0) if final_norm else None
    o_ref = rest.pop(0)
    if emit_bf16:
        wg_bf, wu_bf, wd_bf = rest[:3]
        del rest[:3]
    h_ref, acc_ref = rest[:2]
    xt_ref = rest[2] if x_seq_major else x_ref
    j = pl.program_id(1)

    @pl.when(j == 0)
    def _():
        if x_seq_major:
            n_seq = x_ref.shape[0]
            for t in range(x_ref.shape[1]):
                xt_ref[t * n_seq:(t + 1) * n_seq, :] = x_ref[:, t, :]
        h_ref[...] = _rmsnorm(xt_ref[...], g_ref[...]).astype(BF16)
        acc_ref[...] = jnp.zeros_like(acc_ref)

    if emit_bf16:
        wg_bf[...] = wg_ref[...].astype(BF16)
        wu_bf[...] = wu_ref[...].astype(BF16)
        wd_bf[...] = wd_ref[...].astype(BF16)
        wg_ref, wu_ref, wd_ref = wg_bf, wu_bf, wd_bf

    for r in range(0, FFN_ROWS, FFN_ROW_BLOCK):
        h = h_ref[r:r + FFN_ROW_BLOCK, :]
        gate = _dot(h, wg_ref[...])
        up = _dot(h, wu_ref[...])
        act = (jax.nn.silu(gate) * up).astype(BF16)
        acc_ref[r:r + FFN_ROW_BLOCK, :] += _dot(act, wd_ref[...])

    @pl.when(j == n_chunks - 1)
    def _():
        out = xt_ref[...] + 0.5 * acc_ref[...]
        if final_norm:
            out = _rmsnorm(out, gf_ref[...])
        if out_seq_major:
            n_seq = o_ref.shape[0]
            for t in range(o_ref.shape[1]):
                o_ref[:, t, :] = out[t * n_seq:(t + 1) * n_seq, :]
        else:
            o_ref[...] = out


def _ffn_call(x, x_block, x_map, out_shape, out_block, out_map, n_tiles, g, weights, cols,
              final_g=None):
    n_chunks = D_FF // cols
    emit_bf16 = isinstance(weights[-1], tuple)
    x_seq_major = len(x_block) == 3
    out_seq_major = len(out_block) == 3
    body = functools.partial(_ffn_kernel, n_chunks=n_chunks, final_norm=final_g is not None,
                             x_seq_major=x_seq_major, out_seq_major=out_seq_major, emit_bf16=emit_bf16)
    scratch = [pltpu.VMEM((FFN_ROWS, D_MODEL), BF16), pltpu.VMEM((FFN_ROWS, D_MODEL), F32)]
    if x_seq_major:
        scratch.append(pltpu.VMEM((FFN_ROWS, D_MODEL), F32))

    const2 = lambda i, j: (0, 0)
    gate_bf = pl.BlockSpec((D_MODEL, cols), lambda i, j: (0, j))
    down_bf = pl.BlockSpec((cols, D_MODEL), lambda i, j: (j, 0))
    out_specs = pl.BlockSpec(out_block, lambda i, j: out_map(i))
    out_shapes = jax.ShapeDtypeStruct(out_shape, F32)
    if emit_bf16:
        w_gu, w_down, (layer, half) = weights
        w_specs = [
            pl.BlockSpec((None, None, D_MODEL, cols), lambda i, j: (layer, half, 0, j)),
            pl.BlockSpec((None, None, D_MODEL, cols), lambda i, j: (layer, half, 0, j + n_chunks)),
            pl.BlockSpec((None, None, cols, D_MODEL), lambda i, j: (layer, half, j, 0)),
        ]
        w_args = [w_gu, w_gu, w_down]
        out_specs = (out_specs, gate_bf, gate_bf, down_bf)
        out_shapes = (out_shapes,
                      jax.ShapeDtypeStruct((D_MODEL, D_FF), BF16),
                      jax.ShapeDtypeStruct((D_MODEL, D_FF), BF16),
                      jax.ShapeDtypeStruct((D_FF, D_MODEL), BF16))
    else:
        w_specs = [gate_bf, gate_bf, down_bf]
        w_args = list(weights)
    in_specs = [pl.BlockSpec(x_block, lambda i, j: x_map(i)), pl.BlockSpec((1, D_MODEL), const2)] + w_specs
    args = [x, g.reshape(1, D_MODEL)] + w_args
    if final_g is not None:
        in_specs.append(pl.BlockSpec((1, D_MODEL), const2))
        args.append(final_g.reshape(1, D_MODEL))
    res = pl.pallas_call(
        body,
        grid=(n_tiles, n_chunks),
        in_specs=in_specs,
        out_specs=out_specs,
        out_shape=out_shapes,
        scratch_shapes=scratch,
        compiler_params=pltpu.CompilerParams(
            dimension_semantics=("arbitrary", "arbitrary"), vmem_limit_bytes=VMEM_LIMIT),
        name="ffn",
    )(*args)
    return (res[0], tuple(res[1:])) if emit_bf16 else res


def _ffn_resident_kernel(x_ref, g_ref, wg_ref, wu_ref, wd_ref, *rest, cols, final_norm,
                         x_seq_major, out_seq_major):
    rest = list(rest)
    gf_ref = rest.pop(0) if final_norm else None
    o_ref = rest.pop(0)
    steps_per_block = FFN_ROW_BLOCK // SUBLANES
    for r in range(0, FFN_ROWS, FFN_ROW_BLOCK):
        t0 = r // SUBLANES
        if x_seq_major:
            x = jnp.concatenate([x_ref[:, t0 + t, :] for t in range(steps_per_block)], axis=0)
        else:
            x = x_ref[r:r + FFN_ROW_BLOCK, :]
        h = _rmsnorm(x, g_ref[...]).astype(BF16)
        acc = None
        for c in range(0, D_FF, cols):
            gate = _dot(h, wg_ref[:, c:c + cols])
            up = _dot(h, wu_ref[:, c:c + cols])
            act = (jax.nn.silu(gate) * up).astype(BF16)
            part = _dot(act, wd_ref[c:c + cols, :])
            acc = part if acc is None else acc + part
        out = x + 0.5 * acc
        if final_norm:
            out = _rmsnorm(out, gf_ref[...])
        if out_seq_major:
            for t in range(steps_per_block):
                o_ref[:, t0 + t, :] = out[t * SUBLANES:(t + 1) * SUBLANES, :]
        else:
            o_ref[r:r + FFN_ROW_BLOCK, :] = out


def _ffn_resident_call(x, x_block, x_map, out_shape, out_block, out_map, n_tiles, g, weights, cols,
                       final_g=None):
    body = functools.partial(_ffn_resident_kernel, cols=cols, final_norm=final_g is not None,
                             x_seq_major=len(x_block) == 3, out_seq_major=len(out_block) == 3)
    whole = lambda a: pl.BlockSpec(a.shape, lambda i: (0,) * a.ndim)
    g2 = g.reshape(1, D_MODEL)
    in_specs = [pl.BlockSpec(x_block, x_map), whole(g2)] + [whole(w) for w in weights]
    args = [x, g2, *weights]
    if final_g is not None:
        gf2 = final_g.reshape(1, D_MODEL)
        in_specs.append(whole(gf2))
        args.append(gf2)
    return pl.pallas_call(
        body,
        grid=(n_tiles,),
        in_specs=in_specs,
        out_specs=pl.BlockSpec(out_block, out_map),
        out_shape=jax.ShapeDtypeStruct(out_shape, F32),
        compiler_params=pltpu.CompilerParams(
            dimension_semantics=("arbitrary",), vmem_limit_bytes=VMEM_LIMIT),
        name="ffn_resident",
    )(*args)


def _ssm_prep_kernel(lam_re_ref, lam_im_ref, ldt_ref, lam_re_rep_ref, lam_im_rep_ref,
                     b_re_ref, b_im_ref, lb_re_ref, lb_im_ref, bb_re_ref, bb_im_ref):
    dt = jnp.exp(ldt_ref[...])

    def discretise(lam_re, lam_im):
        mag = jnp.exp(lam_re * dt)
        lb_re = mag * jnp.cos(lam_im * dt)
        lb_im = mag * jnp.sin(lam_im * dt)
        return lb_re, lb_im

    lb_re, lb_im = discretise(lam_re_ref[...], lam_im_ref[...])
    lb_re_ref[...] = lb_re
    lb_im_ref[...] = lb_im
    lam_re = lam_re_rep_ref[...]
    lam_im = lam_im_rep_ref[...]
    lbr, lbi = discretise(lam_re, lam_im)
    den = lam_re * lam_re + lam_im * lam_im
    nr = lbr - 1.0
    ni = lbi
    f_re = (nr * lam_re + ni * lam_im) / den
    f_im = (ni * lam_re - nr * lam_im) / den
    b_re = b_re_ref[...]
    b_im = b_im_ref[...]
    bb_re_ref[...] = f_re * b_re - f_im * b_im
    bb_im_ref[...] = f_re * b_im + f_im * b_re


def _ssm_prep(lam_re, lam_im, log_dt, b_re, b_im):
    pc = P_STATE * GROUP_SIZE
    rep = lambda a: jnp.repeat(a, GROUP_SIZE, axis=-1)
    shp = lambda n: jax.ShapeDtypeStruct((N_GROUPS, n), F32)
    return pl.pallas_call(
        _ssm_prep_kernel,
        out_shape=(shp(P_STATE), shp(P_STATE), shp(pc), shp(pc)),
        name="ssm_prep",
    )(lam_re, lam_im, log_dt.reshape(N_GROUPS, 1), rep(lam_re), rep(lam_im),
      b_re.reshape(N_GROUPS, pc), b_im.reshape(N_GROUPS, pc))


def _block_diag(w):
    gpb = N_GROUPS // SSM_BLOCKS
    a, b = w.shape[1], w.shape[2]
    tiled = jnp.tile(w.reshape(SSM_BLOCKS, gpb * a, b), (1, 1, gpb))
    row_group = lax.broadcasted_iota(jnp.int32, tiled.shape[1:], 0) // a
    col_group = lax.broadcasted_iota(jnp.int32, tiled.shape[1:], 1) // b
    return jnp.where((row_group == col_group)[None], tiled, 0.0)


def _s5_kernel(x_ref, g_ref, s0_re_ref, s0_im_ref, lb_re_ref, lb_im_ref,
               wb_re_ref, wb_im_ref, wc_re_ref, wc_im_ref, d_ref, wglu_ref,
               o_ref, new_re_ref, new_im_ref,
               st_re, st_im, bu_re, bu_im, *, steps, sub_steps, n_seq, carry_over_grid):
    sub_rows = sub_steps * n_seq
    if carry_over_grid:
        @pl.when(pl.program_id(0) == 0)
        def _():
            st_re[...] = s0_re_ref[...]
            st_im[...] = s0_im_ref[...]
    else:
        st_re[...] = s0_re_ref[...]
        st_im[...] = s0_im_ref[...]

    for s in range(steps // sub_steps):
        t0 = s * sub_steps
        if len(x_ref.shape) == 3:
            x = x_ref[t0:t0 + sub_steps].reshape(sub_rows, D_MODEL)
        else:
            x = x_ref[t0 * n_seq:t0 * n_seq + sub_rows, :]
        u = _rmsnorm(x, g_ref[...])
        ub = u.astype(BF16)

        def project_in(k):
            uk = ub[:, k * SSM_BLOCK_CH:(k + 1) * SSM_BLOCK_CH]
            bu_re[k % 2] = _dot(uk, wb_re_ref[k])
            bu_im[k % 2] = _dot(uk, wb_im_ref[k])

        def scan_block(k):
            b_re, b_im = bu_re.at[k % 2], bu_im.at[k % 2]
            for c in range(SSM_BLOCK_ST // SCAN_LANES):
                lo = c * SCAN_LANES
                glo = k * SSM_BLOCK_ST + lo
                lr = jnp.broadcast_to(lb_re_ref[:, glo:glo + SCAN_LANES], (SUBLANES, SCAN_LANES))
                li = jnp.broadcast_to(lb_im_ref[:, glo:glo + SCAN_LANES], (SUBLANES, SCAN_LANES))
                for r_state in range(0, n_seq, SUBLANES):
                    hr = st_re[r_state:r_state + SUBLANES, glo:glo + SCAN_LANES]
                    hi = st_im[r_state:r_state + SUBLANES, glo:glo + SCAN_LANES]
                    for t in range(sub_steps):
                        r = t * n_seq + r_state
                        br = b_re[r:r + SUBLANES, lo:lo + SCAN_LANES]
                        bi = b_im[r:r + SUBLANES, lo:lo + SCAN_LANES]
                        hr, hi = lr * hr - li * hi + br, lr * hi + li * hr + bi
                        b_re[r:r + SUBLANES, lo:lo + SCAN_LANES] = hr
                        b_im[r:r + SUBLANES, lo:lo + SCAN_LANES] = hi
                    st_re[r_state:r_state + SUBLANES, glo:glo + SCAN_LANES] = hr
                    st_im[r_state:r_state + SUBLANES, glo:glo + SCAN_LANES] = hi

        ys = []
        project_in(0)
        for k in range(SSM_BLOCKS):
            if k + 1 < SSM_BLOCKS:
                project_in(k + 1)
            scan_block(k)
            hr = bu_re[k % 2].astype(BF16)
            hi = bu_im[k % 2].astype(BF16)
            ys.append(_dot(hr, wc_re_ref[k]) - _dot(hi, wc_im_ref[k]))
        y = jnp.concatenate(ys, axis=-1) + d_ref[...] * u
        z = _dot(jax.nn.gelu(y).astype(BF16), wglu_ref[...])
        m = z[:, :D_MODEL] * jax.nn.sigmoid(z[:, D_MODEL:])
        if len(o_ref.shape) == 3:
            o_ref[t0:t0 + sub_steps] = (x + m).reshape(sub_steps, n_seq, D_MODEL)
        else:
            o_ref[t0 * n_seq:t0 * n_seq + sub_rows, :] = x + m
    new_re_ref[...] = st_re[...]
    new_im_ref[...] = st_im[...]


def _s5_call(x, x_block, x_map, grid, s0_re, s0_im, state_map, g, lb_re, lb_im,
             wb_re, wb_im, wc_re, wc_im, d_skip, wglu, *, steps, sub_steps, n_seq, carry_over_grid):
    sub_rows = sub_steps * n_seq
    n_state = s0_re.shape[0]
    whole = lambda a: pl.BlockSpec(a.shape, lambda i: (0,) * a.ndim)
    state_spec = pl.BlockSpec((n_seq, S_DIM), state_map)
    g2 = g.reshape(1, D_MODEL)
    d2 = d_skip.reshape(1, D_MODEL)
    lbr = lb_re.reshape(1, S_DIM)
    lbi = lb_im.reshape(1, S_DIM)
    body = functools.partial(_s5_kernel, steps=steps, sub_steps=sub_steps, n_seq=n_seq,
                             carry_over_grid=carry_over_grid)
    return pl.pallas_call(
        body,
        grid=grid,
        in_specs=[pl.BlockSpec(x_block, x_map), whole(g2), state_spec, state_spec,
                  whole(lbr), whole(lbi), whole(wb_re), whole(wb_im), whole(wc_re), whole(wc_im),
                  whole(d2), whole(wglu)],
        out_specs=(pl.BlockSpec(x_block, x_map), state_spec, state_spec),
        out_shape=(jax.ShapeDtypeStruct(x.shape, F32),
                   jax.ShapeDtypeStruct((n_state, S_DIM), F32),
                   jax.ShapeDtypeStruct((n_state, S_DIM), F32)),
        scratch_shapes=[pltpu.VMEM((n_seq, S_DIM), F32), pltpu.VMEM((n_seq, S_DIM), F32),
                        pltpu.VMEM((2, sub_rows, SSM_BLOCK_ST), F32),
                        pltpu.VMEM((2, sub_rows, SSM_BLOCK_ST), F32)],
        compiler_params=pltpu.CompilerParams(
            dimension_semantics=("arbitrary",), vmem_limit_bytes=VMEM_LIMIT),
        name="s5_mixer",
    )(x, g2, s0_re, s0_im, lbr, lbi, wb_re, wb_im, wc_re, wc_im, d2, wglu)


def _conv_kernel(x_ref, g_ref, buf0_ref, win_ref, cw_ref, wout_ref, o_ref, newbuf_ref, zp,
                 *, steps, sub_steps, n_seq, carry_over_grid):
    rows = steps * n_seq
    halo = (CONV_W - 1) * n_seq

    def load_buf0():
        for k in range(CONV_W - 1):
            zp[k * n_seq:(k + 1) * n_seq, :] = buf0_ref[:, k, :]

    if carry_over_grid:
        pl.when(pl.program_id(0) == 0)(load_buf0)
    else:
        load_buf0()

    sub_rows = sub_steps * n_seq
    for s in range(steps // sub_steps):
        t0 = s * sub_steps
        r0 = t0 * n_seq
        if len(x_ref.shape) == 3:
            x = x_ref[t0:t0 + sub_steps].reshape(sub_rows, D_MODEL)
        else:
            x = x_ref[r0:r0 + sub_rows, :]
        h = _rmsnorm(x, g_ref[...]).astype(BF16)
        p = _dot(h, win_ref[...])
        gb = p[:, :D_MODEL]
        zp[halo + r0:halo + r0 + sub_rows, :] = p[:, D_MODEL:2 * D_MODEL] * p[:, 2 * D_MODEL:]
        conv = cw_ref[0:1, :] * zp[r0:r0 + sub_rows, :]
        for k in range(1, CONV_W):
            conv = conv + cw_ref[k:k + 1, :] * zp[r0 + k * n_seq:r0 + k * n_seq + sub_rows, :]
        m = _dot((gb * conv).astype(BF16), wout_ref[...])
        if len(o_ref.shape) == 3:
            o_ref[t0:t0 + sub_steps] = (x + m).reshape(sub_steps, n_seq, D_MODEL)
        else:
            o_ref[r0:r0 + sub_rows, :] = x + m
    tail = zp[rows:rows + halo, :]
    zp[0:halo, :] = tail
    for k in range(CONV_W - 1):
        newbuf_ref[:, k, :] = tail[k * n_seq:(k + 1) * n_seq, :]


def _conv_call(x, x_block, x_map, grid, buf0, buf_map, g, w_in, cw, w_out,
               *, steps, sub_steps, n_seq, carry_over_grid):
    rows = steps * n_seq
    halo = (CONV_W - 1) * n_seq
    whole = lambda a: pl.BlockSpec(a.shape, lambda i: (0,) * a.ndim)
    g2 = g.reshape(1, D_MODEL)
    buf_block = (n_seq, CONV_W - 1, D_MODEL)
    body = functools.partial(_conv_kernel, steps=steps, sub_steps=sub_steps, n_seq=n_seq,
                             carry_over_grid=carry_over_grid)
    return pl.pallas_call(
        body,
        grid=grid,
        in_specs=[pl.BlockSpec(x_block, x_map), whole(g2), pl.BlockSpec(buf_block, buf_map),
                  whole(w_in), whole(cw), whole(w_out)],
        out_specs=(pl.BlockSpec(x_block, x_map), pl.BlockSpec(buf_block, buf_map)),
        out_shape=(jax.ShapeDtypeStruct(x.shape, F32), jax.ShapeDtypeStruct(buf0.shape, F32)),
        scratch_shapes=[pltpu.VMEM((halo + rows, D_MODEL), F32)],
        compiler_params=pltpu.CompilerParams(
            dimension_semantics=("arbitrary",), vmem_limit_bytes=VMEM_LIMIT),
        name="conv_mixer",
    )(x, g2, buf0, w_in, cw, w_out)


S5_PROMPT_STEPS = 64
S5_PROMPT_SUB_STEPS = 32
CONV_PROMPT_STEPS = 128
CONV_PROMPT_SUB_STEPS = 32
SAMPLE_SEQ_BLOCK = 32


def kernel(x_prompt, x_sample, state_ssm_re, state_ssm_im, cache_conv, norm_g, final_norm_g, ffn_w_gate_up, ffn_w_down, ssm_lam_re, ssm_lam_im, ssm_log_dt, ssm_b_re, ssm_b_im, ssm_c_re, ssm_c_im, ssm_d, ssm_w_glu, conv_w_in, conv_w, conv_w_out):
    nb_p, len_p, _ = x_prompt.shape
    nb_s, len_s, _ = x_sample.shape
    rows_p = nb_p * len_p
    rows_s = nb_s * len_s
    assert nb_p == SUBLANES and len_p % FFN_ROWS == 0 and rows_s == FFN_ROWS
    assert nb_s % SAMPLE_SEQ_BLOCK == 0

    lb_re, lb_im, bb_re, bb_im = _ssm_prep(ssm_lam_re[0], ssm_lam_im[0], ssm_log_dt[0],
                                           ssm_b_re[0], ssm_b_im[0])
    to_gcp = lambda a: jnp.swapaxes(a.reshape(N_GROUPS, P_STATE, GROUP_SIZE), 1, 2)
    wb_re = _block_diag(to_gcp(bb_re)).astype(BF16)
    wb_im = _block_diag(to_gcp(bb_im)).astype(BF16)
    wc_re = _block_diag(jnp.swapaxes(ssm_c_re[0], 1, 2)).astype(BF16)
    wc_im = _block_diag(jnp.swapaxes(ssm_c_im[0], 1, 2)).astype(BF16)
    wglu = ssm_w_glu[0].astype(BF16)
    w_in = conv_w_in[0].astype(BF16)
    w_out = conv_w_out[0].astype(BF16)

    t_tiles = len_p // FFN_ROWS
    row_tiles_p = rows_p // FFN_ROWS
    flat_block = (FFN_ROWS, D_MODEL)
    flat_map = lambda i: (i, 0)

    whole_s = (x_sample.shape, lambda i: (0, 0, 0))
    flat_s = (flat_block, flat_map)

    def ffn_sample(xs, layer, half, x_spec=flat_s, out_spec=flat_s, final_g=None):
        out_shape = x_sample.shape if out_spec is whole_s else (rows_s, D_MODEL)
        return _ffn_call(xs, *x_spec, out_shape, *out_spec, 1,
                         norm_g[layer, 2 * half], (ffn_w_gate_up, ffn_w_down, (layer, half)),
                         FFN_COLS_F32, final_g=final_g)

    xs, w_bf_00 = ffn_sample(x_sample, 0, 0, x_spec=whole_s)
    seq_block = (len_s, SAMPLE_SEQ_BLOCK, D_MODEL)
    seq_map = lambda i: (0, i, 0)
    seq_grid = (nb_s // SAMPLE_SEQ_BLOCK,)
    xs, sre_s, sim_s = _s5_call(
        xs.reshape(len_s, nb_s, D_MODEL), seq_block, seq_map, seq_grid,
        state_ssm_re[0].reshape(nb_s, S_DIM), state_ssm_im[0].reshape(nb_s, S_DIM), lambda i: (i, 0),
        norm_g[0, 1], lb_re, lb_im, wb_re, wb_im, wc_re, wc_im, ssm_d[0], wglu,
        steps=len_s, sub_steps=len_s, n_seq=SAMPLE_SEQ_BLOCK, carry_over_grid=False)
    xs = xs.reshape(rows_s, D_MODEL)
    xs, w_bf_01 = ffn_sample(xs, 0, 1)
    xs, w_bf_10 = ffn_sample(xs, 1, 0)
    xs, buf_s = _conv_call(
        xs.reshape(len_s, nb_s, D_MODEL), seq_block, seq_map, seq_grid,
        cache_conv[0], lambda i: (i, 0, 0), norm_g[1, 1], w_in, conv_w[0], w_out,
        steps=len_s, sub_steps=len_s, n_seq=SAMPLE_SEQ_BLOCK, carry_over_grid=False)
    xs = xs.reshape(rows_s, D_MODEL)
    y_sample, w_bf_11 = ffn_sample(xs, 1, 1, out_spec=whole_s, final_g=final_norm_g)

    def ffn(x, x_block, x_map, out_shape, out_block, out_map, g, w_bf, final_g=None):
        return _ffn_resident_call(x, x_block, x_map, out_shape, out_block, out_map, row_tiles_p, g, w_bf,
                                  FFN_COLS, final_g=final_g)

    seq_major_block = (nb_p, FFN_ROWS // nb_p, D_MODEL)
    seq_major_map = lambda i: (0, i, 0)
    xp = ffn(x_prompt, seq_major_block, seq_major_map, (rows_p, D_MODEL), flat_block, flat_map,
             norm_g[0, 0], w_bf_00)
    zero_state = jnp.zeros((nb_p, S_DIM), F32)
    xp, sre_p, sim_p = _s5_call(
        xp, (S5_PROMPT_STEPS * nb_p, D_MODEL), flat_map, (len_p // S5_PROMPT_STEPS,),
        zero_state, zero_state, lambda i: (0, 0), norm_g[0, 1], lb_re, lb_im,
        wb_re, wb_im, wc_re, wc_im, ssm_d[0], wglu,
        steps=S5_PROMPT_STEPS, sub_steps=S5_PROMPT_SUB_STEPS, n_seq=nb_p, carry_over_grid=True)
    xp = ffn(xp, flat_block, flat_map, (rows_p, D_MODEL), flat_block, flat_map, norm_g[0, 2], w_bf_01)
    xp = ffn(xp, flat_block, flat_map, (rows_p, D_MODEL), flat_block, flat_map, norm_g[1, 0], w_bf_10)
    xp, buf_p = _conv_call(
        xp, (CONV_PROMPT_STEPS * nb_p, D_MODEL), flat_map, (len_p // CONV_PROMPT_STEPS,),
        jnp.zeros((nb_p, CONV_W - 1, D_MODEL), F32), lambda i: (0, 0, 0),
        norm_g[1, 1], w_in, conv_w[0], w_out,
        steps=CONV_PROMPT_STEPS, sub_steps=CONV_PROMPT_SUB_STEPS, n_seq=nb_p, carry_over_grid=True)
    y_prompt = ffn(xp, flat_block, flat_map, (nb_p, len_p, D_MODEL), seq_major_block, seq_major_map,
                   norm_g[1, 2], w_bf_11, final_g=final_norm_g)

    state4 = lambda s: s.reshape(1, -1, N_GROUPS, P_STATE)
    return (y_prompt, y_sample, state4(sre_p), state4(sim_p), buf_p[None],
            state4(sre_s), state4(sim_s), buf_s[None])
```

```python
import functools

import jax
import jax.numpy as jnp
from jax import lax
from jax.experimental import pallas as pl
from jax.experimental.pallas import tpu as pltpu

F32 = jnp.float32
BF16 = jnp.bfloat16

D_MODEL = 1024
D_FF = 4 * D_MODEL
GROUP_SIZE = 16
N_GROUPS = D_MODEL // GROUP_SIZE
P_STATE = 64
S_DIM = N_GROUPS * P_STATE
CONV_W = 3
EPS = 1e-6

SUBLANES = 8
FFN_ROWS = 1024
FFN_COLS = 1024
FFN_COLS_F32 = 512
FFN_ROW_BLOCK = 256
SSM_BLOCKS = 4
SSM_BLOCK_CH = D_MODEL // SSM_BLOCKS
SSM_BLOCK_ST = S_DIM // SSM_BLOCKS
SCAN_LANES = 256
VMEM_LIMIT = 56 * 1024 * 1024


def _rmsnorm(x, g):
    return x * lax.rsqrt(jnp.mean(x * x, axis=-1, keepdims=True) + EPS) * g


def _dot(a, b):
    return jnp.dot(a, b, preferred_element_type=F32)


def _ffn_kernel(x_ref, g_ref, wg_ref, wu_ref, wd_ref, *rest, n_chunks, final_norm,
                x_seq_major, out_seq_major, emit_bf16):
    rest = list(rest)
    gf_ref = rest.pop(0) if final_norm else None
    o_ref = rest.pop(0)
    if emit_bf16:
        wg_bf, wu_bf, wd_bf = rest[:3]
        del rest[:3]
    h_ref, acc_ref = rest[:2]
    xt_ref = rest[2] if x_seq_major else x_ref
    j = pl.program_id(1)

    @pl.when(j == 0)
    def _():
        if x_seq_major:
            n_seq = x_ref.shape[0]
            for t in range(x_ref.shape[1]):
                xt_ref[t * n_seq:(t + 1) * n_seq, :] = x_ref[:, t, :]
        h_ref[...] = _rmsnorm(xt_ref[...], g_ref[...]).astype(BF16)
        acc_ref[...] = jnp.zeros_like(acc_ref)

    if emit_bf16:
        wg_bf[...] = wg_ref[...].astype(BF16)
        wu_bf[...] = wu_ref[...].astype(BF16)
        wd_bf[...] = wd_ref[...].astype(BF16)
        wg_ref, wu_ref, wd_ref = wg_bf, wu_bf, wd_bf

    for r in range(0, FFN_ROWS, FFN_ROW_BLOCK):
        h = h_ref[r:r + FFN_ROW_BLOCK, :]
        gate = _dot(h, wg_ref[...])
        up = _dot(h, wu_ref[...])
        act = (jax.nn.silu(gate) * up).astype(BF16)
        acc_ref[r:r + FFN_ROW_BLOCK, :] += _dot(act, wd_ref[...])

    @pl.when(j == n_chunks - 1)
    def _():
        out = xt_ref[...] + 0.5 * acc_ref[...]
        if final_norm:
            out = _rmsnorm(out, gf_ref[...])
        if out_seq_major:
            n_seq = o_ref.shape[0]
            for t in range(o_ref.shape[1]):
                o_ref[:, t, :] = out[t * n_seq:(t + 1) * n_seq, :]
        else:
            o_ref[...] = out


def _ffn_call(x, x_block, x_map, out_shape, out_block, out_map, n_tiles, g, weights, cols,
              final_g=None):
    n_chunks = D_FF // cols
    emit_bf16 = isinstance(weights[-1], tuple)
    x_seq_major = len(x_block) == 3
    out_seq_major = len(out_block) == 3
    body = functools.partial(_ffn_kernel, n_chunks=n_chunks, final_norm=final_g is not None,
                             x_seq_major=x_seq_major, out_seq_major=out_seq_major, emit_bf16=emit_bf16)
    scratch = [pltpu.VMEM((FFN_ROWS, D_MODEL), BF16), pltpu.VMEM((FFN_ROWS, D_MODEL), F32)]
    if x_seq_major:
        scratch.append(pltpu.VMEM((FFN_ROWS, D_MODEL), F32))

    const2 = lambda i, j: (0, 0)
    gate_bf = pl.BlockSpec((D_MODEL, cols), lambda i, j: (0, j))
    down_bf = pl.BlockSpec((cols, D_MODEL), lambda i, j: (j, 0))
    out_specs = pl.BlockSpec(out_block, lambda i, j: out_map(i))
    out_shapes = jax.ShapeDtypeStruct(out_shape, F32)
    if emit_bf16:
        w_gu, w_down, (layer, half) = weights
        w_specs = [
            pl.BlockSpec((None, None, D_MODEL, cols), lambda i, j: (layer, half, 0, j)),
            pl.BlockSpec((None, None, D_MODEL, cols), lambda i, j: (layer, half, 0, j + n_chunks)),
            pl.BlockSpec((None, None, cols, D_MODEL), lambda i, j: (layer, half, j, 0)),
        ]
        w_args = [w_gu, w_gu, w_down]
        out_specs = (out_specs, gate_bf, gate_bf, down_bf)
        out_shapes = (out_shapes,
                      jax.ShapeDtypeStruct((D_MODEL, D_FF), BF16),
                      jax.ShapeDtypeStruct((D_MODEL, D_FF), BF16),
                      jax.ShapeDtypeStruct((D_FF, D_MODEL), BF16))
    else:
        w_specs = [gate_bf, gate_bf, down_bf]
        w_args = list(weights)
    in_specs = [pl.BlockSpec(x_block, lambda i, j: x_map(i)), pl.BlockSpec((1, D_MODEL), const2)] + w_specs
    args = [x, g.reshape(1, D_MODEL)] + w_args
    if final_g is not None:
        in_specs.append(pl.BlockSpec((1, D_MODEL), const2))
        args.append(final_g.reshape(1, D_MODEL))
    res = pl.pallas_call(
        body,
        grid=(n_tiles, n_chunks),
        in_specs=in_specs,
        out_specs=out_specs,
        out_shape=out_shapes,
        scratch_shapes=scratch,
        compiler_params=pltpu.CompilerParams(
            dimension_semantics=("arbitrary", "arbitrary"), vmem_limit_bytes=VMEM_LIMIT),
        name="ffn",
    )(*args)
    return (res[0], tuple(res[1:])) if emit_bf16 else res


def _ffn_resident_kernel(x_ref, g_ref, wg_ref, wu_ref, wd_ref, *rest, cols, final_norm,
                         x_seq_major, out_seq_major):
    rest = list(rest)
    gf_ref = rest.pop(0) if final_norm else None
    o_ref = rest.pop(0)
    if x_seq_major:
        xbuf, xsem = rest[:2]
        del rest[:2]
    if out_seq_major:
        obuf, osem = rest[:2]
    i = pl.program_id(0)
    n_tiles = pl.num_programs(0)
    slot = i % 2
    tile_steps = FFN_ROWS // SUBLANES
    steps_per_block = FFN_ROW_BLOCK // SUBLANES

    def x_copies(tile, slot_):
        return [pltpu.make_async_copy(x_ref.at[s, pl.ds(tile * tile_steps, tile_steps), :],
                                      xbuf.at[slot_, :, s, :], xsem.at[slot_]) for s in range(SUBLANES)]

    def o_copies(tile, slot_):
        return [pltpu.make_async_copy(obuf.at[slot_, :, s, :],
                                      o_ref.at[s, pl.ds(tile * tile_steps, tile_steps), :],
                                      osem.at[slot_]) for s in range(SUBLANES)]

    if x_seq_major:
        @pl.when(i == 0)
        def _():
            for cp in x_copies(0, 0):
                cp.start()

        @pl.when(i + 1 < n_tiles)
        def _():
            for cp in x_copies(i + 1, 1 - slot):
                cp.start()

        for cp in x_copies(i, slot):
            cp.wait()
    if out_seq_major:
        @pl.when(i >= 2)
        def _():
            for cp in o_copies(i - 2, slot):
                cp.wait()

    for r in range(0, FFN_ROWS, FFN_ROW_BLOCK):
        t0 = r // SUBLANES
        if x_seq_major:
            x = xbuf[slot, t0:t0 + steps_per_block].reshape(FFN_ROW_BLOCK, D_MODEL)
        else:
            x = x_ref[r:r + FFN_ROW_BLOCK, :]
        h = _rmsnorm(x, g_ref[...]).astype(BF16)
        acc = None
        for c in range(0, D_FF, cols):
            gate = _dot(h, wg_ref[:, c:c + cols])
            up = _dot(h, wu_ref[:, c:c + cols])
            act = (jax.nn.silu(gate) * up).astype(BF16)
            part = _dot(act, wd_ref[c:c + cols, :])
            acc = part if acc is None else acc + part
        out = x + 0.5 * acc
        if final_norm:
            out = _rmsnorm(out, gf_ref[...])
        if out_seq_major:
            obuf[slot, t0:t0 + steps_per_block] = out.reshape(steps_per_block, SUBLANES, D_MODEL)
        else:
            o_ref[r:r + FFN_ROW_BLOCK, :] = out

    if out_seq_major:
        for cp in o_copies(i, slot):
            cp.start()

        @pl.when(i == n_tiles - 1)
        def _():
            for cp in o_copies(i, slot):
                cp.wait()

        @pl.when(jnp.logical_and(i == n_tiles - 1, i >= 1))
        def _():
            for cp in o_copies(i - 1, 1 - slot):
                cp.wait()


def _ffn_resident_call(x, out_shape, n_tiles, g, weights, cols, final_g=None):
    x_seq_major = x.ndim == 3
    out_seq_major = len(out_shape) == 3
    body = functools.partial(_ffn_resident_kernel, cols=cols, final_norm=final_g is not None,
                             x_seq_major=x_seq_major, out_seq_major=out_seq_major)
    whole = lambda a: pl.BlockSpec(a.shape, lambda i: (0,) * a.ndim)
    row_tile = pl.BlockSpec((FFN_ROWS, D_MODEL), lambda i: (i, 0))
    in_hbm = pl.BlockSpec(memory_space=pl.ANY)
    g2 = g.reshape(1, D_MODEL)
    in_specs = [in_hbm if x_seq_major else row_tile, whole(g2)] + [whole(w) for w in weights]
    args = [x, g2, *weights]
    if final_g is not None:
        gf2 = final_g.reshape(1, D_MODEL)
        in_specs.append(whole(gf2))
        args.append(gf2)
    reorder_scratch = [pltpu.VMEM((2, FFN_ROWS // SUBLANES, SUBLANES, D_MODEL), F32),
                       pltpu.SemaphoreType.DMA((2,))]
    scratch = (reorder_scratch if x_seq_major else []) + (reorder_scratch if out_seq_major else [])
    return pl.pallas_call(
        body,
        grid=(n_tiles,),
        in_specs=in_specs,
        out_specs=in_hbm if out_seq_major else row_tile,
        out_shape=jax.ShapeDtypeStruct(out_shape, F32),
        scratch_shapes=scratch,
        compiler_params=pltpu.CompilerParams(
            dimension_semantics=("arbitrary",), vmem_limit_bytes=VMEM_LIMIT),
        name="ffn_resident",
    )(*args)


def _ssm_prep_kernel(lam_re_ref, lam_im_ref, ldt_ref, lam_re_rep_ref, lam_im_rep_ref,
                     b_re_ref, b_im_ref, lb_re_ref, lb_im_ref, bb_re_ref, bb_im_ref):
    dt = jnp.exp(ldt_ref[...])

    def discretise(lam_re, lam_im):
        mag = jnp.exp(lam_re * dt)
        lb_re = mag * jnp.cos(lam_im * dt)
        lb_im = mag * jnp.sin(lam_im * dt)
        return lb_re, lb_im

    lb_re, lb_im = discretise(lam_re_ref[...], lam_im_ref[...])
    lb_re_ref[...] = lb_re
    lb_im_ref[...] = lb_im
    lam_re = lam_re_rep_ref[...]
    lam_im = lam_im_rep_ref[...]
    lbr, lbi = discretise(lam_re, lam_im)
    den = lam_re * lam_re + lam_im * lam_im
    nr = lbr - 1.0
    ni = lbi
    f_re = (nr * lam_re + ni * lam_im) / den
    f_im = (ni * lam_re - nr * lam_im) / den
    b_re = b_re_ref[...]
    b_im = b_im_ref[...]
    bb_re_ref[...] = f_re * b_re - f_im * b_im
    bb_im_ref[...] = f_re * b_im + f_im * b_re


def _ssm_prep(lam_re, lam_im, log_dt, b_re, b_im):
    pc = P_STATE * GROUP_SIZE
    rep = lambda a: jnp.repeat(a, GROUP_SIZE, axis=-1)
    shp = lambda n: jax.ShapeDtypeStruct((N_GROUPS, n), F32)
    return pl.pallas_call(
        _ssm_prep_kernel,
        out_shape=(shp(P_STATE), shp(P_STATE), shp(pc), shp(pc)),
        name="ssm_prep",
    )(lam_re, lam_im, log_dt.reshape(N_GROUPS, 1), rep(lam_re), rep(lam_im),
      b_re.reshape(N_GROUPS, pc), b_im.reshape(N_GROUPS, pc))


def _block_diag(w):
    gpb = N_GROUPS // SSM_BLOCKS
    a, b = w.shape[1], w.shape[2]
    tiled = jnp.tile(w.reshape(SSM_BLOCKS, gpb * a, b), (1, 1, gpb))
    row_group = lax.broadcasted_iota(jnp.int32, tiled.shape[1:], 0) // a
    col_group = lax.broadcasted_iota(jnp.int32, tiled.shape[1:], 1) // b
    return jnp.where((row_group == col_group)[None], tiled, 0.0)


def _s5_kernel(x_ref, g_ref, s0_re_ref, s0_im_ref, lb_re_ref, lb_im_ref,
               wb_re_ref, wb_im_ref, wc_re_ref, wc_im_ref, d_ref, wglu_ref,
               o_ref, new_re_ref, new_im_ref,
               st_re, st_im, bu_re, bu_im, *, steps, sub_steps, n_seq, carry_over_grid):
    sub_rows = sub_steps * n_seq
    if carry_over_grid:
        @pl.when(pl.program_id(0) == 0)
        def _():
            st_re[...] = s0_re_ref[...]
            st_im[...] = s0_im_ref[...]
    else:
        st_re[...] = s0_re_ref[...]
        st_im[...] = s0_im_ref[...]

    for s in range(steps // sub_steps):
        t0 = s * sub_steps
        if len(x_ref.shape) == 3:
            x = x_ref[t0:t0 + sub_steps].reshape(sub_rows, D_MODEL)
        else:
            x = x_ref[t0 * n_seq:t0 * n_seq + sub_rows, :]
        u = _rmsnorm(x, g_ref[...])
        ub = u.astype(BF16)

        def project_in(k):
            uk = ub[:, k * SSM_BLOCK_CH:(k + 1) * SSM_BLOCK_CH]
            bu_re[k % 2] = _dot(uk, wb_re_ref[k])
            bu_im[k % 2] = _dot(uk, wb_im_ref[k])

        def scan_block(k):
            b_re, b_im = bu_re.at[k % 2], bu_im.at[k % 2]
            for c in range(SSM_BLOCK_ST // SCAN_LANES):
                lo = c * SCAN_LANES
                glo = k * SSM_BLOCK_ST + lo
                lr = jnp.broadcast_to(lb_re_ref[:, glo:glo + SCAN_LANES], (SUBLANES, SCAN_LANES))
                li = jnp.broadcast_to(lb_im_ref[:, glo:glo + SCAN_LANES], (SUBLANES, SCAN_LANES))
                for r_state in range(0, n_seq, SUBLANES):
                    hr = st_re[r_state:r_state + SUBLANES, glo:glo + SCAN_LANES]
                    hi = st_im[r_state:r_state + SUBLANES, glo:glo + SCAN_LANES]
                    for t in range(sub_steps):
                        r = t * n_seq + r_state
                        br = b_re[r:r + SUBLANES, lo:lo + SCAN_LANES]
                        bi = b_im[r:r + SUBLANES, lo:lo + SCAN_LANES]
                        hr, hi = lr * hr - li * hi + br, lr * hi + li * hr + bi
                        b_re[r:r + SUBLANES, lo:lo + SCAN_LANES] = hr
                        b_im[r:r + SUBLANES, lo:lo + SCAN_LANES] = hi
                    st_re[r_state:r_state + SUBLANES, glo:glo + SCAN_LANES] = hr
                    st_im[r_state:r_state + SUBLANES, glo:glo + SCAN_LANES] = hi

        ys = []
        project_in(0)
        for k in range(SSM_BLOCKS):
            if k + 1 < SSM_BLOCKS:
                project_in(k + 1)
            scan_block(k)
            hr = bu_re[k % 2].astype(BF16)
            hi = bu_im[k % 2].astype(BF16)
            ys.append(_dot(hr, wc_re_ref[k]) - _dot(hi, wc_im_ref[k]))
        y = jnp.concatenate(ys, axis=-1) + d_ref[...] * u
        z = _dot(jax.nn.gelu(y).astype(BF16), wglu_ref[...])
        m = z[:, :D_MODEL] * jax.nn.sigmoid(z[:, D_MODEL:])
        if len(o_ref.shape) == 3:
            o_ref[t0:t0 + sub_steps] = (x + m).reshape(sub_steps, n_seq, D_MODEL)
        else:
            o_ref[t0 * n_seq:t0 * n_seq + sub_rows, :] = x + m
    new_re_ref[...] = st_re[...]
    new_im_ref[...] = st_im[...]


def _s5_call(x, x_block, x_map, grid, s0_re, s0_im, state_map, g, lb_re, lb_im,
             wb_re, wb_im, wc_re, wc_im, d_skip, wglu, *, steps, sub_steps, n_seq, carry_over_grid):
    sub_rows = sub_steps * n_seq
    n_state = s0_re.shape[0]
    whole = lambda a: pl.BlockSpec(a.shape, lambda i: (0,) * a.ndim)
    state_spec = pl.BlockSpec((n_seq, S_DIM), state_map)
    g2 = g.reshape(1, D_MODEL)
    d2 = d_skip.reshape(1, D_MODEL)
    lbr = lb_re.reshape(1, S_DIM)
    lbi = lb_im.reshape(1, S_DIM)
    body = functools.partial(_s5_kernel, steps=steps, sub_steps=sub_steps, n_seq=n_seq,
                             carry_over_grid=carry_over_grid)
    return pl.pallas_call(
        body,
        grid=grid,
        in_specs=[pl.BlockSpec(x_block, x_map), whole(g2), state_spec, state_spec,
                  whole(lbr), whole(lbi), whole(wb_re), whole(wb_im), whole(wc_re), whole(wc_im),
                  whole(d2), whole(wglu)],
        out_specs=(pl.BlockSpec(x_block, x_map), state_spec, state_spec),
        out_shape=(jax.ShapeDtypeStruct(x.shape, F32),
                   jax.ShapeDtypeStruct((n_state, S_DIM), F32),
                   jax.ShapeDtypeStruct((n_state, S_DIM), F32)),
        scratch_shapes=[pltpu.VMEM((n_seq, S_DIM), F32), pltpu.VMEM((n_seq, S_DIM), F32),
                        pltpu.VMEM((2, sub_rows, SSM_BLOCK_ST), F32),
                        pltpu.VMEM((2, sub_rows, SSM_BLOCK_ST), F32)],
        compiler_params=pltpu.CompilerParams(
            dimension_semantics=("arbitrary",), vmem_limit_bytes=VMEM_LIMIT),
        name="s5_mixer",
    )(x, g2, s0_re, s0_im, lbr, lbi, wb_re, wb_im, wc_re, wc_im, d2, wglu)


def _conv_kernel(x_ref, g_ref, buf0_ref, win_ref, cw_ref, wout_ref, o_ref, newbuf_ref, zp,
                 *, steps, sub_steps, n_seq, carry_over_grid):
    rows = steps * n_seq
    halo = (CONV_W - 1) * n_seq

    def load_buf0():
        for k in range(CONV_W - 1):
            zp[k * n_seq:(k + 1) * n_seq, :] = buf0_ref[:, k, :]

    if carry_over_grid:
        pl.when(pl.program_id(0) == 0)(load_buf0)
    else:
        load_buf0()

    sub_rows = sub_steps * n_seq
    for s in range(steps // sub_steps):
        t0 = s * sub_steps
        r0 = t0 * n_seq
        if len(x_ref.shape) == 3:
            x = x_ref[t0:t0 + sub_steps].reshape(sub_rows, D_MODEL)
        else:
            x = x_ref[r0:r0 + sub_rows, :]
        h = _rmsnorm(x, g_ref[...]).astype(BF16)
        p = _dot(h, win_ref[...])
        gb = p[:, :D_MODEL]
        zp[halo + r0:halo + r0 + sub_rows, :] = p[:, D_MODEL:2 * D_MODEL] * p[:, 2 * D_MODEL:]
        conv = cw_ref[0:1, :] * zp[r0:r0 + sub_rows, :]
        for k in range(1, CONV_W):
            conv = conv + cw_ref[k:k + 1, :] * zp[r0 + k * n_seq:r0 + k * n_seq + sub_rows, :]
        m = _dot((gb * conv).astype(BF16), wout_ref[...])
        if len(o_ref.shape) == 3:
            o_ref[t0:t0 + sub_steps] = (x + m).reshape(sub_steps, n_seq, D_MODEL)
        else:
            o_ref[r0:r0 + sub_rows, :] = x + m
    tail = zp[rows:rows + halo, :]
    zp[0:halo, :] = tail
    for k in range(CONV_W - 1):
        newbuf_ref[:, k, :] = tail[k * n_seq:(k + 1) * n_seq, :]


def _conv_call(x, x_block, x_map, grid, buf0, buf_map, g, w_in, cw, w_out,
               *, steps, sub_steps, n_seq, carry_over_grid):
    rows = steps * n_seq
    halo = (CONV_W - 1) * n_seq
    whole = lambda a: pl.BlockSpec(a.shape, lambda i: (0,) * a.ndim)
    g2 = g.reshape(1, D_MODEL)
    buf_block = (n_seq, CONV_W - 1, D_MODEL)
    body = functools.partial(_conv_kernel, steps=steps, sub_steps=sub_steps, n_seq=n_seq,
                             carry_over_grid=carry_over_grid)
    return pl.pallas_call(
        body,
        grid=grid,
        in_specs=[pl.BlockSpec(x_block, x_map), whole(g2), pl.BlockSpec(buf_block, buf_map),
                  whole(w_in), whole(cw), whole(w_out)],
        out_specs=(pl.BlockSpec(x_block, x_map), pl.BlockSpec(buf_block, buf_map)),
        out_shape=(jax.ShapeDtypeStruct(x.shape, F32), jax.ShapeDtypeStruct(buf0.shape, F32)),
        scratch_shapes=[pltpu.VMEM((halo + rows, D_MODEL), F32)],
        compiler_params=pltpu.CompilerParams(
            dimension_semantics=("arbitrary",), vmem_limit_bytes=VMEM_LIMIT),
        name="conv_mixer",
    )(x, g2, buf0, w_in, cw, w_out)


S5_PROMPT_STEPS = 64
S5_PROMPT_SUB_STEPS = 32
CONV_PROMPT_STEPS = 128
CONV_PROMPT_SUB_STEPS = 32
SAMPLE_SEQ_BLOCK = 32


def kernel(x_prompt, x_sample, state_ssm_re, state_ssm_im, cache_conv, norm_g, final_norm_g, ffn_w_gate_up, ffn_w_down, ssm_lam_re, ssm_lam_im, ssm_log_dt, ssm_b_re, ssm_b_im, ssm_c_re, ssm_c_im, ssm_d, ssm_w_glu, conv_w_in, conv_w, conv_w_out):
    nb_p, len_p, _ = x_prompt.shape
    nb_s, len_s, _ = x_sample.shape
    rows_p = nb_p * len_p
    rows_s = nb_s * len_s
    assert nb_p == SUBLANES and len_p % FFN_ROWS == 0 and rows_s == FFN_ROWS
    assert nb_s % SAMPLE_SEQ_BLOCK == 0

    lb_re, lb_im, bb_re, bb_im = _ssm_prep(ssm_lam_re[0], ssm_lam_im[0], ssm_log_dt[0],
                                           ssm_b_re[0], ssm_b_im[0])
    to_gcp = lambda a: jnp.swapaxes(a.reshape(N_GROUPS, P_STATE, GROUP_SIZE), 1, 2)
    wb_re = _block_diag(to_gcp(bb_re)).astype(BF16)
    wb_im = _block_diag(to_gcp(bb_im)).astype(BF16)
    wc_re = _block_diag(jnp.swapaxes(ssm_c_re[0], 1, 2)).astype(BF16)
    wc_im = _block_diag(jnp.swapaxes(ssm_c_im[0], 1, 2)).astype(BF16)
    wglu = ssm_w_glu[0].astype(BF16)
    w_in = conv_w_in[0].astype(BF16)
    w_out = conv_w_out[0].astype(BF16)

    t_tiles = len_p // FFN_ROWS
    row_tiles_p = rows_p // FFN_ROWS
    flat_block = (FFN_ROWS, D_MODEL)
    flat_map = lambda i: (i, 0)

    whole_s = (x_sample.shape, lambda i: (0, 0, 0))
    flat_s = (flat_block, flat_map)

    def ffn_sample(xs, layer, half, x_spec=flat_s, out_spec=flat_s, final_g=None):
        out_shape = x_sample.shape if out_spec is whole_s else (rows_s, D_MODEL)
        return _ffn_call(xs, *x_spec, out_shape, *out_spec, 1,
                         norm_g[layer, 2 * half], (ffn_w_gate_up, ffn_w_down, (layer, half)),
                         FFN_COLS_F32, final_g=final_g)

    xs, w_bf_00 = ffn_sample(x_sample, 0, 0, x_spec=whole_s)
    seq_block = (len_s, SAMPLE_SEQ_BLOCK, D_MODEL)
    seq_map = lambda i: (0, i, 0)
    seq_grid = (nb_s // SAMPLE_SEQ_BLOCK,)
    xs, sre_s, sim_s = _s5_call(
        xs.reshape(len_s, nb_s, D_MODEL), seq_block, seq_map, seq_grid,
        state_ssm_re[0].reshape(nb_s, S_DIM), state_ssm_im[0].reshape(nb_s, S_DIM), lambda i: (i, 0),
        norm_g[0, 1], lb_re, lb_im, wb_re, wb_im, wc_re, wc_im, ssm_d[0], wglu,
        steps=len_s, sub_steps=len_s, n_seq=SAMPLE_SEQ_BLOCK, carry_over_grid=False)
    xs = xs.reshape(rows_s, D_MODEL)
    xs, w_bf_01 = ffn_sample(xs, 0, 1)
    xs, w_bf_10 = ffn_sample(xs, 1, 0)
    xs, buf_s = _conv_call(
        xs.reshape(len_s, nb_s, D_MODEL), seq_block, seq_map, seq_grid,
        cache_conv[0], lambda i: (i, 0, 0), norm_g[1, 1], w_in, conv_w[0], w_out,
        steps=len_s, sub_steps=len_s, n_seq=SAMPLE_SEQ_BLOCK, carry_over_grid=False)
    xs = xs.reshape(rows_s, D_MODEL)
    y_sample, w_bf_11 = ffn_sample(xs, 1, 1, out_spec=whole_s, final_g=final_norm_g)

    def ffn(x, out_shape, g, w_bf, final_g=None):
        return _ffn_resident_call(x, out_shape, row_tiles_p, g, w_bf, FFN_COLS, final_g=final_g)

    xp = ffn(x_prompt, (rows_p, D_MODEL), norm_g[0, 0], w_bf_00)
    zero_state = jnp.zeros((nb_p, S_DIM), F32)
    xp, sre_p, sim_p = _s5_call(
        xp, (S5_PROMPT_STEPS * nb_p, D_MODEL), flat_map, (len_p // S5_PROMPT_STEPS,),
        zero_state, zero_state, lambda i: (0, 0), norm_g[0, 1], lb_re, lb_im,
        wb_re, wb_im, wc_re, wc_im, ssm_d[0], wglu,
        steps=S5_PROMPT_STEPS, sub_steps=S5_PROMPT_SUB_STEPS, n_seq=nb_p, carry_over_grid=True)
    xp = ffn(xp, (rows_p, D_MODEL), norm_g[0, 2], w_bf_01)
    xp = ffn(xp, (rows_p, D_MODEL), norm_g[1, 0], w_bf_10)
    xp, buf_p = _conv_call(
        xp, (CONV_PROMPT_STEPS * nb_p, D_MODEL), flat_map, (len_p // CONV_PROMPT_STEPS,),
        jnp.zeros((nb_p, CONV_W - 1, D_MODEL), F32), lambda i: (0, 0, 0),
        norm_g[1, 1], w_in, conv_w[0], w_out,
        steps=CONV_PROMPT_STEPS, sub_steps=CONV_PROMPT_SUB_STEPS, n_seq=nb_p, carry_over_grid=True)
    y_prompt = ffn(xp, x_prompt.shape, norm_g[1, 2], w_bf_11, final_g=final_norm_g)

    state4 = lambda s: s.reshape(1, -1, N_GROUPS, P_STATE)
    return (y_prompt, y_sample, state4(sre_p), state4(sim_p), buf_p[None],
            state4(sre_s), state4(sim_s), buf_s[None])
```

```python
import functools

import jax
import jax.numpy as jnp
from jax import lax
from jax.experimental import pallas as pl
from jax.experimental.pallas import tpu as pltpu

F32 = jnp.float32
BF16 = jnp.bfloat16

D_MODEL = 1024
D_FF = 4 * D_MODEL
GROUP_SIZE = 16
N_GROUPS = D_MODEL // GROUP_SIZE
P_STATE = 64
S_DIM = N_GROUPS * P_STATE
CONV_W = 3
EPS = 1e-6

SUBLANES = 8
FFN_ROWS = 1024
FFN_COLS = 1024
FFN_COLS_F32 = 512
FFN_ROW_BLOCK = 256
SSM_BLOCKS = 4
SSM_BLOCK_CH = D_MODEL // SSM_BLOCKS
SSM_BLOCK_ST = S_DIM // SSM_BLOCKS
SCAN_LANES = 256
VMEM_LIMIT = 56 * 1024 * 1024

def _rmsnorm(x, g):
    return x * lax.rsqrt(jnp.mean(x * x, axis=-1, keepdims=True) + EPS) * g


def _dot(a, b):
    return jnp.dot(a, b, preferred_element_type=F32)


def _ffn_kernel(x_ref, g_ref, wg_ref, wu_ref, wd_ref, *rest, n_chunks, final_norm,
                x_seq_major, out_seq_major, emit_bf16):
    rest = list(rest)
    gf_ref = rest.pop(0) if final_norm else None
    o_ref = rest.pop(0)
    if emit_bf16:
        wg_bf, wu_bf, wd_bf = rest[:3]
        del rest[:3]
    h_ref, acc_ref = rest[:2]
    xt_ref = rest[2] if x_seq_major else x_ref
    j = pl.program_id(1)

    @pl.when(j == 0)
    def _():
        if x_seq_major:
            n_seq = x_ref.shape[0]
            for t in range(x_ref.shape[1]):
                xt_ref[t * n_seq:(t + 1) * n_seq, :] = x_ref[:, t, :]
        h_ref[...] = _rmsnorm(xt_ref[...], g_ref[...]).astype(BF16)
        acc_ref[...] = jnp.zeros_like(acc_ref)

    if emit_bf16:
        wg_bf[...] = wg_ref[...].astype(BF16)
        wu_bf[...] = wu_ref[...].astype(BF16)
        wd_bf[...] = wd_ref[...].astype(BF16)
        wg_ref, wu_ref, wd_ref = wg_bf, wu_bf, wd_bf

    for r in range(0, FFN_ROWS, FFN_ROW_BLOCK):
        h = h_ref[r:r + FFN_ROW_BLOCK, :]
        gate = _dot(h, wg_ref[...])
        up = _dot(h, wu_ref[...])
        act = (jax.nn.silu(gate) * up).astype(BF16)
        acc_ref[r:r + FFN_ROW_BLOCK, :] += _dot(act, wd_ref[...])

    @pl.when(j == n_chunks - 1)
    def _():
        out = xt_ref[...] + 0.5 * acc_ref[...]
        if final_norm:
            out = _rmsnorm(out, gf_ref[...])
        if out_seq_major:
            n_seq = o_ref.shape[0]
            for t in range(o_ref.shape[1]):
                o_ref[:, t, :] = out[t * n_seq:(t + 1) * n_seq, :]
        else:
            o_ref[...] = out


def _ffn_call(x, x_block, x_map, out_shape, out_block, out_map, n_tiles, g, weights, cols,
              final_g=None):
    n_chunks = D_FF // cols
    emit_bf16 = isinstance(weights[-1], tuple)
    x_seq_major = len(x_block) == 3
    out_seq_major = len(out_block) == 3
    body = functools.partial(_ffn_kernel, n_chunks=n_chunks, final_norm=final_g is not None,
                             x_seq_major=x_seq_major, out_seq_major=out_seq_major, emit_bf16=emit_bf16)
    scratch = [pltpu.VMEM((FFN_ROWS, D_MODEL), BF16), pltpu.VMEM((FFN_ROWS, D_MODEL), F32)]
    if x_seq_major:
        scratch.append(pltpu.VMEM((FFN_ROWS, D_MODEL), F32))

    const2 = lambda i, j: (0, 0)
    gate_bf = pl.BlockSpec((D_MODEL, cols), lambda i, j: (0, j))
    down_bf = pl.BlockSpec((cols, D_MODEL), lambda i, j: (j, 0))
    out_specs = pl.BlockSpec(out_block, lambda i, j: out_map(i))
    out_shapes = jax.ShapeDtypeStruct(out_shape, F32)
    if emit_bf16:
        w_gu, w_down, (layer, half) = weights
        w_specs = [
            pl.BlockSpec((None, None, D_MODEL, cols), lambda i, j: (layer, half, 0, j)),
            pl.BlockSpec((None, None, D_MODEL, cols), lambda i, j: (layer, half, 0, j + n_chunks)),
            pl.BlockSpec((None, None, cols, D_MODEL), lambda i, j: (layer, half, j, 0)),
        ]
        w_args = [w_gu, w_gu, w_down]
        out_specs = (out_specs, gate_bf, gate_bf, down_bf)
        out_shapes = (out_shapes,
                      jax.ShapeDtypeStruct((D_MODEL, D_FF), BF16),
                      jax.ShapeDtypeStruct((D_MODEL, D_FF), BF16),
                      jax.ShapeDtypeStruct((D_FF, D_MODEL), BF16))
    else:
        w_specs = [gate_bf, gate_bf, down_bf]
        w_args = list(weights)
    in_specs = [pl.BlockSpec(x_block, lambda i, j: x_map(i)), pl.BlockSpec((1, D_MODEL), const2)] + w_specs
    args = [x, g.reshape(1, D_MODEL)] + w_args
    if final_g is not None:
        in_specs.append(pl.BlockSpec((1, D_MODEL), const2))
        args.append(final_g.reshape(1, D_MODEL))
    res = pl.pallas_call(
        body,
        grid=(n_tiles, n_chunks),
        in_specs=in_specs,
        out_specs=out_specs,
        out_shape=out_shapes,
        scratch_shapes=scratch,
        compiler_params=pltpu.CompilerParams(
            dimension_semantics=("arbitrary", "arbitrary"), vmem_limit_bytes=VMEM_LIMIT),
        name="ffn",
    )(*args)
    return (res[0], tuple(res[1:])) if emit_bf16 else res


def _ffn_resident_kernel(x_ref, g_ref, wg_ref, wu_ref, wd_ref, *rest, cols, final_norm,
                         x_seq_major, out_seq_major):
    rest = list(rest)
    gf_ref = rest.pop(0) if final_norm else None
    o_ref = rest.pop(0)
    if x_seq_major:
        xbuf, xsem = rest[:2]
        del rest[:2]
    if out_seq_major:
        obuf, osem = rest[:2]
    i = pl.program_id(0)
    n_tiles = pl.num_programs(0)
    slot = i % 2
    tile_steps = FFN_ROWS // SUBLANES
    steps_per_block = FFN_ROW_BLOCK // SUBLANES

    def x_copies(tile, slot_):
        return [pltpu.make_async_copy(x_ref.at[s, pl.ds(tile * tile_steps, tile_steps), :],
                                      xbuf.at[slot_, :, s, :], xsem.at[slot_]) for s in range(SUBLANES)]

    def o_copies(tile, slot_):
        return [pltpu.make_async_copy(obuf.at[slot_, :, s, :],
                                      o_ref.at[s, pl.ds(tile * tile_steps, tile_steps), :],
                                      osem.at[slot_]) for s in range(SUBLANES)]

    if x_seq_major:
        @pl.when(i == 0)
        def _():
            for cp in x_copies(0, 0):
                cp.start()

        @pl.when(i + 1 < n_tiles)
        def _():
            for cp in x_copies(i + 1, 1 - slot):
                cp.start()

        for cp in x_copies(i, slot):
            cp.wait()
    if out_seq_major:
        @pl.when(i >= 2)
        def _():
            for cp in o_copies(i - 2, slot):
                cp.wait()

    for r in range(0, FFN_ROWS, FFN_ROW_BLOCK):
        t0 = r // SUBLANES
        if x_seq_major:
            x = xbuf[slot, t0:t0 + steps_per_block].reshape(FFN_ROW_BLOCK, D_MODEL)
        else:
            x = x_ref[r:r + FFN_ROW_BLOCK, :]
        h = _rmsnorm(x, g_ref[...]).astype(BF16)
        acc = None
        for c in range(0, D_FF, cols):
            gate = _dot(h, wg_ref[:, c:c + cols])
            up = _dot(h, wu_ref[:, c:c + cols])
            act = (jax.nn.silu(gate) * up).astype(BF16)
            part = _dot(act, wd_ref[c:c + cols, :])
            acc = part if acc is None else acc + part
        out = x + 0.5 * acc
        if final_norm:
            out = _rmsnorm(out, gf_ref[...])
        if out_seq_major:
            obuf[slot, t0:t0 + steps_per_block] = out.reshape(steps_per_block, SUBLANES, D_MODEL)
        else:
            o_ref[r:r + FFN_ROW_BLOCK, :] = out

    if out_seq_major:
        for cp in o_copies(i, slot):
            cp.start()

        @pl.when(i == n_tiles - 1)
        def _():
            for cp in o_copies(i, slot):
                cp.wait()

        @pl.when(jnp.logical_and(i == n_tiles - 1, i >= 1))
        def _():
            for cp in o_copies(i - 1, 1 - slot):
                cp.wait()


def _ffn_resident_call(x, out_shape, n_tiles, g, weights, cols, final_g=None):
    x_seq_major = x.ndim == 3
    out_seq_major = len(out_shape) == 3
    body = functools.partial(_ffn_resident_kernel, cols=cols, final_norm=final_g is not None,
                             x_seq_major=x_seq_major, out_seq_major=out_seq_major)
    whole = lambda a: pl.BlockSpec(a.shape, lambda i: (0,) * a.ndim)
    row_tile = pl.BlockSpec((FFN_ROWS, D_MODEL), lambda i: (i, 0))
    in_hbm = pl.BlockSpec(memory_space=pl.ANY)
    g2 = g.reshape(1, D_MODEL)
    in_specs = [in_hbm if x_seq_major else row_tile, whole(g2)] + [whole(w) for w in weights]
    args = [x, g2, *weights]
    if final_g is not None:
        gf2 = final_g.reshape(1, D_MODEL)
        in_specs.append(whole(gf2))
        args.append(gf2)
    reorder_scratch = [pltpu.VMEM((2, FFN_ROWS // SUBLANES, SUBLANES, D_MODEL), F32),
                       pltpu.SemaphoreType.DMA((2,))]
    scratch = (reorder_scratch if x_seq_major else []) + (reorder_scratch if out_seq_major else [])
    return pl.pallas_call(
        body,
        grid=(n_tiles,),
        in_specs=in_specs,
        out_specs=in_hbm if out_seq_major else row_tile,
        out_shape=jax.ShapeDtypeStruct(out_shape, F32),
        scratch_shapes=scratch,
        compiler_params=pltpu.CompilerParams(
            dimension_semantics=("arbitrary",), vmem_limit_bytes=VMEM_LIMIT),
        name="ffn_resident",
    )(*args)


def _ssm_prep_kernel(lam_re_ref, lam_im_ref, ldt_ref, lam_re_rep_ref, lam_im_rep_ref,
                     b_re_ref, b_im_ref, lb_re_ref, lb_im_ref, bb_re_ref, bb_im_ref):
    dt = jnp.exp(ldt_ref[...])

    def discretise(lam_re, lam_im):
        mag = jnp.exp(lam_re * dt)
        lb_re = mag * jnp.cos(lam_im * dt)
        lb_im = mag * jnp.sin(lam_im * dt)
        return lb_re, lb_im

    lb_re, lb_im = discretise(lam_re_ref[...], lam_im_ref[...])
    lb_re_ref[...] = lb_re
    lb_im_ref[...] = lb_im
    lam_re = lam_re_rep_ref[...]
    lam_im = lam_im_rep_ref[...]
    lbr, lbi = discretise(lam_re, lam_im)
    den = lam_re * lam_re + lam_im * lam_im
    nr = lbr - 1.0
    ni = lbi
    f_re = (nr * lam_re + ni * lam_im) / den
    f_im = (ni * lam_re - nr * lam_im) / den
    b_re = b_re_ref[...]
    b_im = b_im_ref[...]
    bb_re_ref[...] = f_re * b_re - f_im * b_im
    bb_im_ref[...] = f_re * b_im + f_im * b_re


def _ssm_prep(lam_re, lam_im, log_dt, b_re, b_im):
    pc = P_STATE * GROUP_SIZE
    rep = lambda a: jnp.repeat(a, GROUP_SIZE, axis=-1)
    shp = lambda n: jax.ShapeDtypeStruct((N_GROUPS, n), F32)
    return pl.pallas_call(
        _ssm_prep_kernel,
        out_shape=(shp(P_STATE), shp(P_STATE), shp(pc), shp(pc)),
        name="ssm_prep",
    )(lam_re, lam_im, log_dt.reshape(N_GROUPS, 1), rep(lam_re), rep(lam_im),
      b_re.reshape(N_GROUPS, pc), b_im.reshape(N_GROUPS, pc))


def _block_diag(w):
    gpb = N_GROUPS // SSM_BLOCKS
    a, b = w.shape[1], w.shape[2]
    tiled = jnp.tile(w.reshape(SSM_BLOCKS, gpb * a, b), (1, 1, gpb))
    row_group = lax.broadcasted_iota(jnp.int32, tiled.shape[1:], 0) // a
    col_group = lax.broadcasted_iota(jnp.int32, tiled.shape[1:], 1) // b
    return jnp.where((row_group == col_group)[None], tiled, 0.0)


def _s5_kernel(x_ref, g_ref, s0_re_ref, s0_im_ref, lb_re_ref, lb_im_ref,
               wb_re_ref, wb_im_ref, wc_ref, d_ref, wglu_ref, o_ref, new_re_ref, new_im_ref, *rest,
               steps, sub_steps, n_seq, carry_over_grid, emit_bf16):
    sub_rows = sub_steps * n_seq
    rest = list(rest)
    if emit_bf16:
        wglu_bf = rest.pop(0)
    st_re, st_im, bu_re, bu_im = rest
    first_step = pl.program_id(0) == 0

    def load_state():
        for grp in range(N_GROUPS):
            st_re[:, grp * P_STATE:(grp + 1) * P_STATE] = s0_re_ref[:, grp, :]
            st_im[:, grp * P_STATE:(grp + 1) * P_STATE] = s0_im_ref[:, grp, :]

    if carry_over_grid:
        pl.when(first_step)(load_state)
    else:
        load_state()
    if emit_bf16:
        @pl.when(first_step)
        def _():
            wglu_bf[...] = wglu_ref[...].astype(BF16)
        wglu_ref = wglu_bf

    for s in range(steps // sub_steps):
        t0 = s * sub_steps
        if len(x_ref.shape) == 3:
            x = x_ref[t0:t0 + sub_steps].reshape(sub_rows, D_MODEL)
        else:
            x = x_ref[t0 * n_seq:t0 * n_seq + sub_rows, :]
        u = _rmsnorm(x, g_ref[...])
        ub = u.astype(BF16)

        def project_in(k):
            uk = ub[:, k * SSM_BLOCK_CH:(k + 1) * SSM_BLOCK_CH]
            bu_re[k % 2] = _dot(uk, wb_re_ref[k])
            bu_im[k % 2] = _dot(uk, wb_im_ref[k])

        def scan_block(k):
            b_re, b_im = bu_re.at[k % 2], bu_im.at[k % 2]
            for c in range(SSM_BLOCK_ST // SCAN_LANES):
                lo = c * SCAN_LANES
                glo = k * SSM_BLOCK_ST + lo
                lr = jnp.broadcast_to(lb_re_ref[:, glo:glo + SCAN_LANES], (SUBLANES, SCAN_LANES))
                li = jnp.broadcast_to(lb_im_ref[:, glo:glo + SCAN_LANES], (SUBLANES, SCAN_LANES))
                for r_state in range(0, n_seq, SUBLANES):
                    hr = st_re[r_state:r_state + SUBLANES, glo:glo + SCAN_LANES]
                    hi = st_im[r_state:r_state + SUBLANES, glo:glo + SCAN_LANES]
                    for t in range(sub_steps):
                        r = t * n_seq + r_state
                        br = b_re[r:r + SUBLANES, lo:lo + SCAN_LANES]
                        bi = b_im[r:r + SUBLANES, lo:lo + SCAN_LANES]
                        hr, hi = lr * hr - li * hi + br, lr * hi + li * hr + bi
                        b_re[r:r + SUBLANES, lo:lo + SCAN_LANES] = hr
                        b_im[r:r + SUBLANES, lo:lo + SCAN_LANES] = hi
                    st_re[r_state:r_state + SUBLANES, glo:glo + SCAN_LANES] = hr
                    st_im[r_state:r_state + SUBLANES, glo:glo + SCAN_LANES] = hi

        ys = []
        project_in(0)
        for k in range(SSM_BLOCKS):
            if k + 1 < SSM_BLOCKS:
                project_in(k + 1)
            scan_block(k)
            hr = bu_re[k % 2].astype(BF16)
            hi = bu_im[k % 2].astype(BF16)
            ys.append(_dot(hr, wc_ref[0, k]) - _dot(hi, wc_ref[1, k]))
        y = jnp.concatenate(ys, axis=-1) + d_ref[...] * u
        z = _dot(jax.nn.gelu(y).astype(BF16), wglu_ref[...])
        m = z[:, :D_MODEL] * jax.nn.sigmoid(z[:, D_MODEL:])
        if len(o_ref.shape) == 3:
            o_ref[t0:t0 + sub_steps] = (x + m).reshape(sub_steps, n_seq, D_MODEL)
        else:
            o_ref[t0 * n_seq:t0 * n_seq + sub_rows, :] = x + m
    def store_state():
        for grp in range(N_GROUPS):
            new_re_ref[:, grp, :] = st_re[:, grp * P_STATE:(grp + 1) * P_STATE]
            new_im_ref[:, grp, :] = st_im[:, grp * P_STATE:(grp + 1) * P_STATE]

    if carry_over_grid:
        pl.when(pl.program_id(0) == pl.num_programs(0) - 1)(store_state)
    else:
        store_state()


def _s5_call(x, x_block, x_map, grid, s0_re, s0_im, state_map, g, lb_re, lb_im,
             wb_re, wb_im, wc, d_skip, wglu, *, steps, sub_steps, n_seq, carry_over_grid):
    sub_rows = sub_steps * n_seq
    emit_bf16 = wglu.dtype == F32
    whole = lambda a: pl.BlockSpec(a.shape, lambda i: (0,) * a.ndim)
    state_spec = pl.BlockSpec((n_seq, N_GROUPS, P_STATE), state_map)
    g2 = g.reshape(1, D_MODEL)
    d2 = d_skip.reshape(1, D_MODEL)
    lbr = lb_re.reshape(1, S_DIM)
    lbi = lb_im.reshape(1, S_DIM)
    body = functools.partial(_s5_kernel, steps=steps, sub_steps=sub_steps, n_seq=n_seq,
                             carry_over_grid=carry_over_grid, emit_bf16=emit_bf16)
    out_specs = [pl.BlockSpec(x_block, x_map), state_spec, state_spec]
    out_shape = [jax.ShapeDtypeStruct(x.shape, F32),
                 jax.ShapeDtypeStruct(s0_re.shape, F32), jax.ShapeDtypeStruct(s0_re.shape, F32)]
    if emit_bf16:
        out_specs.append(whole(wglu))
        out_shape.append(jax.ShapeDtypeStruct(wglu.shape, BF16))
    return pl.pallas_call(
        body,
        grid=grid,
        in_specs=[pl.BlockSpec(x_block, x_map), whole(g2), state_spec, state_spec,
                  whole(lbr), whole(lbi), whole(wb_re), whole(wb_im), whole(wc),
                  whole(d2), whole(wglu)],
        out_specs=tuple(out_specs),
        out_shape=tuple(out_shape),
        scratch_shapes=[pltpu.VMEM((n_seq, S_DIM), F32), pltpu.VMEM((n_seq, S_DIM), F32),
                        pltpu.VMEM((2, sub_rows, SSM_BLOCK_ST), F32),
                        pltpu.VMEM((2, sub_rows, SSM_BLOCK_ST), F32)],
        compiler_params=pltpu.CompilerParams(
            dimension_semantics=("arbitrary",), vmem_limit_bytes=VMEM_LIMIT),
        name="s5_mixer",
    )(x, g2, s0_re, s0_im, lbr, lbi, wb_re, wb_im, wc, d2, wglu)


def _conv_kernel(x_ref, g_ref, buf0_ref, win_ref, cw_ref, wout_ref, o_ref, newbuf_ref, *rest,
                 steps, sub_steps, n_seq, carry_over_grid, emit_bf16):
    rows = steps * n_seq
    halo = (CONV_W - 1) * n_seq
    zp = rest[-1]
    if emit_bf16:
        win_bf, wout_bf = rest[:2]

        @pl.when(pl.program_id(0) == 0)
        def _():
            win_bf[...] = win_ref[...].astype(BF16)
            wout_bf[...] = wout_ref[...].astype(BF16)
        win_ref, wout_ref = win_bf, wout_bf

    def load_buf0():
        for k in range(CONV_W - 1):
            zp[k * n_seq:(k + 1) * n_seq, :] = buf0_ref[:, k, :]

    if carry_over_grid:
        pl.when(pl.program_id(0) == 0)(load_buf0)
    else:
        load_buf0()

    sub_rows = sub_steps * n_seq
    for s in range(steps // sub_steps):
        t0 = s * sub_steps
        r0 = t0 * n_seq
        if len(x_ref.shape) == 3:
            x = x_ref[t0:t0 + sub_steps].reshape(sub_rows, D_MODEL)
        else:
            x = x_ref[r0:r0 + sub_rows, :]
        h = _rmsnorm(x, g_ref[...]).astype(BF16)
        p = _dot(h, win_ref[...])
        gb = p[:, :D_MODEL]
        zp[halo + r0:halo + r0 + sub_rows, :] = p[:, D_MODEL:2 * D_MODEL] * p[:, 2 * D_MODEL:]
        conv = cw_ref[0:1, :] * zp[r0:r0 + sub_rows, :]
        for k in range(1, CONV_W):
            conv = conv + cw_ref[k:k + 1, :] * zp[r0 + k * n_seq:r0 + k * n_seq + sub_rows, :]
        m = _dot((gb * conv).astype(BF16), wout_ref[...])
        if len(o_ref.shape) == 3:
            o_ref[t0:t0 + sub_steps] = (x + m).reshape(sub_steps, n_seq, D_MODEL)
        else:
            o_ref[r0:r0 + sub_rows, :] = x + m
    tail = zp[rows:rows + halo, :]
    zp[0:halo, :] = tail
    for k in range(CONV_W - 1):
        newbuf_ref[:, k, :] = tail[k * n_seq:(k + 1) * n_seq, :]


def _conv_call(x, x_block, x_map, grid, buf0, buf_map, g, w_in, cw, w_out,
               *, steps, sub_steps, n_seq, carry_over_grid):
    rows = steps * n_seq
    halo = (CONV_W - 1) * n_seq
    emit_bf16 = w_in.dtype == F32
    whole = lambda a: pl.BlockSpec(a.shape, lambda i: (0,) * a.ndim)
    g2 = g.reshape(1, D_MODEL)
    buf_block = (n_seq, CONV_W - 1, D_MODEL)
    body = functools.partial(_conv_kernel, steps=steps, sub_steps=sub_steps, n_seq=n_seq,
                             carry_over_grid=carry_over_grid, emit_bf16=emit_bf16)
    out_specs = [pl.BlockSpec(x_block, x_map), pl.BlockSpec(buf_block, buf_map)]
    out_shape = [jax.ShapeDtypeStruct(x.shape, F32), jax.ShapeDtypeStruct(buf0.shape, F32)]
    if emit_bf16:
        out_specs += [whole(w_in), whole(w_out)]
        out_shape += [jax.ShapeDtypeStruct(w_in.shape, BF16), jax.ShapeDtypeStruct(w_out.shape, BF16)]
    return pl.pallas_call(
        body,
        grid=grid,
        in_specs=[pl.BlockSpec(x_block, x_map), whole(g2), pl.BlockSpec(buf_block, buf_map),
                  whole(w_in), whole(cw), whole(w_out)],
        out_specs=tuple(out_specs),
        out_shape=tuple(out_shape),
        scratch_shapes=[pltpu.VMEM((halo + rows, D_MODEL), F32)],
        compiler_params=pltpu.CompilerParams(
            dimension_semantics=("arbitrary",), vmem_limit_bytes=VMEM_LIMIT),
        name="conv_mixer",
    )(x, g2, buf0, w_in, cw, w_out)


S5_PROMPT_STEPS = 64
S5_PROMPT_SUB_STEPS = 32
CONV_PROMPT_STEPS = 128
CONV_PROMPT_SUB_STEPS = 32
SAMPLE_SEQ_BLOCK = 32


def kernel(x_prompt, x_sample, state_ssm_re, state_ssm_im, cache_conv, norm_g, final_norm_g, ffn_w_gate_up, ffn_w_down, ssm_lam_re, ssm_lam_im, ssm_log_dt, ssm_b_re, ssm_b_im, ssm_c_re, ssm_c_im, ssm_d, ssm_w_glu, conv_w_in, conv_w, conv_w_out):
    nb_p, len_p, _ = x_prompt.shape
    nb_s, len_s, _ = x_sample.shape
    rows_p = nb_p * len_p
    rows_s = nb_s * len_s
    assert nb_p == SUBLANES and len_p % FFN_ROWS == 0 and rows_s == FFN_ROWS
    assert nb_s % SAMPLE_SEQ_BLOCK == 0

    lb_re, lb_im, bb_re, bb_im = _ssm_prep(ssm_lam_re[0], ssm_lam_im[0], ssm_log_dt[0],
                                           ssm_b_re[0], ssm_b_im[0])
    to_gcp = lambda a: jnp.swapaxes(a.reshape(N_GROUPS, P_STATE, GROUP_SIZE), 1, 2)
    wb_re = _block_diag(to_gcp(bb_re)).astype(BF16)
    wb_im = _block_diag(to_gcp(bb_im)).astype(BF16)
    wc = jnp.stack([_block_diag(jnp.swapaxes(ssm_c_re[0], 1, 2)),
                    _block_diag(jnp.swapaxes(ssm_c_im[0], 1, 2))]).astype(BF16)

    t_tiles = len_p // FFN_ROWS
    row_tiles_p = rows_p // FFN_ROWS
    flat_block = (FFN_ROWS, D_MODEL)
    flat_map = lambda i: (i, 0)

    whole_s = (x_sample.shape, lambda i: (0, 0, 0))
    flat_s = (flat_block, flat_map)

    def ffn_sample(xs, layer, half, x_spec=flat_s, out_spec=flat_s, final_g=None):
        out_shape = x_sample.shape if out_spec is whole_s else (rows_s, D_MODEL)
        return _ffn_call(xs, *x_spec, out_shape, *out_spec, 1,
                         norm_g[layer, 2 * half], (ffn_w_gate_up, ffn_w_down, (layer, half)),
                         FFN_COLS_F32, final_g=final_g)

    xs, w_bf_00 = ffn_sample(x_sample, 0, 0, x_spec=whole_s)
    seq_block = (len_s, SAMPLE_SEQ_BLOCK, D_MODEL)
    seq_map = lambda i: (0, i, 0)
    seq_grid = (nb_s // SAMPLE_SEQ_BLOCK,)
    xs, sre_s, sim_s, wglu = _s5_call(
        xs.reshape(len_s, nb_s, D_MODEL), seq_block, seq_map, seq_grid,
        state_ssm_re[0], state_ssm_im[0], lambda i: (i, 0, 0),
        norm_g[0, 1], lb_re, lb_im, wb_re, wb_im, wc, ssm_d[0], ssm_w_glu[0],
        steps=len_s, sub_steps=len_s, n_seq=SAMPLE_SEQ_BLOCK, carry_over_grid=False)
    xs = xs.reshape(rows_s, D_MODEL)
    xs, w_bf_01 = ffn_sample(xs, 0, 1)
    xs, w_bf_10 = ffn_sample(xs, 1, 0)
    xs, buf_s, w_in, w_out = _conv_call(
        xs.reshape(len_s, nb_s, D_MODEL), seq_block, seq_map, seq_grid,
        cache_conv[0], lambda i: (i, 0, 0), norm_g[1, 1], conv_w_in[0], conv_w[0], conv_w_out[0],
        steps=len_s, sub_steps=len_s, n_seq=SAMPLE_SEQ_BLOCK, carry_over_grid=False)
    xs = xs.reshape(rows_s, D_MODEL)
    y_sample, w_bf_11 = ffn_sample(xs, 1, 1, out_spec=whole_s, final_g=final_norm_g)

    def ffn(x, out_shape, g, w_bf, final_g=None):
        return _ffn_resident_call(x, out_shape, row_tiles_p, g, w_bf, FFN_COLS, final_g=final_g)

    xp = ffn(x_prompt, (rows_p, D_MODEL), norm_g[0, 0], w_bf_00)
    zero_state = jnp.zeros((nb_p, N_GROUPS, P_STATE), F32)
    xp, sre_p, sim_p = _s5_call(
        xp, (S5_PROMPT_STEPS * nb_p, D_MODEL), flat_map, (len_p // S5_PROMPT_STEPS,),
        zero_state, zero_state, lambda i: (0, 0, 0), norm_g[0, 1], lb_re, lb_im,
        wb_re, wb_im, wc, ssm_d[0], wglu,
        steps=S5_PROMPT_STEPS, sub_steps=S5_PROMPT_SUB_STEPS, n_seq=nb_p, carry_over_grid=True)
    xp = ffn(xp, (rows_p, D_MODEL), norm_g[0, 2], w_bf_01)
    xp = ffn(xp, (rows_p, D_MODEL), norm_g[1, 0], w_bf_10)
    xp, buf_p = _conv_call(
        xp, (CONV_PROMPT_STEPS * nb_p, D_MODEL), flat_map, (len_p // CONV_PROMPT_STEPS,),
        jnp.zeros((nb_p, CONV_W - 1, D_MODEL), F32), lambda i: (0, 0, 0),
        norm_g[1, 1], w_in, conv_w[0], w_out,
        steps=CONV_PROMPT_STEPS, sub_steps=CONV_PROMPT_SUB_STEPS, n_seq=nb_p, carry_over_grid=True)
    y_prompt = ffn(xp, x_prompt.shape, norm_g[1, 2], w_bf_11, final_g=final_norm_g)

    return (y_prompt, y_sample, sre_p[None], sim_p[None], buf_p[None],
            sre_s[None], sim_s[None], buf_s[None])
```

```python
import functools

import jax
import jax.numpy as jnp
from jax import lax
from jax.experimental import pallas as pl
from jax.experimental.pallas import tpu as pltpu

F32 = jnp.float32
BF16 = jnp.bfloat16

D_MODEL = 1024
D_FF = 4 * D_MODEL
GROUP_SIZE = 16
N_GROUPS = D_MODEL // GROUP_SIZE
P_STATE = 64
S_DIM = N_GROUPS * P_STATE
CONV_W = 3
EPS = 1e-6

SUBLANES = 8
FFN_ROWS = 1024
FFN_COLS = 1024
FFN_COLS_F32 = 512
FFN_ROW_BLOCK = 256
SSM_BLOCKS = 4
SSM_BLOCK_CH = D_MODEL // SSM_BLOCKS
SSM_BLOCK_ST = S_DIM // SSM_BLOCKS
SCAN_LANES = 256
VMEM_LIMIT = 56 * 1024 * 1024


def _rmsnorm(x, g):
    return x * lax.rsqrt(jnp.mean(x * x, axis=-1, keepdims=True) + EPS) * g


def _dot(a, b):
    return jnp.dot(a, b, preferred_element_type=F32)


def _ffn_kernel(x_ref, g_ref, wg_ref, wu_ref, wd_ref, *rest, n_chunks, final_norm,
                x_seq_major, out_seq_major, emit_bf16):
    rest = list(rest)
    gf_ref = rest.pop(0) if final_norm else None
    o_ref = rest.pop(0)
    if emit_bf16:
        wg_bf, wu_bf, wd_bf = rest[:3]
        del rest[:3]
    h_ref, acc_ref = rest[:2]
    xt_ref = rest[2] if x_seq_major else x_ref
    j = pl.program_id(1)

    @pl.when(j == 0)
    def _():
        if x_seq_major:
            n_seq = x_ref.shape[0]
            for t in range(x_ref.shape[1]):
                xt_ref[t * n_seq:(t + 1) * n_seq, :] = x_ref[:, t, :]
        h_ref[...] = _rmsnorm(xt_ref[...], g_ref[...]).astype(BF16)
        acc_ref[...] = jnp.zeros_like(acc_ref)

    if emit_bf16:
        wg_bf[...] = wg_ref[...].astype(BF16)
        wu_bf[...] = wu_ref[...].astype(BF16)
        wd_bf[...] = wd_ref[...].astype(BF16)
        wg_ref, wu_ref, wd_ref = wg_bf, wu_bf, wd_bf

    for r in range(0, FFN_ROWS, FFN_ROW_BLOCK):
        h = h_ref[r:r + FFN_ROW_BLOCK, :]
        gate = _dot(h, wg_ref[...])
        up = _dot(h, wu_ref[...])
        act = (jax.nn.silu(gate) * up).astype(BF16)
        acc_ref[r:r + FFN_ROW_BLOCK, :] += _dot(act, wd_ref[...])

    @pl.when(j == n_chunks - 1)
    def _():
        out = xt_ref[...] + 0.5 * acc_ref[...]
        if final_norm:
            out = _rmsnorm(out, gf_ref[...])
        if out_seq_major:
            n_seq = o_ref.shape[0]
            for t in range(o_ref.shape[1]):
                o_ref[:, t, :] = out[t * n_seq:(t + 1) * n_seq, :]
        else:
            o_ref[...] = out


def _ffn_call(x, x_block, x_map, out_shape, out_block, out_map, n_tiles, g, weights, cols,
              final_g=None):
    n_chunks = D_FF // cols
    emit_bf16 = isinstance(weights[-1], tuple)
    x_seq_major = len(x_block) == 3
    out_seq_major = len(out_block) == 3
    body = functools.partial(_ffn_kernel, n_chunks=n_chunks, final_norm=final_g is not None,
                             x_seq_major=x_seq_major, out_seq_major=out_seq_major, emit_bf16=emit_bf16)
    scratch = [pltpu.VMEM((FFN_ROWS, D_MODEL), BF16), pltpu.VMEM((FFN_ROWS, D_MODEL), F32)]
    if x_seq_major:
        scratch.append(pltpu.VMEM((FFN_ROWS, D_MODEL), F32))

    const2 = lambda i, j: (0, 0)
    gate_bf = pl.BlockSpec((D_MODEL, cols), lambda i, j: (0, j))
    down_bf = pl.BlockSpec((cols, D_MODEL), lambda i, j: (j, 0))
    out_specs = pl.BlockSpec(out_block, lambda i, j: out_map(i))
    out_shapes = jax.ShapeDtypeStruct(out_shape, F32)
    if emit_bf16:
        w_gu, w_down, (layer, half) = weights
        w_specs = [
            pl.BlockSpec((None, None, D_MODEL, cols), lambda i, j: (layer, half, 0, j)),
            pl.BlockSpec((None, None, D_MODEL, cols), lambda i, j: (layer, half, 0, j + n_chunks)),
            pl.BlockSpec((None, None, cols, D_MODEL), lambda i, j: (layer, half, j, 0)),
        ]
        w_args = [w_gu, w_gu, w_down]
        out_specs = (out_specs, gate_bf, gate_bf, down_bf)
        out_shapes = (out_shapes,
                      jax.ShapeDtypeStruct((D_MODEL, D_FF), BF16),
                      jax.ShapeDtypeStruct((D_MODEL, D_FF), BF16),
                      jax.ShapeDtypeStruct((D_FF, D_MODEL), BF16))
    else:
        w_specs = [gate_bf, gate_bf, down_bf]
        w_args = list(weights)
    in_specs = [pl.BlockSpec(x_block, lambda i, j: x_map(i)), pl.BlockSpec((1, D_MODEL), const2)] + w_specs
    args = [x, g.reshape(1, D_MODEL)] + w_args
    if final_g is not None:
        in_specs.append(pl.BlockSpec((1, D_MODEL), const2))
        args.append(final_g.reshape(1, D_MODEL))
    res = pl.pallas_call(
        body,
        grid=(n_tiles, n_chunks),
        in_specs=in_specs,
        out_specs=out_specs,
        out_shape=out_shapes,
        scratch_shapes=scratch,
        compiler_params=pltpu.CompilerParams(
            dimension_semantics=("arbitrary", "arbitrary"), vmem_limit_bytes=VMEM_LIMIT),
        name="ffn",
    )(*args)
    return (res[0], tuple(res[1:])) if emit_bf16 else res


def _ffn_resident_kernel(x_ref, g_ref, wg_ref, wu_ref, wd_ref, *rest, cols, final_norm,
                         x_seq_major, out_seq_major):
    rest = list(rest)
    gf_ref = rest.pop(0) if final_norm else None
    o_ref = rest.pop(0)
    if x_seq_major:
        xbuf, xsem = rest[:2]
        del rest[:2]
    if out_seq_major:
        obuf, osem = rest[:2]
    i = pl.program_id(0)
    n_tiles = pl.num_programs(0)
    slot = i % 2
    tile_steps = FFN_ROWS // SUBLANES
    steps_per_block = FFN_ROW_BLOCK // SUBLANES

    def x_copies(tile, slot_):
        return [pltpu.make_async_copy(x_ref.at[s, pl.ds(tile * tile_steps, tile_steps), :],
                                      xbuf.at[slot_, :, s, :], xsem.at[slot_]) for s in range(SUBLANES)]

    def o_copies(tile, slot_):
        return [pltpu.make_async_copy(obuf.at[slot_, :, s, :],
                                      o_ref.at[s, pl.ds(tile * tile_steps, tile_steps), :],
                                      osem.at[slot_]) for s in range(SUBLANES)]

    if x_seq_major:
        @pl.when(i == 0)
        def _():
            for cp in x_copies(0, 0):
                cp.start()

        @pl.when(i + 1 < n_tiles)
        def _():
            for cp in x_copies(i + 1, 1 - slot):
                cp.start()

        for cp in x_copies(i, slot):
            cp.wait()
    if out_seq_major:
        @pl.when(i >= 2)
        def _():
            for cp in o_copies(i - 2, slot):
                cp.wait()

    for r in range(0, FFN_ROWS, FFN_ROW_BLOCK):
        t0 = r // SUBLANES
        if x_seq_major:
            x = xbuf[slot, t0:t0 + steps_per_block].reshape(FFN_ROW_BLOCK, D_MODEL)
        else:
            x = x_ref[r:r + FFN_ROW_BLOCK, :]
        h = _rmsnorm(x, g_ref[...]).astype(BF16)
        acc = None
        for c in range(0, D_FF, cols):
            gate = _dot(h, wg_ref[:, c:c + cols])
            up = _dot(h, wu_ref[:, c:c + cols])
            act = (jax.nn.silu(gate) * up).astype(BF16)
            part = _dot(act, wd_ref[c:c + cols, :])
            acc = part if acc is None else acc + part
        out = x + 0.5 * acc
        if final_norm:
            out = _rmsnorm(out, gf_ref[...])
        if out_seq_major:
            obuf[slot, t0:t0 + steps_per_block] = out.reshape(steps_per_block, SUBLANES, D_MODEL)
        else:
            o_ref[r:r + FFN_ROW_BLOCK, :] = out

    if out_seq_major:
        for cp in o_copies(i, slot):
            cp.start()

        @pl.when(i == n_tiles - 1)
        def _():
            for cp in o_copies(i, slot):
                cp.wait()

        @pl.when(jnp.logical_and(i == n_tiles - 1, i >= 1))
        def _():
            for cp in o_copies(i - 1, 1 - slot):
                cp.wait()


def _ffn_resident_call(x, out_shape, n_tiles, g, weights, cols, final_g=None):
    x_seq_major = x.ndim == 3
    out_seq_major = len(out_shape) == 3
    body = functools.partial(_ffn_resident_kernel, cols=cols, final_norm=final_g is not None,
                             x_seq_major=x_seq_major, out_seq_major=out_seq_major)
    whole = lambda a: pl.BlockSpec(a.shape, lambda i: (0,) * a.ndim)
    row_tile = pl.BlockSpec((FFN_ROWS, D_MODEL), lambda i: (i, 0))
    in_hbm = pl.BlockSpec(memory_space=pl.ANY)
    g2 = g.reshape(1, D_MODEL)
    in_specs = [in_hbm if x_seq_major else row_tile, whole(g2)] + [whole(w) for w in weights]
    args = [x, g2, *weights]
    if final_g is not None:
        gf2 = final_g.reshape(1, D_MODEL)
        in_specs.append(whole(gf2))
        args.append(gf2)
    reorder_scratch = [pltpu.VMEM((2, FFN_ROWS // SUBLANES, SUBLANES, D_MODEL), F32),
                       pltpu.SemaphoreType.DMA((2,))]
    scratch = (reorder_scratch if x_seq_major else []) + (reorder_scratch if out_seq_major else [])
    return pl.pallas_call(
        body,
        grid=(n_tiles,),
        in_specs=in_specs,
        out_specs=in_hbm if out_seq_major else row_tile,
        out_shape=jax.ShapeDtypeStruct(out_shape, F32),
        scratch_shapes=scratch,
        compiler_params=pltpu.CompilerParams(
            dimension_semantics=("arbitrary",), vmem_limit_bytes=VMEM_LIMIT),
        name="ffn_resident",
    )(*args)


def _ssm_prep_kernel(lam_re_ref, lam_im_ref, ldt_ref, lam_re_rep_ref, lam_im_rep_ref,
                     b_re_ref, b_im_ref, lb_re_ref, lb_im_ref, bb_re_ref, bb_im_ref):
    dt = jnp.exp(ldt_ref[...])

    def discretise(lam_re, lam_im):
        mag = jnp.exp(lam_re * dt)
        lb_re = mag * jnp.cos(lam_im * dt)
        lb_im = mag * jnp.sin(lam_im * dt)
        return lb_re, lb_im

    lb_re, lb_im = discretise(lam_re_ref[...], lam_im_ref[...])
    lb_re_ref[...] = lb_re
    lb_im_ref[...] = lb_im
    lam_re = lam_re_rep_ref[...]
    lam_im = lam_im_rep_ref[...]
    lbr, lbi = discretise(lam_re, lam_im)
    den = lam_re * lam_re + lam_im * lam_im
    nr = lbr - 1.0
    ni = lbi
    f_re = (nr * lam_re + ni * lam_im) / den
    f_im = (ni * lam_re - nr * lam_im) / den
    b_re = b_re_ref[...]
    b_im = b_im_ref[...]
    bb_re_ref[...] = f_re * b_re - f_im * b_im
    bb_im_ref[...] = f_re * b_im + f_im * b_re


def _ssm_prep(lam_re, lam_im, log_dt, b_re, b_im):
    pc = P_STATE * GROUP_SIZE
    rep = lambda a: jnp.repeat(a, GROUP_SIZE, axis=-1)
    shp = lambda n: jax.ShapeDtypeStruct((N_GROUPS, n), F32)
    return pl.pallas_call(
        _ssm_prep_kernel,
        out_shape=(shp(P_STATE), shp(P_STATE), shp(pc), shp(pc)),
        name="ssm_prep",
    )(lam_re, lam_im, log_dt.reshape(N_GROUPS, 1), rep(lam_re), rep(lam_im),
      b_re.reshape(N_GROUPS, pc), b_im.reshape(N_GROUPS, pc))


def _block_diag(w):
    gpb = N_GROUPS // SSM_BLOCKS
    a, b = w.shape[1], w.shape[2]
    tiled = jnp.tile(w.reshape(SSM_BLOCKS, gpb * a, b), (1, 1, gpb))
    row_group = lax.broadcasted_iota(jnp.int32, tiled.shape[1:], 0) // a
    col_group = lax.broadcasted_iota(jnp.int32, tiled.shape[1:], 1) // b
    return jnp.where((row_group == col_group)[None], tiled, 0.0)


def _s5_kernel(x_ref, g_ref, s0_re_ref, s0_im_ref, lb_re_ref, lb_im_ref,
               wb_re_ref, wb_im_ref, wc_re_ref, wc_im_ref, d_ref, wglu_ref,
               o_ref, new_re_ref, new_im_ref,
               st_re, st_im, bu_re, bu_im, *, steps, sub_steps, n_seq, carry_over_grid):
    sub_rows = sub_steps * n_seq
    if carry_over_grid:
        @pl.when(pl.program_id(0) == 0)
        def _():
            st_re[...] = s0_re_ref[...]
            st_im[...] = s0_im_ref[...]
    else:
        st_re[...] = s0_re_ref[...]
        st_im[...] = s0_im_ref[...]

    for s in range(steps // sub_steps):
        t0 = s * sub_steps
        if len(x_ref.shape) == 3:
            x = x_ref[t0:t0 + sub_steps].reshape(sub_rows, D_MODEL)
        else:
            x = x_ref[t0 * n_seq:t0 * n_seq + sub_rows, :]
        u = _rmsnorm(x, g_ref[...])
        ub = u.astype(BF16)

        def project_in(k):
            uk = ub[:, k * SSM_BLOCK_CH:(k + 1) * SSM_BLOCK_CH]
            bu_re[k % 2] = _dot(uk, wb_re_ref[k])
            bu_im[k % 2] = _dot(uk, wb_im_ref[k])

        def scan_block(k):
            b_re, b_im = bu_re.at[k % 2], bu_im.at[k % 2]
            for c in range(SSM_BLOCK_ST // SCAN_LANES):
                lo = c * SCAN_LANES
                glo = k * SSM_BLOCK_ST + lo
                lr = jnp.broadcast_to(lb_re_ref[:, glo:glo + SCAN_LANES], (SUBLANES, SCAN_LANES))
                li = jnp.broadcast_to(lb_im_ref[:, glo:glo + SCAN_LANES], (SUBLANES, SCAN_LANES))
                for r_state in range(0, n_seq, SUBLANES):
                    hr = st_re[r_state:r_state + SUBLANES, glo:glo + SCAN_LANES]
                    hi = st_im[r_state:r_state + SUBLANES, glo:glo + SCAN_LANES]
                    for t in range(sub_steps):
                        r = t * n_seq + r_state
                        br = b_re[r:r + SUBLANES, lo:lo + SCAN_LANES]
                        bi = b_im[r:r + SUBLANES, lo:lo + SCAN_LANES]
                        hr, hi = lr * hr - li * hi + br, lr * hi + li * hr + bi
                        b_re[r:r + SUBLANES, lo:lo + SCAN_LANES] = hr
                        b_im[r:r + SUBLANES, lo:lo + SCAN_LANES] = hi
                    st_re[r_state:r_state + SUBLANES, glo:glo + SCAN_LANES] = hr
                    st_im[r_state:r_state + SUBLANES, glo:glo + SCAN_LANES] = hi

        ys = []
        project_in(0)
        for k in range(SSM_BLOCKS):
            if k + 1 < SSM_BLOCKS:
                project_in(k + 1)
            scan_block(k)
            hr = bu_re[k % 2].astype(BF16)
            hi = bu_im[k % 2].astype(BF16)
            ys.append(_dot(hr, wc_re_ref[k]) - _dot(hi, wc_im_ref[k]))
        y = jnp.concatenate(ys, axis=-1) + d_ref[...] * u
        z = _dot(jax.nn.gelu(y).astype(BF16), wglu_ref[...])
        m = z[:, :D_MODEL] * jax.nn.sigmoid(z[:, D_MODEL:])
        if len(o_ref.shape) == 3:
            o_ref[t0:t0 + sub_steps] = (x + m).reshape(sub_steps, n_seq, D_MODEL)
        else:
            o_ref[t0 * n_seq:t0 * n_seq + sub_rows, :] = x + m
    new_re_ref[...] = st_re[...]
    new_im_ref[...] = st_im[...]


def _s5_call(x, x_block, x_map, grid, s0_re, s0_im, state_map, g, lb_re, lb_im,
             wb_re, wb_im, wc_re, wc_im, d_skip, wglu, *, steps, sub_steps, n_seq, carry_over_grid):
    sub_rows = sub_steps * n_seq
    n_state = s0_re.shape[0]
    whole = lambda a: pl.BlockSpec(a.shape, lambda i: (0,) * a.ndim)
    state_spec = pl.BlockSpec((n_seq, S_DIM), state_map)
    g2 = g.reshape(1, D_MODEL)
    d2 = d_skip.reshape(1, D_MODEL)
    lbr = lb_re.reshape(1, S_DIM)
    lbi = lb_im.reshape(1, S_DIM)
    body = functools.partial(_s5_kernel, steps=steps, sub_steps=sub_steps, n_seq=n_seq,
                             carry_over_grid=carry_over_grid)
    return pl.pallas_call(
        body,
        grid=grid,
        in_specs=[pl.BlockSpec(x_block, x_map), whole(g2), state_spec, state_spec,
                  whole(lbr), whole(lbi), whole(wb_re), whole(wb_im), whole(wc_re), whole(wc_im),
                  whole(d2), whole(wglu)],
        out_specs=(pl.BlockSpec(x_block, x_map), state_spec, state_spec),
        out_shape=(jax.ShapeDtypeStruct(x.shape, F32),
                   jax.ShapeDtypeStruct((n_state, S_DIM), F32),
                   jax.ShapeDtypeStruct((n_state, S_DIM), F32)),
        scratch_shapes=[pltpu.VMEM((n_seq, S_DIM), F32), pltpu.VMEM((n_seq, S_DIM), F32),
                        pltpu.VMEM((2, sub_rows, SSM_BLOCK_ST), F32),
                        pltpu.VMEM((2, sub_rows, SSM_BLOCK_ST), F32)],
        compiler_params=pltpu.CompilerParams(
            dimension_semantics=("arbitrary",), vmem_limit_bytes=VMEM_LIMIT),
        name="s5_mixer",
    )(x, g2, s0_re, s0_im, lbr, lbi, wb_re, wb_im, wc_re, wc_im, d2, wglu)


def _conv_kernel(x_ref, g_ref, buf0_ref, win_ref, cw_ref, wout_ref, o_ref, newbuf_ref, zp,
                 *, steps, sub_steps, n_seq, carry_over_grid):
    rows = steps * n_seq
    halo = (CONV_W - 1) * n_seq

    def load_buf0():
        for k in range(CONV_W - 1):
            zp[k * n_seq:(k + 1) * n_seq, :] = buf0_ref[:, k, :]

    if carry_over_grid:
        pl.when(pl.program_id(0) == 0)(load_buf0)
    else:
        load_buf0()

    sub_rows = sub_steps * n_seq
    for s in range(steps // sub_steps):
        t0 = s * sub_steps
        r0 = t0 * n_seq
        if len(x_ref.shape) == 3:
            x = x_ref[t0:t0 + sub_steps].reshape(sub_rows, D_MODEL)
        else:
            x = x_ref[r0:r0 + sub_rows, :]
        h = _rmsnorm(x, g_ref[...]).astype(BF16)
        p = _dot(h, win_ref[...])
        gb = p[:, :D_MODEL]
        zp[halo + r0:halo + r0 + sub_rows, :] = p[:, D_MODEL:2 * D_MODEL] * p[:, 2 * D_MODEL:]
        conv = cw_ref[0:1, :] * zp[r0:r0 + sub_rows, :]
        for k in range(1, CONV_W):
            conv = conv + cw_ref[k:k + 1, :] * zp[r0 + k * n_seq:r0 + k * n_seq + sub_rows, :]
        m = _dot((gb * conv).astype(BF16), wout_ref[...])
        if len(o_ref.shape) == 3:
            o_ref[t0:t0 + sub_steps] = (x + m).reshape(sub_steps, n_seq, D_MODEL)
        else:
            o_ref[r0:r0 + sub_rows, :] = x + m
    tail = zp[rows:rows + halo, :]
    zp[0:halo, :] = tail
    for k in range(CONV_W - 1):
        newbuf_ref[:, k, :] = tail[k * n_seq:(k + 1) * n_seq, :]


def _conv_call(x, x_block, x_map, grid, buf0, buf_map, g, w_in, cw, w_out,
               *, steps, sub_steps, n_seq, carry_over_grid):
    rows = steps * n_seq
    halo = (CONV_W - 1) * n_seq
    whole = lambda a: pl.BlockSpec(a.shape, lambda i: (0,) * a.ndim)
    g2 = g.reshape(1, D_MODEL)
    buf_block = (n_seq, CONV_W - 1, D_MODEL)
    body = functools.partial(_conv_kernel, steps=steps, sub_steps=sub_steps, n_seq=n_seq,
                             carry_over_grid=carry_over_grid)
    return pl.pallas_call(
        body,
        grid=grid,
        in_specs=[pl.BlockSpec(x_block, x_map), whole(g2), pl.BlockSpec(buf_block, buf_map),
                  whole(w_in), whole(cw), whole(w_out)],
        out_specs=(pl.BlockSpec(x_block, x_map), pl.BlockSpec(buf_block, buf_map)),
        out_shape=(jax.ShapeDtypeStruct(x.shape, F32), jax.ShapeDtypeStruct(buf0.shape, F32)),
        scratch_shapes=[pltpu.VMEM((halo + rows, D_MODEL), F32)],
        compiler_params=pltpu.CompilerParams(
            dimension_semantics=("arbitrary",), vmem_limit_bytes=VMEM_LIMIT),
        name="conv_mixer",
    )(x, g2, buf0, w_in, cw, w_out)


S5_PROMPT_STEPS = 128
S5_PROMPT_SUB_STEPS = 32
CONV_PROMPT_STEPS = 128
CONV_PROMPT_SUB_STEPS = 32
SAMPLE_SEQ_BLOCK = 32


def kernel(x_prompt, x_sample, state_ssm_re, state_ssm_im, cache_conv, norm_g, final_norm_g, ffn_w_gate_up, ffn_w_down, ssm_lam_re, ssm_lam_im, ssm_log_dt, ssm_b_re, ssm_b_im, ssm_c_re, ssm_c_im, ssm_d, ssm_w_glu, conv_w_in, conv_w, conv_w_out):
    nb_p, len_p, _ = x_prompt.shape
    nb_s, len_s, _ = x_sample.shape
    rows_p = nb_p * len_p
    rows_s = nb_s * len_s
    assert nb_p == SUBLANES and len_p % FFN_ROWS == 0 and rows_s == FFN_ROWS
    assert nb_s % SAMPLE_SEQ_BLOCK == 0

    lb_re, lb_im, bb_re, bb_im = _ssm_prep(ssm_lam_re[0], ssm_lam_im[0], ssm_log_dt[0],
                                           ssm_b_re[0], ssm_b_im[0])
    to_gcp = lambda a: jnp.swapaxes(a.reshape(N_GROUPS, P_STATE, GROUP_SIZE), 1, 2)
    wb_re = _block_diag(to_gcp(bb_re)).astype(BF16)
    wb_im = _block_diag(to_gcp(bb_im)).astype(BF16)
    wc_re = _block_diag(jnp.swapaxes(ssm_c_re[0], 1, 2)).astype(BF16)
    wc_im = _block_diag(jnp.swapaxes(ssm_c_im[0], 1, 2)).astype(BF16)
    wglu = ssm_w_glu[0].astype(BF16)
    w_in = conv_w_in[0].astype(BF16)
    w_out = conv_w_out[0].astype(BF16)

    row_tiles_p = rows_p // FFN_ROWS
    flat_block = (FFN_ROWS, D_MODEL)
    flat_map = lambda i: (i, 0)

    whole_s = (x_sample.shape, lambda i: (0, 0, 0))
    flat_s = (flat_block, flat_map)

    def ffn_sample(xs, layer, half, x_spec=flat_s, out_spec=flat_s, final_g=None):
        out_shape = x_sample.shape if out_spec is whole_s else (rows_s, D_MODEL)
        return _ffn_call(xs, *x_spec, out_shape, *out_spec, 1,
                         norm_g[layer, 2 * half], (ffn_w_gate_up, ffn_w_down, (layer, half)),
                         FFN_COLS_F32, final_g=final_g)

    xs, w_bf_00 = ffn_sample(x_sample, 0, 0, x_spec=whole_s)
    seq_block = (len_s, SAMPLE_SEQ_BLOCK, D_MODEL)
    seq_map = lambda i: (0, i, 0)
    seq_grid = (nb_s // SAMPLE_SEQ_BLOCK,)
    xs, sre_s, sim_s = _s5_call(
        xs.reshape(len_s, nb_s, D_MODEL), seq_block, seq_map, seq_grid,
        state_ssm_re[0].reshape(nb_s, S_DIM), state_ssm_im[0].reshape(nb_s, S_DIM), lambda i: (i, 0),
        norm_g[0, 1], lb_re, lb_im, wb_re, wb_im, wc_re, wc_im, ssm_d[0], wglu,
        steps=len_s, sub_steps=len_s, n_seq=SAMPLE_SEQ_BLOCK, carry_over_grid=False)
    xs = xs.reshape(rows_s, D_MODEL)
    xs, w_bf_01 = ffn_sample(xs, 0, 1)
    xs, w_bf_10 = ffn_sample(xs, 1, 0)
    xs, buf_s = _conv_call(
        xs.reshape(len_s, nb_s, D_MODEL), seq_block, seq_map, seq_grid,
        cache_conv[0], lambda i: (i, 0, 0), norm_g[1, 1], w_in, conv_w[0], w_out,
        steps=len_s, sub_steps=len_s, n_seq=SAMPLE_SEQ_BLOCK, carry_over_grid=False)
    xs = xs.reshape(rows_s, D_MODEL)
    y_sample, w_bf_11 = ffn_sample(xs, 1, 1, out_spec=whole_s, final_g=final_norm_g)

    def ffn(x, out_shape, g, w_bf, final_g=None):
        return _ffn_resident_call(x, out_shape, row_tiles_p, g, w_bf, FFN_COLS, final_g=final_g)

    xp = ffn(x_prompt, (rows_p, D_MODEL), norm_g[0, 0], w_bf_00)
    zero_state = jnp.zeros((nb_p, S_DIM), F32)
    xp, sre_p, sim_p = _s5_call(
        xp, (S5_PROMPT_STEPS * nb_p, D_MODEL), flat_map, (len_p // S5_PROMPT_STEPS,),
        zero_state, zero_state, lambda i: (0, 0), norm_g[0, 1], lb_re, lb_im,
        wb_re, wb_im, wc_re, wc_im, ssm_d[0], wglu,
        steps=S5_PROMPT_STEPS, sub_steps=S5_PROMPT_SUB_STEPS, n_seq=nb_p, carry_over_grid=True)
    xp = ffn(xp, (rows_p, D_MODEL), norm_g[0, 2], w_bf_01)
    xp = ffn(xp, (rows_p, D_MODEL), norm_g[1, 0], w_bf_10)
    xp, buf_p = _conv_call(
        xp, (CONV_PROMPT_STEPS * nb_p, D_MODEL), flat_map, (len_p // CONV_PROMPT_STEPS,),
        jnp.zeros((nb_p, CONV_W - 1, D_MODEL), F32), lambda i: (0, 0, 0),
        norm_g[1, 1], w_in, conv_w[0], w_out,
        steps=CONV_PROMPT_STEPS, sub_steps=CONV_PROMPT_SUB_STEPS, n_seq=nb_p, carry_over_grid=True)
    y_prompt = ffn(xp, x_prompt.shape, norm_g[1, 2], w_bf_11, final_g=final_norm_g)

    state4 = lambda s: s.reshape(1, -1, N_GROUPS, P_STATE)
    return (y_prompt, y_sample, state4(sre_p), state4(sim_p), buf_p[None],
            state4(sre_s), state4(sim_s), buf_s[None])
```

```python
import functools

import jax
import jax.numpy as jnp
from jax import lax
from jax.experimental import pallas as pl
from jax.experimental.pallas import tpu as pltpu

F32 = jnp.float32
BF16 = jnp.bfloat16

D_MODEL = 1024
D_FF = 4 * D_MODEL
GROUP_SIZE = 16
N_GROUPS = D_MODEL // GROUP_SIZE
P_STATE = 64
S_DIM = N_GROUPS * P_STATE
CONV_W = 3
EPS = 1e-6

SUBLANES = 8
FFN_ROWS = 1024
FFN_COLS = 1024
FFN_COLS_F32 = 512
FFN_ROW_BLOCK = 256
SSM_BLOCKS = 4
SSM_BLOCK_CH = D_MODEL // SSM_BLOCKS
SSM_BLOCK_ST = S_DIM // SSM_BLOCKS
SCAN_LANES = 256
VMEM_LIMIT = 56 * 1024 * 1024


def _rmsnorm(x, g):
    return x * lax.rsqrt(jnp.mean(x * x, axis=-1, keepdims=True) + EPS) * g


def _dot(a, b):
    return jnp.dot(a, b, preferred_element_type=F32)


def _ffn_kernel(x_ref, g_ref, wg_ref, wu_ref, wd_ref, *rest, n_chunks, final_norm,
                x_seq_major, out_seq_major, emit_bf16):
    rest = list(rest)
    gf_ref = rest.pop(0) if final_norm else None
    o_ref = rest.pop(0)
    if emit_bf16:
        wg_bf, wu_bf, wd_bf = rest[:3]
        del rest[:3]
    h_ref, acc_ref = rest[:2]
    xt_ref = rest[2] if x_seq_major else x_ref
    j = pl.program_id(1)

    @pl.when(j == 0)
    def _():
        if x_seq_major:
            n_seq = x_ref.shape[0]
            for t in range(x_ref.shape[1]):
                xt_ref[t * n_seq:(t + 1) * n_seq, :] = x_ref[:, t, :]
        h_ref[...] = _rmsnorm(xt_ref[...], g_ref[...]).astype(BF16)
        acc_ref[...] = jnp.zeros_like(acc_ref)

    if emit_bf16:
        wg_bf[...] = wg_ref[...].astype(BF16)
        wu_bf[...] = wu_ref[...].astype(BF16)
        wd_bf[...] = wd_ref[...].astype(BF16)
        wg_ref, wu_ref, wd_ref = wg_bf, wu_bf, wd_bf

    for r in range(0, FFN_ROWS, FFN_ROW_BLOCK):
        h = h_ref[r:r + FFN_ROW_BLOCK, :]
        gate = _dot(h, wg_ref[...])
        up = _dot(h, wu_ref[...])
        act = (jax.nn.silu(gate) * up).astype(BF16)
        acc_ref[r:r + FFN_ROW_BLOCK, :] += _dot(act, wd_ref[...])

    @pl.when(j == n_chunks - 1)
    def _():
        out = xt_ref[...] + 0.5 * acc_ref[...]
        if final_norm:
            out = _rmsnorm(out, gf_ref[...])
        if out_seq_major:
            n_seq = o_ref.shape[0]
            for t in range(o_ref.shape[1]):
                o_ref[:, t, :] = out[t * n_seq:(t + 1) * n_seq, :]
        else:
            o_ref[...] = out


def _ffn_call(x, x_block, x_map, out_shape, out_block, out_map, n_tiles, g, weights, cols,
              final_g=None):
    n_chunks = D_FF // cols
    emit_bf16 = isinstance(weights[-1], tuple)
    x_seq_major = len(x_block) == 3
    out_seq_major = len(out_block) == 3
    body = functools.partial(_ffn_kernel, n_chunks=n_chunks, final_norm=final_g is not None,
                             x_seq_major=x_seq_major, out_seq_major=out_seq_major, emit_bf16=emit_bf16)
    scratch = [pltpu.VMEM((FFN_ROWS, D_MODEL), BF16), pltpu.VMEM((FFN_ROWS, D_MODEL), F32)]
    if x_seq_major:
        scratch.append(pltpu.VMEM((FFN_ROWS, D_MODEL), F32))

    const2 = lambda i, j: (0, 0)
    gate_bf = pl.BlockSpec((D_MODEL, cols), lambda i, j: (0, j))
    down_bf = pl.BlockSpec((cols, D_MODEL), lambda i, j: (j, 0))
    out_specs = pl.BlockSpec(out_block, lambda i, j: out_map(i))
    out_shapes = jax.ShapeDtypeStruct(out_shape, F32)
    if emit_bf16:
        w_gu, w_down, (layer, half) = weights
        w_specs = [
            pl.BlockSpec((None, None, D_MODEL, cols), lambda i, j: (layer, half, 0, j)),
            pl.BlockSpec((None, None, D_MODEL, cols), lambda i, j: (layer, half, 0, j + n_chunks)),
            pl.BlockSpec((None, None, cols, D_MODEL), lambda i, j: (layer, half, j, 0)),
        ]
        w_args = [w_gu, w_gu, w_down]
        out_specs = (out_specs, gate_bf, gate_bf, down_bf)
        out_shapes = (out_shapes,
                      jax.ShapeDtypeStruct((D_MODEL, D_FF), BF16),
                      jax.ShapeDtypeStruct((D_MODEL, D_FF), BF16),
                      jax.ShapeDtypeStruct((D_FF, D_MODEL), BF16))
    else:
        w_specs = [gate_bf, gate_bf, down_bf]
        w_args = list(weights)
    in_specs = [pl.BlockSpec(x_block, lambda i, j: x_map(i)), pl.BlockSpec((1, D_MODEL), const2)] + w_specs
    args = [x, g.reshape(1, D_MODEL)] + w_args
    if final_g is not None:
        in_specs.append(pl.BlockSpec((1, D_MODEL), const2))
        args.append(final_g.reshape(1, D_MODEL))
    res = pl.pallas_call(
        body,
        grid=(n_tiles, n_chunks),
        in_specs=in_specs,
        out_specs=out_specs,
        out_shape=out_shapes,
        scratch_shapes=scratch,
        compiler_params=pltpu.CompilerParams(
            dimension_semantics=("arbitrary", "arbitrary"), vmem_limit_bytes=VMEM_LIMIT),
        name="ffn",
    )(*args)
    return (res[0], tuple(res[1:])) if emit_bf16 else res


def _ffn_resident_kernel(x_ref, g_ref, wg_ref, wu_ref, wd_ref, *rest, cols, final_norm,
                         x_seq_major, out_seq_major, n_cast, cast_ffn):
    rest = list(rest)
    gf_ref = rest.pop(0) if final_norm else None
    cast_in = [rest.pop(0) for _ in range(n_cast)]
    if cast_ffn:
        wgu_src, wd_src = rest[:2]
        del rest[:2]
    o_ref = rest.pop(0)
    for src in cast_in:
        rest.pop(0)[...] = src[...].astype(BF16)
    if cast_ffn:
        wg_dst, wu_dst, wd_dst = rest[:3]
        del rest[:3]
        wg_dst[...] = wgu_src[:, :D_FF].astype(BF16)
        wu_dst[...] = wgu_src[:, D_FF:].astype(BF16)
        wd_dst[...] = wd_src[...].astype(BF16)
    if x_seq_major:
        xbuf, xsem = rest[:2]
        del rest[:2]
    if out_seq_major:
        obuf, osem = rest[:2]
    i = pl.program_id(0)
    n_tiles = pl.num_programs(0)
    slot = i % 2
    tile_steps = FFN_ROWS // SUBLANES
    steps_per_block = FFN_ROW_BLOCK // SUBLANES

    def x_copies(tile, slot_):
        return [pltpu.make_async_copy(x_ref.at[s, pl.ds(tile * tile_steps, tile_steps), :],
                                      xbuf.at[slot_, :, s, :], xsem.at[slot_]) for s in range(SUBLANES)]

    def o_copies(tile, slot_):
        return [pltpu.make_async_copy(obuf.at[slot_, :, s, :],
                                      o_ref.at[s, pl.ds(tile * tile_steps, tile_steps), :],
                                      osem.at[slot_]) for s in range(SUBLANES)]

    if x_seq_major:
        @pl.when(i == 0)
        def _():
            for cp in x_copies(0, 0):
                cp.start()

        @pl.when(i + 1 < n_tiles)
        def _():
            for cp in x_copies(i + 1, 1 - slot):
                cp.start()

        for cp in x_copies(i, slot):
            cp.wait()
    if out_seq_major:
        @pl.when(i >= 2)
        def _():
            for cp in o_copies(i - 2, slot):
                cp.wait()

    for r in range(0, FFN_ROWS, FFN_ROW_BLOCK):
        t0 = r // SUBLANES
        if x_seq_major:
            x = xbuf[slot, t0:t0 + steps_per_block].reshape(FFN_ROW_BLOCK, D_MODEL)
        else:
            x = x_ref[r:r + FFN_ROW_BLOCK, :]
        h = _rmsnorm(x, g_ref[...]).astype(BF16)
        acc = None
        for c in range(0, D_FF, cols):
            gate = _dot(h, wg_ref[:, c:c + cols])
            up = _dot(h, wu_ref[:, c:c + cols])
            act = (jax.nn.silu(gate) * up).astype(BF16)
            part = _dot(act, wd_ref[c:c + cols, :])
            acc = part if acc is None else acc + part
        out = x + 0.5 * acc
        if final_norm:
            out = _rmsnorm(out, gf_ref[...])
        if out_seq_major:
            obuf[slot, t0:t0 + steps_per_block] = out.reshape(steps_per_block, SUBLANES, D_MODEL)
        else:
            o_ref[r:r + FFN_ROW_BLOCK, :] = out

    if out_seq_major:
        for cp in o_copies(i, slot):
            cp.start()

        @pl.when(i == n_tiles - 1)
        def _():
            for cp in o_copies(i, slot):
                cp.wait()

        @pl.when(jnp.logical_and(i == n_tiles - 1, i >= 1))
        def _():
            for cp in o_copies(i - 1, 1 - slot):
                cp.wait()


def _ffn_resident_call(x, out_shape, n_tiles, g, weights, cols, final_g=None, cast=(), cast_ffn=None):
    x_seq_major = x.ndim == 3
    out_seq_major = len(out_shape) == 3
    body = functools.partial(_ffn_resident_kernel, cols=cols, final_norm=final_g is not None,
                             x_seq_major=x_seq_major, out_seq_major=out_seq_major, n_cast=len(cast),
                             cast_ffn=cast_ffn is not None)
    whole = lambda a: pl.BlockSpec(a.shape, lambda i: (0,) * a.ndim)
    row_tile = pl.BlockSpec((FFN_ROWS, D_MODEL), lambda i: (i, 0))
    in_hbm = pl.BlockSpec(memory_space=pl.ANY)
    g2 = g.reshape(1, D_MODEL)
    in_specs = [in_hbm if x_seq_major else row_tile, whole(g2)] + [whole(w) for w in weights]
    args = [x, g2, *weights]
    if final_g is not None:
        gf2 = final_g.reshape(1, D_MODEL)
        in_specs.append(whole(gf2))
        args.append(gf2)
    slab = lambda a: pl.BlockSpec((a.shape[0] // n_tiles, a.shape[1]), lambda i: (i, 0))
    in_specs += [slab(a) for a in cast]
    args += list(cast)
    out_specs = [in_hbm if out_seq_major else row_tile] + [slab(a) for a in cast]
    out_shapes = [jax.ShapeDtypeStruct(out_shape, F32)] + [jax.ShapeDtypeStruct(a.shape, BF16) for a in cast]
    if cast_ffn is not None:
        w_gu, w_down, (layer, half) = cast_ffn
        in_specs += [pl.BlockSpec((None, None, D_MODEL // n_tiles, 2 * D_FF), lambda i: (layer, half, i, 0)),
                     pl.BlockSpec((None, None, D_FF // n_tiles, D_MODEL), lambda i: (layer, half, i, 0))]
        args += [w_gu, w_down]
        out_specs += [pl.BlockSpec((D_MODEL // n_tiles, D_FF), lambda i: (i, 0))] * 2
        out_specs += [pl.BlockSpec((D_FF // n_tiles, D_MODEL), lambda i: (i, 0))]
        out_shapes += [jax.ShapeDtypeStruct((D_MODEL, D_FF), BF16)] * 2
        out_shapes += [jax.ShapeDtypeStruct((D_FF, D_MODEL), BF16)]
    reorder_scratch = [pltpu.VMEM((2, FFN_ROWS // SUBLANES, SUBLANES, D_MODEL), F32),
                       pltpu.SemaphoreType.DMA((2,))]
    scratch = (reorder_scratch if x_seq_major else []) + (reorder_scratch if out_seq_major else [])
    res = pl.pallas_call(
        body,
        grid=(n_tiles,),
        in_specs=in_specs,
        out_specs=tuple(out_specs),
        out_shape=tuple(out_shapes),
        scratch_shapes=scratch,
        compiler_params=pltpu.CompilerParams(
            dimension_semantics=("arbitrary",), vmem_limit_bytes=VMEM_LIMIT),
        name="ffn_resident",
    )(*args)
    n_plain = 1 + len(cast)
    if cast_ffn is not None:
        return (*res[:n_plain], tuple(res[n_plain:]))
    return res if cast else res[0]


def _ssm_prep_kernel(lam_re_ref, lam_im_ref, ldt_ref, lam_re_rep_ref, lam_im_rep_ref,
                     b_re_ref, b_im_ref, lb_re_ref, lb_im_ref, bb_re_ref, bb_im_ref):
    dt = jnp.exp(ldt_ref[...])

    def discretise(lam_re, lam_im):
        mag = jnp.exp(lam_re * dt)
        lb_re = mag * jnp.cos(lam_im * dt)
        lb_im = mag * jnp.sin(lam_im * dt)
        return lb_re, lb_im

    lb_re, lb_im = discretise(lam_re_ref[...], lam_im_ref[...])
    lb_re_ref[...] = lb_re
    lb_im_ref[...] = lb_im
    lam_re = lam_re_rep_ref[...]
    lam_im = lam_im_rep_ref[...]
    lbr, lbi = discretise(lam_re, lam_im)
    den = lam_re * lam_re + lam_im * lam_im
    nr = lbr - 1.0
    ni = lbi
    f_re = (nr * lam_re + ni * lam_im) / den
    f_im = (ni * lam_re - nr * lam_im) / den
    b_re = b_re_ref[...]
    b_im = b_im_ref[...]
    bb_re_ref[...] = f_re * b_re - f_im * b_im
    bb_im_ref[...] = f_re * b_im + f_im * b_re


def _ssm_prep(lam_re, lam_im, log_dt, b_re, b_im):
    pc = P_STATE * GROUP_SIZE
    rep = lambda a: jnp.repeat(a, GROUP_SIZE, axis=-1)
    shp = lambda n: jax.ShapeDtypeStruct((N_GROUPS, n), F32)
    return pl.pallas_call(
        _ssm_prep_kernel,
        out_shape=(shp(P_STATE), shp(P_STATE), shp(pc), shp(pc)),
        name="ssm_prep",
    )(lam_re, lam_im, log_dt.reshape(N_GROUPS, 1), rep(lam_re), rep(lam_im),
      b_re.reshape(N_GROUPS, pc), b_im.reshape(N_GROUPS, pc))


def _block_diag(w):
    gpb = N_GROUPS // SSM_BLOCKS
    a, b = w.shape[1], w.shape[2]
    tiled = jnp.tile(w.reshape(SSM_BLOCKS, gpb * a, b), (1, 1, gpb))
    row_group = lax.broadcasted_iota(jnp.int32, tiled.shape[1:], 0) // a
    col_group = lax.broadcasted_iota(jnp.int32, tiled.shape[1:], 1) // b
    return jnp.where((row_group == col_group)[None], tiled, 0.0)


def _s5_kernel(x_ref, g_ref, s0_re_ref, s0_im_ref, lb_re_ref, lb_im_ref,
               wb_re_ref, wb_im_ref, wc_re_ref, wc_im_ref, d_ref, wglu_ref,
               o_ref, new_re_ref, new_im_ref,
               st_re, st_im, bu_re, bu_im, *, steps, sub_steps, n_seq, carry_over_grid):
    sub_rows = sub_steps * n_seq
    if carry_over_grid:
        @pl.when(pl.program_id(0) == 0)
        def _():
            st_re[...] = s0_re_ref[...]
            st_im[...] = s0_im_ref[...]
    else:
        st_re[...] = s0_re_ref[...]
        st_im[...] = s0_im_ref[...]

    for s in range(steps // sub_steps):
        t0 = s * sub_steps
        if len(x_ref.shape) == 3:
            x = x_ref[t0:t0 + sub_steps].reshape(sub_rows, D_MODEL)
        else:
            x = x_ref[t0 * n_seq:t0 * n_seq + sub_rows, :]
        u = _rmsnorm(x, g_ref[...])
        ub = u.astype(BF16)

        def project_in(k):
            uk = ub[:, k * SSM_BLOCK_CH:(k + 1) * SSM_BLOCK_CH]
            bu_re[k % 2] = _dot(uk, wb_re_ref[k])
            bu_im[k % 2] = _dot(uk, wb_im_ref[k])

        def scan_block(k):
            b_re, b_im = bu_re.at[k % 2], bu_im.at[k % 2]
            for c in range(SSM_BLOCK_ST // SCAN_LANES):
                lo = c * SCAN_LANES
                glo = k * SSM_BLOCK_ST + lo
                lr = jnp.broadcast_to(lb_re_ref[:, glo:glo + SCAN_LANES], (SUBLANES, SCAN_LANES))
                li = jnp.broadcast_to(lb_im_ref[:, glo:glo + SCAN_LANES], (SUBLANES, SCAN_LANES))
                for r_state in range(0, n_seq, SUBLANES):
                    hr = st_re[r_state:r_state + SUBLANES, glo:glo + SCAN_LANES]
                    hi = st_im[r_state:r_state + SUBLANES, glo:glo + SCAN_LANES]
                    for t in range(sub_steps):
                        r = t * n_seq + r_state
                        br = b_re[r:r + SUBLANES, lo:lo + SCAN_LANES]
                        bi = b_im[r:r + SUBLANES, lo:lo + SCAN_LANES]
                        hr, hi = lr * hr - li * hi + br, lr * hi + li * hr + bi
                        b_re[r:r + SUBLANES, lo:lo + SCAN_LANES] = hr
                        b_im[r:r + SUBLANES, lo:lo + SCAN_LANES] = hi
                    st_re[r_state:r_state + SUBLANES, glo:glo + SCAN_LANES] = hr
                    st_im[r_state:r_state + SUBLANES, glo:glo + SCAN_LANES] = hi

        ys = []
        project_in(0)
        for k in range(SSM_BLOCKS):
            if k + 1 < SSM_BLOCKS:
                project_in(k + 1)
            scan_block(k)
            hr = bu_re[k % 2].astype(BF16)
            hi = bu_im[k % 2].astype(BF16)
            ys.append(_dot(hr, wc_re_ref[k]) - _dot(hi, wc_im_ref[k]))
        y = jnp.concatenate(ys, axis=-1) + d_ref[...] * u
        z = _dot(jax.nn.gelu(y).astype(BF16), wglu_ref[...])
        m = z[:, :D_MODEL] * jax.nn.sigmoid(z[:, D_MODEL:])
        if len(o_ref.shape) == 3:
            o_ref[t0:t0 + sub_steps] = (x + m).reshape(sub_steps, n_seq, D_MODEL)
        else:
            o_ref[t0 * n_seq:t0 * n_seq + sub_rows, :] = x + m
    new_re_ref[...] = st_re[...]
    new_im_ref[...] = st_im[...]


def _s5_call(x, x_block, x_map, grid, s0_re, s0_im, state_map, g, lb_re, lb_im,
             wb_re, wb_im, wc_re, wc_im, d_skip, wglu, *, steps, sub_steps, n_seq, carry_over_grid):
    sub_rows = sub_steps * n_seq
    n_state = s0_re.shape[0]
    whole = lambda a: pl.BlockSpec(a.shape, lambda i: (0,) * a.ndim)
    state_spec = pl.BlockSpec((n_seq, S_DIM), state_map)
    g2 = g.reshape(1, D_MODEL)
    d2 = d_skip.reshape(1, D_MODEL)
    lbr = lb_re.reshape(1, S_DIM)
    lbi = lb_im.reshape(1, S_DIM)
    body = functools.partial(_s5_kernel, steps=steps, sub_steps=sub_steps, n_seq=n_seq,
                             carry_over_grid=carry_over_grid)
    return pl.pallas_call(
        body,
        grid=grid,
        in_specs=[pl.BlockSpec(x_block, x_map), whole(g2), state_spec, state_spec,
                  whole(lbr), whole(lbi), whole(wb_re), whole(wb_im), whole(wc_re), whole(wc_im),
                  whole(d2), whole(wglu)],
        out_specs=(pl.BlockSpec(x_block, x_map), state_spec, state_spec),
        out_shape=(jax.ShapeDtypeStruct(x.shape, F32),
                   jax.ShapeDtypeStruct((n_state, S_DIM), F32),
                   jax.ShapeDtypeStruct((n_state, S_DIM), F32)),
        scratch_shapes=[pltpu.VMEM((n_seq, S_DIM), F32), pltpu.VMEM((n_seq, S_DIM), F32),
                        pltpu.VMEM((2, sub_rows, SSM_BLOCK_ST), F32),
                        pltpu.VMEM((2, sub_rows, SSM_BLOCK_ST), F32)],
        compiler_params=pltpu.CompilerParams(
            dimension_semantics=("arbitrary",), vmem_limit_bytes=VMEM_LIMIT),
        name="s5_mixer",
    )(x, g2, s0_re, s0_im, lbr, lbi, wb_re, wb_im, wc_re, wc_im, d2, wglu)


def _conv_kernel(x_ref, g_ref, buf0_ref, win_ref, cw_ref, wout_ref, o_ref, newbuf_ref, zp,
                 *, steps, sub_steps, n_seq, carry_over_grid):
    rows = steps * n_seq
    halo = (CONV_W - 1) * n_seq

    def load_buf0():
        for k in range(CONV_W - 1):
            zp[k * n_seq:(k + 1) * n_seq, :] = buf0_ref[:, k, :]

    if carry_over_grid:
        pl.when(pl.program_id(0) == 0)(load_buf0)
    else:
        load_buf0()

    sub_rows = sub_steps * n_seq
    for s in range(steps // sub_steps):
        t0 = s * sub_steps
        r0 = t0 * n_seq
        if len(x_ref.shape) == 3:
            x = x_ref[t0:t0 + sub_steps].reshape(sub_rows, D_MODEL)
        else:
            x = x_ref[r0:r0 + sub_rows, :]
        h = _rmsnorm(x, g_ref[...]).astype(BF16)
        p = _dot(h, win_ref[...])
        gb = p[:, :D_MODEL]
        zp[halo + r0:halo + r0 + sub_rows, :] = p[:, D_MODEL:2 * D_MODEL] * p[:, 2 * D_MODEL:]
        conv = cw_ref[0:1, :] * zp[r0:r0 + sub_rows, :]
        for k in range(1, CONV_W):
            conv = conv + cw_ref[k:k + 1, :] * zp[r0 + k * n_seq:r0 + k * n_seq + sub_rows, :]
        m = _dot((gb * conv).astype(BF16), wout_ref[...])
        if len(o_ref.shape) == 3:
            o_ref[t0:t0 + sub_steps] = (x + m).reshape(sub_steps, n_seq, D_MODEL)
        else:
            o_ref[r0:r0 + sub_rows, :] = x + m
    tail = zp[rows:rows + halo, :]
    zp[0:halo, :] = tail
    for k in range(CONV_W - 1):
        newbuf_ref[:, k, :] = tail[k * n_seq:(k + 1) * n_seq, :]


def _conv_call(x, x_block, x_map, grid, buf0, buf_map, g, w_in, cw, w_out,
               *, steps, sub_steps, n_seq, carry_over_grid):
    rows = steps * n_seq
    halo = (CONV_W - 1) * n_seq
    whole = lambda a: pl.BlockSpec(a.shape, lambda i: (0,) * a.ndim)
    g2 = g.reshape(1, D_MODEL)
    buf_block = (n_seq, CONV_W - 1, D_MODEL)
    body = functools.partial(_conv_kernel, steps=steps, sub_steps=sub_steps, n_seq=n_seq,
                             carry_over_grid=carry_over_grid)
    return pl.pallas_call(
        body,
        grid=grid,
        in_specs=[pl.BlockSpec(x_block, x_map), whole(g2), pl.BlockSpec(buf_block, buf_map),
                  whole(w_in), whole(cw), whole(w_out)],
        out_specs=(pl.BlockSpec(x_block, x_map), pl.BlockSpec(buf_block, buf_map)),
        out_shape=(jax.ShapeDtypeStruct(x.shape, F32), jax.ShapeDtypeStruct(buf0.shape, F32)),
        scratch_shapes=[pltpu.VMEM((halo + rows, D_MODEL), F32)],
        compiler_params=pltpu.CompilerParams(
            dimension_semantics=("arbitrary",), vmem_limit_bytes=VMEM_LIMIT),
        name="conv_mixer",
    )(x, g2, buf0, w_in, cw, w_out)


S5_PROMPT_STEPS = 128
S5_PROMPT_SUB_STEPS = 32
CONV_PROMPT_STEPS = 128
CONV_PROMPT_SUB_STEPS = 32
SAMPLE_SEQ_BLOCK = 32


def kernel(x_prompt, x_sample, state_ssm_re, state_ssm_im, cache_conv, norm_g, final_norm_g, ffn_w_gate_up, ffn_w_down, ssm_lam_re, ssm_lam_im, ssm_log_dt, ssm_b_re, ssm_b_im, ssm_c_re, ssm_c_im, ssm_d, ssm_w_glu, conv_w_in, conv_w, conv_w_out):
    nb_p, len_p, _ = x_prompt.shape
    nb_s, len_s, _ = x_sample.shape
    rows_p = nb_p * len_p
    rows_s = nb_s * len_s
    assert nb_p == SUBLANES and len_p % FFN_ROWS == 0 and rows_s == FFN_ROWS
    assert nb_s % SAMPLE_SEQ_BLOCK == 0

    lb_re, lb_im, bb_re, bb_im = _ssm_prep(ssm_lam_re[0], ssm_lam_im[0], ssm_log_dt[0],
                                           ssm_b_re[0], ssm_b_im[0])
    to_gcp = lambda a: jnp.swapaxes(a.reshape(N_GROUPS, P_STATE, GROUP_SIZE), 1, 2)
    wb_re = _block_diag(to_gcp(bb_re)).astype(BF16)
    wb_im = _block_diag(to_gcp(bb_im)).astype(BF16)
    wc_re = _block_diag(jnp.swapaxes(ssm_c_re[0], 1, 2)).astype(BF16)
    wc_im = _block_diag(jnp.swapaxes(ssm_c_im[0], 1, 2)).astype(BF16)

    row_tiles_p = rows_p // FFN_ROWS
    flat_block = (FFN_ROWS, D_MODEL)
    flat_map = lambda i: (i, 0)

    whole_s = (x_sample.shape, lambda i: (0, 0, 0))
    flat_s = (flat_block, flat_map)

    def ffn_sample(xs, g, weights, cols, x_spec=flat_s, out_spec=flat_s, final_g=None):
        out_shape = x_sample.shape if out_spec is whole_s else (rows_s, D_MODEL)
        return _ffn_call(xs, *x_spec, out_shape, *out_spec, 1, g, weights, cols, final_g=final_g)

    def ffn(x, out_shape, g, w_bf, final_g=None, cast=(), cast_ffn=None):
        return _ffn_resident_call(x, out_shape, row_tiles_p, g, w_bf, FFN_COLS, final_g=final_g, cast=cast,
                                  cast_ffn=None if cast_ffn is None else (ffn_w_gate_up, ffn_w_down, cast_ffn))

    xs, w_bf_00 = ffn_sample(x_sample, norm_g[0, 0], (ffn_w_gate_up, ffn_w_down, (0, 0)), FFN_COLS_F32,
                             x_spec=whole_s)
    xp, wglu, w_bf_01 = ffn(x_prompt, (rows_p, D_MODEL), norm_g[0, 0], w_bf_00,
                            cast=(ssm_w_glu[0],), cast_ffn=(0, 1))

    seq_block = (len_s, SAMPLE_SEQ_BLOCK, D_MODEL)
    seq_map = lambda i: (0, i, 0)
    seq_grid = (nb_s // SAMPLE_SEQ_BLOCK,)
    xs, sre_s, sim_s = _s5_call(
        xs.reshape(len_s, nb_s, D_MODEL), seq_block, seq_map, seq_grid,
        state_ssm_re[0].reshape(nb_s, S_DIM), state_ssm_im[0].reshape(nb_s, S_DIM), lambda i: (i, 0),
        norm_g[0, 1], lb_re, lb_im, wb_re, wb_im, wc_re, wc_im, ssm_d[0], wglu,
        steps=len_s, sub_steps=len_s, n_seq=SAMPLE_SEQ_BLOCK, carry_over_grid=False)
    xs = ffn_sample(xs.reshape(rows_s, D_MODEL), norm_g[0, 2], w_bf_01, FFN_COLS)

    zero_state = jnp.zeros((nb_p, S_DIM), F32)
    xp, sre_p, sim_p = _s5_call(
        xp, (S5_PROMPT_STEPS * nb_p, D_MODEL), flat_map, (len_p // S5_PROMPT_STEPS,),
        zero_state, zero_state, lambda i: (0, 0), norm_g[0, 1], lb_re, lb_im,
        wb_re, wb_im, wc_re, wc_im, ssm_d[0], wglu,
        steps=S5_PROMPT_STEPS, sub_steps=S5_PROMPT_SUB_STEPS, n_seq=nb_p, carry_over_grid=True)
    xp, w_in, w_out, w_bf_10 = ffn(xp, (rows_p, D_MODEL), norm_g[0, 2], w_bf_01,
                                   cast=(conv_w_in[0], conv_w_out[0]), cast_ffn=(1, 0))
    xs = ffn_sample(xs, norm_g[1, 0], w_bf_10, FFN_COLS)
    xp, w_bf_11 = ffn(xp, (rows_p, D_MODEL), norm_g[1, 0], w_bf_10, cast_ffn=(1, 1))

    xs, buf_s = _conv_call(
        xs.reshape(len_s, nb_s, D_MODEL), seq_block, seq_map, seq_grid,
        cache_conv[0], lambda i: (i, 0, 0), norm_g[1, 1], w_in, conv_w[0], w_out,
        steps=len_s, sub_steps=len_s, n_seq=SAMPLE_SEQ_BLOCK, carry_over_grid=False)
    y_sample = ffn_sample(xs.reshape(rows_s, D_MODEL), norm_g[1, 2], w_bf_11, FFN_COLS,
                          out_spec=whole_s, final_g=final_norm_g)

    xp, buf_p = _conv_call(
        xp, (CONV_PROMPT_STEPS * nb_p, D_MODEL), flat_map, (len_p // CONV_PROMPT_STEPS,),
        jnp.zeros((nb_p, CONV_W - 1, D_MODEL), F32), lambda i: (0, 0, 0),
        norm_g[1, 1], w_in, conv_w[0], w_out,
        steps=CONV_PROMPT_STEPS, sub_steps=CONV_PROMPT_SUB_STEPS, n_seq=nb_p, carry_over_grid=True)
    y_prompt = ffn(xp, x_prompt.shape, norm_g[1, 2], w_bf_11, final_g=final_norm_g)

    state4 = lambda s: s.reshape(1, -1, N_GROUPS, P_STATE)
    return (y_prompt, y_sample, state4(sre_p), state4(sim_p), buf_p[None],
            state4(sre_s), state4(sim_s), buf_s[None])
```

```python
import functools

import jax
import jax.numpy as jnp
from jax import lax
from jax.experimental import pallas as pl
from jax.experimental.pallas import tpu as pltpu

F32 = jnp.float32
BF16 = jnp.bfloat16

D_MODEL = 1024
D_FF = 4 * D_MODEL
GROUP_SIZE = 16
N_GROUPS = D_MODEL // GROUP_SIZE
P_STATE = 64
S_DIM = N_GROUPS * P_STATE
CONV_W = 3
EPS = 1e-6

SUBLANES = 8
FFN_ROWS = 1024
FFN_COLS = 1024
FFN_COLS_F32 = 512
FFN_ROW_BLOCK = 256
SSM_BLOCKS = 4
SSM_BLOCK_CH = D_MODEL // SSM_BLOCKS
SSM_BLOCK_ST = S_DIM // SSM_BLOCKS
SCAN_LANES = 256
VMEM_LIMIT = 56 * 1024 * 1024


def _rmsnorm(x, g):
    return x * lax.rsqrt(jnp.mean(x * x, axis=-1, keepdims=True) + EPS) * g


def _dot(a, b):
    return jnp.dot(a, b, preferred_element_type=F32)


def _ffn_kernel(x_ref, g_ref, wg_ref, wu_ref, wd_ref, *rest, n_chunks, final_norm,
                x_seq_major, out_seq_major, emit_bf16):
    rest = list(rest)
    gf_ref = rest.pop(0) if final_norm else None
    o_ref = rest.pop(0)
    if emit_bf16:
        wg_bf, wu_bf, wd_bf = rest[:3]
        del rest[:3]
    h_ref, acc_ref = rest[:2]
    xt_ref = rest[2] if x_seq_major else x_ref
    j = pl.program_id(1)

    @pl.when(j == 0)
    def _():
        if x_seq_major:
            n_seq = x_ref.shape[0]
            for t in range(x_ref.shape[1]):
                xt_ref[t * n_seq:(t + 1) * n_seq, :] = x_ref[:, t, :]
        h_ref[...] = _rmsnorm(xt_ref[...], g_ref[...]).astype(BF16)
        acc_ref[...] = jnp.zeros_like(acc_ref)

    if emit_bf16:
        wg_bf[...] = wg_ref[...].astype(BF16)
        wu_bf[...] = wu_ref[...].astype(BF16)
        wd_bf[...] = wd_ref[...].astype(BF16)
        wg_ref, wu_ref, wd_ref = wg_bf, wu_bf, wd_bf

    for r in range(0, FFN_ROWS, FFN_ROW_BLOCK):
        h = h_ref[r:r + FFN_ROW_BLOCK, :]
        gate = _dot(h, wg_ref[...])
        up = _dot(h, wu_ref[...])
        act = (jax.nn.silu(gate) * up).astype(BF16)
        acc_ref[r:r + FFN_ROW_BLOCK, :] += _dot(act, wd_ref[...])

    @pl.when(j == n_chunks - 1)
    def _():
        out = xt_ref[...] + 0.5 * acc_ref[...]
        if final_norm:
            out = _rmsnorm(out, gf_ref[...])
        if out_seq_major:
            n_seq = o_ref.shape[0]
            for t in range(o_ref.shape[1]):
                o_ref[:, t, :] = out[t * n_seq:(t + 1) * n_seq, :]
        else:
            o_ref[...] = out


def _ffn_call(x, x_block, x_map, out_shape, out_block, out_map, n_tiles, g, weights, cols,
              final_g=None):
    n_chunks = D_FF // cols
    emit_bf16 = isinstance(weights[-1], tuple)
    x_seq_major = len(x_block) == 3
    out_seq_major = len(out_block) == 3
    body = functools.partial(_ffn_kernel, n_chunks=n_chunks, final_norm=final_g is not None,
                             x_seq_major=x_seq_major, out_seq_major=out_seq_major, emit_bf16=emit_bf16)
    scratch = [pltpu.VMEM((FFN_ROWS, D_MODEL), BF16), pltpu.VMEM((FFN_ROWS, D_MODEL), F32)]
    if x_seq_major:
        scratch.append(pltpu.VMEM((FFN_ROWS, D_MODEL), F32))

    const2 = lambda i, j: (0, 0)
    gate_bf = pl.BlockSpec((D_MODEL, cols), lambda i, j: (0, j))
    down_bf = pl.BlockSpec((cols, D_MODEL), lambda i, j: (j, 0))
    out_specs = pl.BlockSpec(out_block, lambda i, j: out_map(i))
    out_shapes = jax.ShapeDtypeStruct(out_shape, F32)
    if emit_bf16:
        w_gu, w_down, (layer, half) = weights
        w_specs = [
            pl.BlockSpec((None, None, D_MODEL, cols), lambda i, j: (layer, half, 0, j)),
            pl.BlockSpec((None, None, D_MODEL, cols), lambda i, j: (layer, half, 0, j + n_chunks)),
            pl.BlockSpec((None, None, cols, D_MODEL), lambda i, j: (layer, half, j, 0)),
        ]
        w_args = [w_gu, w_gu, w_down]
        out_specs = (out_specs, gate_bf, gate_bf, down_bf)
        out_shapes = (out_shapes,
                      jax.ShapeDtypeStruct((D_MODEL, D_FF), BF16),
                      jax.ShapeDtypeStruct((D_MODEL, D_FF), BF16),
                      jax.ShapeDtypeStruct((D_FF, D_MODEL), BF16))
    else:
        w_specs = [gate_bf, gate_bf, down_bf]
        w_args = list(weights)
    in_specs = [pl.BlockSpec(x_block, lambda i, j: x_map(i)), pl.BlockSpec((1, D_MODEL), const2)] + w_specs
    args = [x, g.reshape(1, D_MODEL)] + w_args
    if final_g is not None:
        in_specs.append(pl.BlockSpec((1, D_MODEL), const2))
        args.append(final_g.reshape(1, D_MODEL))
    res = pl.pallas_call(
        body,
        grid=(n_tiles, n_chunks),
        in_specs=in_specs,
        out_specs=out_specs,
        out_shape=out_shapes,
        scratch_shapes=scratch,
        compiler_params=pltpu.CompilerParams(
            dimension_semantics=("arbitrary", "arbitrary"), vmem_limit_bytes=VMEM_LIMIT),
        name="ffn",
    )(*args)
    return (res[0], tuple(res[1:])) if emit_bf16 else res


def _ffn_resident_kernel(x_ref, g_ref, wg_ref, wu_ref, wd_ref, *rest, cols, final_norm,
                         x_seq_major, out_seq_major):
    rest = list(rest)
    gf_ref = rest.pop(0) if final_norm else None
    o_ref = rest.pop(0)
    if x_seq_major:
        xbuf, xsem = rest[:2]
        del rest[:2]
    if out_seq_major:
        obuf, osem = rest[:2]
    i = pl.program_id(0)
    n_tiles = pl.num_programs(0)
    slot = i % 2
    tile_steps = FFN_ROWS // SUBLANES
    steps_per_block = FFN_ROW_BLOCK // SUBLANES

    def x_copies(tile, slot_):
        return [pltpu.make_async_copy(x_ref.at[s, pl.ds(tile * tile_steps, tile_steps), :],
                                      xbuf.at[slot_, :, s, :], xsem.at[slot_]) for s in range(SUBLANES)]

    def o_copies(tile, slot_):
        return [pltpu.make_async_copy(obuf.at[slot_, :, s, :],
                                      o_ref.at[s, pl.ds(tile * tile_steps, tile_steps), :],
                                      osem.at[slot_]) for s in range(SUBLANES)]

    if x_seq_major:
        @pl.when(i == 0)
        def _():
            for cp in x_copies(0, 0):
                cp.start()

        @pl.when(i + 1 < n_tiles)
        def _():
            for cp in x_copies(i + 1, 1 - slot):
                cp.start()

        for cp in x_copies(i, slot):
            cp.wait()
    if out_seq_major:
        @pl.when(i >= 2)
        def _():
            for cp in o_copies(i - 2, slot):
                cp.wait()

    for r in range(0, FFN_ROWS, FFN_ROW_BLOCK):
        t0 = r // SUBLANES
        if x_seq_major:
            x = xbuf[slot, t0:t0 + steps_per_block].reshape(FFN_ROW_BLOCK, D_MODEL)
        else:
            x = x_ref[r:r + FFN_ROW_BLOCK, :]
        h = _rmsnorm(x, g_ref[...]).astype(BF16)
        acc = None
        for c in range(0, D_FF, cols):
            gate = _dot(h, wg_ref[:, c:c + cols])
            up = _dot(h, wu_ref[:, c:c + cols])
            act = (jax.nn.silu(gate) * up).astype(BF16)
            part = _dot(act, wd_ref[c:c + cols, :])
            acc = part if acc is None else acc + part
        out = x + 0.5 * acc
        if final_norm:
            out = _rmsnorm(out, gf_ref[...])
        if out_seq_major:
            obuf[slot, t0:t0 + steps_per_block] = out.reshape(steps_per_block, SUBLANES, D_MODEL)
        else:
            o_ref[r:r + FFN_ROW_BLOCK, :] = out

    if out_seq_major:
        for cp in o_copies(i, slot):
            cp.start()

        @pl.when(i == n_tiles - 1)
        def _():
            for cp in o_copies(i, slot):
                cp.wait()

        @pl.when(jnp.logical_and(i == n_tiles - 1, i >= 1))
        def _():
            for cp in o_copies(i - 1, 1 - slot):
                cp.wait()


def _ffn_resident_call(x, out_shape, n_tiles, g, weights, cols, final_g=None):
    x_seq_major = x.ndim == 3
    out_seq_major = len(out_shape) == 3
    body = functools.partial(_ffn_resident_kernel, cols=cols, final_norm=final_g is not None,
                             x_seq_major=x_seq_major, out_seq_major=out_seq_major)
    whole = lambda a: pl.BlockSpec(a.shape, lambda i: (0,) * a.ndim)
    row_tile = pl.BlockSpec((FFN_ROWS, D_MODEL), lambda i: (i, 0))
    in_hbm = pl.BlockSpec(memory_space=pl.ANY)
    g2 = g.reshape(1, D_MODEL)
    in_specs = [in_hbm if x_seq_major else row_tile, whole(g2)] + [whole(w) for w in weights]
    args = [x, g2, *weights]
    if final_g is not None:
        gf2 = final_g.reshape(1, D_MODEL)
        in_specs.append(whole(gf2))
        args.append(gf2)
    reorder_scratch = [pltpu.VMEM((2, FFN_ROWS // SUBLANES, SUBLANES, D_MODEL), F32),
                       pltpu.SemaphoreType.DMA((2,))]
    scratch = (reorder_scratch if x_seq_major else []) + (reorder_scratch if out_seq_major else [])
    return pl.pallas_call(
        body,
        grid=(n_tiles,),
        in_specs=in_specs,
        out_specs=in_hbm if out_seq_major else row_tile,
        out_shape=jax.ShapeDtypeStruct(out_shape, F32),
        scratch_shapes=scratch,
        compiler_params=pltpu.CompilerParams(
            dimension_semantics=("arbitrary",), vmem_limit_bytes=VMEM_LIMIT),
        name="ffn_resident",
    )(*args)


def _ssm_prep_kernel(lam_re_ref, lam_im_ref, ldt_ref, bt_re_ref, bt_im_ref, ct_re_ref, ct_im_ref,
                     lb_re_ref, lb_im_ref, wb_re_ref, wb_im_ref, wc_re_ref, wc_im_ref,
                     sb_re, sb_im, sc_re, sc_im):
    lam_re = lam_re_ref[...]
    lam_im = lam_im_ref[...]
    dt = jnp.exp(ldt_ref[...])
    mag = jnp.exp(lam_re * dt)
    lb_re = mag * jnp.cos(lam_im * dt)
    lb_im = mag * jnp.sin(lam_im * dt)
    lb_re_ref[...] = lb_re
    lb_im_ref[...] = lb_im
    den = lam_re * lam_re + lam_im * lam_im
    nr = lb_re - 1.0
    ni = lb_im
    f_re = (nr * lam_re + ni * lam_im) / den
    f_im = (ni * lam_re - nr * lam_im) / den
    gpb = N_GROUPS // SSM_BLOCKS
    for k in range(SSM_BLOCKS):
        for scratch in (sb_re, sb_im, sc_re, sc_im):
            scratch[...] = jnp.zeros_like(scratch)
        for j in range(gpb):
            grp = k * gpb + j
            fr = f_re[grp:grp + 1, :]
            fi = f_im[grp:grp + 1, :]
            b_re = bt_re_ref[grp]
            b_im = bt_im_ref[grp]
            rows = slice(j * GROUP_SIZE, (j + 1) * GROUP_SIZE)
            cols = slice(j * P_STATE, (j + 1) * P_STATE)
            sb_re[rows, cols] = fr * b_re - fi * b_im
            sb_im[rows, cols] = fr * b_im + fi * b_re
            sc_re[cols, rows] = ct_re_ref[grp]
            sc_im[cols, rows] = ct_im_ref[grp]
        wb_re_ref[k] = sb_re[...].astype(BF16)
        wb_im_ref[k] = sb_im[...].astype(BF16)
        wc_re_ref[k] = sc_re[...].astype(BF16)
        wc_im_ref[k] = sc_im[...].astype(BF16)


def _ssm_prep(lam_re, lam_im, log_dt, b_re, b_im, c_re, c_im):
    swap = lambda a: jnp.swapaxes(a, 1, 2)
    lb = jax.ShapeDtypeStruct((N_GROUPS, P_STATE), F32)
    wb = jax.ShapeDtypeStruct((SSM_BLOCKS, SSM_BLOCK_CH, SSM_BLOCK_ST), BF16)
    wc = jax.ShapeDtypeStruct((SSM_BLOCKS, SSM_BLOCK_ST, SSM_BLOCK_CH), BF16)
    return pl.pallas_call(
        _ssm_prep_kernel,
        out_shape=(lb, lb, wb, wb, wc, wc),
        scratch_shapes=[pltpu.VMEM((SSM_BLOCK_CH, SSM_BLOCK_ST), F32)] * 2
        + [pltpu.VMEM((SSM_BLOCK_ST, SSM_BLOCK_CH), F32)] * 2,
        name="ssm_prep",
    )(lam_re, lam_im, log_dt.reshape(N_GROUPS, 1), swap(b_re), swap(b_im), swap(c_re), swap(c_im))


def _s5_kernel(x_ref, g_ref, s0_re_ref, s0_im_ref, lb_re_ref, lb_im_ref,
               wb_re_ref, wb_im_ref, wc_re_ref, wc_im_ref, d_ref, wglu_ref,
               o_ref, new_re_ref, new_im_ref,
               st_re, st_im, bu_re, bu_im, *, steps, sub_steps, n_seq, carry_over_grid):
    sub_rows = sub_steps * n_seq
    if carry_over_grid:
        @pl.when(pl.program_id(0) == 0)
        def _():
            st_re[...] = s0_re_ref[...]
            st_im[...] = s0_im_ref[...]
    else:
        st_re[...] = s0_re_ref[...]
        st_im[...] = s0_im_ref[...]

    for s in range(steps // sub_steps):
        t0 = s * sub_steps
        if len(x_ref.shape) == 3:
            x = x_ref[t0:t0 + sub_steps].reshape(sub_rows, D_MODEL)
        else:
            x = x_ref[t0 * n_seq:t0 * n_seq + sub_rows, :]
        u = _rmsnorm(x, g_ref[...])
        ub = u.astype(BF16)

        def project_in(k):
            uk = ub[:, k * SSM_BLOCK_CH:(k + 1) * SSM_BLOCK_CH]
            bu_re[k % 2] = _dot(uk, wb_re_ref[k])
            bu_im[k % 2] = _dot(uk, wb_im_ref[k])

        def scan_block(k):
            b_re, b_im = bu_re.at[k % 2], bu_im.at[k % 2]
            for c in range(SSM_BLOCK_ST // SCAN_LANES):
                lo = c * SCAN_LANES
                glo = k * SSM_BLOCK_ST + lo
                lr = jnp.broadcast_to(lb_re_ref[:, glo:glo + SCAN_LANES], (SUBLANES, SCAN_LANES))
                li = jnp.broadcast_to(lb_im_ref[:, glo:glo + SCAN_LANES], (SUBLANES, SCAN_LANES))
                for r_state in range(0, n_seq, SUBLANES):
                    hr = st_re[r_state:r_state + SUBLANES, glo:glo + SCAN_LANES]
                    hi = st_im[r_state:r_state + SUBLANES, glo:glo + SCAN_LANES]
                    for t in range(sub_steps):
                        r = t * n_seq + r_state
                        br = b_re[r:r + SUBLANES, lo:lo + SCAN_LANES]
                        bi = b_im[r:r + SUBLANES, lo:lo + SCAN_LANES]
                        hr, hi = lr * hr - li * hi + br, lr * hi + li * hr + bi
                        b_re[r:r + SUBLANES, lo:lo + SCAN_LANES] = hr
                        b_im[r:r + SUBLANES, lo:lo + SCAN_LANES] = hi
                    st_re[r_state:r_state + SUBLANES, glo:glo + SCAN_LANES] = hr
                    st_im[r_state:r_state + SUBLANES, glo:glo + SCAN_LANES] = hi

        ys = []
        project_in(0)
        for k in range(SSM_BLOCKS):
            if k + 1 < SSM_BLOCKS:
                project_in(k + 1)
            scan_block(k)
            hr = bu_re[k % 2].astype(BF16)
            hi = bu_im[k % 2].astype(BF16)
            ys.append(_dot(hr, wc_re_ref[k]) - _dot(hi, wc_im_ref[k]))
        y = jnp.concatenate(ys, axis=-1) + d_ref[...] * u
        z = _dot(jax.nn.gelu(y).astype(BF16), wglu_ref[...])
        m = z[:, :D_MODEL] * jax.nn.sigmoid(z[:, D_MODEL:])
        if len(o_ref.shape) == 3:
            o_ref[t0:t0 + sub_steps] = (x + m).reshape(sub_steps, n_seq, D_MODEL)
        else:
            o_ref[t0 * n_seq:t0 * n_seq + sub_rows, :] = x + m
    new_re_ref[...] = st_re[...]
    new_im_ref[...] = st_im[...]


def _s5_call(x, x_block, x_map, grid, s0_re, s0_im, state_map, g, lb_re, lb_im,
             wb_re, wb_im, wc_re, wc_im, d_skip, wglu, *, steps, sub_steps, n_seq, carry_over_grid):
    sub_rows = sub_steps * n_seq
    n_state = s0_re.shape[0]
    whole = lambda a: pl.BlockSpec(a.shape, lambda i: (0,) * a.ndim)
    state_spec = pl.BlockSpec((n_seq, S_DIM), state_map)
    g2 = g.reshape(1, D_MODEL)
    d2 = d_skip.reshape(1, D_MODEL)
    lbr = lb_re.reshape(1, S_DIM)
    lbi = lb_im.reshape(1, S_DIM)
    body = functools.partial(_s5_kernel, steps=steps, sub_steps=sub_steps, n_seq=n_seq,
                             carry_over_grid=carry_over_grid)
    return pl.pallas_call(
        body,
        grid=grid,
        in_specs=[pl.BlockSpec(x_block, x_map), whole(g2), state_spec, state_spec,
                  whole(lbr), whole(lbi), whole(wb_re), whole(wb_im), whole(wc_re), whole(wc_im),
                  whole(d2), whole(wglu)],
        out_specs=(pl.BlockSpec(x_block, x_map), state_spec, state_spec),
        out_shape=(jax.ShapeDtypeStruct(x.shape, F32),
                   jax.ShapeDtypeStruct((n_state, S_DIM), F32),
                   jax.ShapeDtypeStruct((n_state, S_DIM), F32)),
        scratch_shapes=[pltpu.VMEM((n_seq, S_DIM), F32), pltpu.VMEM((n_seq, S_DIM), F32),
                        pltpu.VMEM((2, sub_rows, SSM_BLOCK_ST), F32),
                        pltpu.VMEM((2, sub_rows, SSM_BLOCK_ST), F32)],
        compiler_params=pltpu.CompilerParams(
            dimension_semantics=("arbitrary",), vmem_limit_bytes=VMEM_LIMIT),
        name="s5_mixer",
    )(x, g2, s0_re, s0_im, lbr, lbi, wb_re, wb_im, wc_re, wc_im, d2, wglu)


def _conv_kernel(x_ref, g_ref, buf0_ref, win_ref, cw_ref, wout_ref, o_ref, newbuf_ref, zp,
                 *, steps, sub_steps, n_seq, carry_over_grid):
    rows = steps * n_seq
    halo = (CONV_W - 1) * n_seq

    def load_buf0():
        for k in range(CONV_W - 1):
            zp[k * n_seq:(k + 1) * n_seq, :] = buf0_ref[:, k, :]

    if carry_over_grid:
        pl.when(pl.program_id(0) == 0)(load_buf0)
    else:
        load_buf0()

    sub_rows = sub_steps * n_seq
    for s in range(steps // sub_steps):
        t0 = s * sub_steps
        r0 = t0 * n_seq
        if len(x_ref.shape) == 3:
            x = x_ref[t0:t0 + sub_steps].reshape(sub_rows, D_MODEL)
        else:
            x = x_ref[r0:r0 + sub_rows, :]
        h = _rmsnorm(x, g_ref[...]).astype(BF16)
        p = _dot(h, win_ref[...])
        gb = p[:, :D_MODEL]
        zp[halo + r0:halo + r0 + sub_rows, :] = p[:, D_MODEL:2 * D_MODEL] * p[:, 2 * D_MODEL:]
        conv = cw_ref[0:1, :] * zp[r0:r0 + sub_rows, :]
        for k in range(1, CONV_W):
            conv = conv + cw_ref[k:k + 1, :] * zp[r0 + k * n_seq:r0 + k * n_seq + sub_rows, :]
        m = _dot((gb * conv).astype(BF16), wout_ref[...])
        if len(o_ref.shape) == 3:
            o_ref[t0:t0 + sub_steps] = (x + m).reshape(sub_steps, n_seq, D_MODEL)
        else:
            o_ref[r0:r0 + sub_rows, :] = x + m
    tail = zp[rows:rows + halo, :]
    zp[0:halo, :] = tail
    for k in range(CONV_W - 1):
        newbuf_ref[:, k, :] = tail[k * n_seq:(k + 1) * n_seq, :]


def _conv_call(x, x_block, x_map, grid, buf0, buf_map, g, w_in, cw, w_out,
               *, steps, sub_steps, n_seq, carry_over_grid):
    rows = steps * n_seq
    halo = (CONV_W - 1) * n_seq
    whole = lambda a: pl.BlockSpec(a.shape, lambda i: (0,) * a.ndim)
    g2 = g.reshape(1, D_MODEL)
    buf_block = (n_seq, CONV_W - 1, D_MODEL)
    body = functools.partial(_conv_kernel, steps=steps, sub_steps=sub_steps, n_seq=n_seq,
                             carry_over_grid=carry_over_grid)
    return pl.pallas_call(
        body,
        grid=grid,
        in_specs=[pl.BlockSpec(x_block, x_map), whole(g2), pl.BlockSpec(buf_block, buf_map),
                  whole(w_in), whole(cw), whole(w_out)],
        out_specs=(pl.BlockSpec(x_block, x_map), pl.BlockSpec(buf_block, buf_map)),
        out_shape=(jax.ShapeDtypeStruct(x.shape, F32), jax.ShapeDtypeStruct(buf0.shape, F32)),
        scratch_shapes=[pltpu.VMEM((halo + rows, D_MODEL), F32)],
        compiler_params=pltpu.CompilerParams(
            dimension_semantics=("arbitrary",), vmem_limit_bytes=VMEM_LIMIT),
        name="conv_mixer",
    )(x, g2, buf0, w_in, cw, w_out)


S5_PROMPT_STEPS = 128
S5_PROMPT_SUB_STEPS = 32
CONV_PROMPT_STEPS = 256
CONV_PROMPT_SUB_STEPS = 32
SAMPLE_SEQ_BLOCK = 64


def kernel(x_prompt, x_sample, state_ssm_re, state_ssm_im, cache_conv, norm_g, final_norm_g, ffn_w_gate_up, ffn_w_down, ssm_lam_re, ssm_lam_im, ssm_log_dt, ssm_b_re, ssm_b_im, ssm_c_re, ssm_c_im, ssm_d, ssm_w_glu, conv_w_in, conv_w, conv_w_out):
    nb_p, len_p, _ = x_prompt.shape
    nb_s, len_s, _ = x_sample.shape
    rows_p = nb_p * len_p
    rows_s = nb_s * len_s
    assert nb_p == SUBLANES and len_p % FFN_ROWS == 0 and rows_s == FFN_ROWS
    assert nb_s % SAMPLE_SEQ_BLOCK == 0

    lb_re, lb_im, wb_re, wb_im, wc_re, wc_im = _ssm_prep(
        ssm_lam_re[0], ssm_lam_im[0], ssm_log_dt[0], ssm_b_re[0], ssm_b_im[0], ssm_c_re[0], ssm_c_im[0])
    wglu = ssm_w_glu[0].astype(BF16)
    w_in = conv_w_in[0].astype(BF16)
    w_out = conv_w_out[0].astype(BF16)

    row_tiles_p = rows_p // FFN_ROWS
    flat_block = (FFN_ROWS, D_MODEL)
    flat_map = lambda i: (i, 0)

    whole_s = (x_sample.shape, lambda i: (0, 0, 0))
    flat_s = (flat_block, flat_map)

    def ffn_sample(xs, layer, half, x_spec=flat_s, out_spec=flat_s, final_g=None):
        out_shape = x_sample.shape if out_spec is whole_s else (rows_s, D_MODEL)
        return _ffn_call(xs, *x_spec, out_shape, *out_spec, 1,
                         norm_g[layer, 2 * half], (ffn_w_gate_up, ffn_w_down, (layer, half)),
                         FFN_COLS_F32, final_g=final_g)

    xs, w_bf_00 = ffn_sample(x_sample, 0, 0, x_spec=whole_s)
    seq_block = (len_s, SAMPLE_SEQ_BLOCK, D_MODEL)
    seq_map = lambda i: (0, i, 0)
    seq_grid = (nb_s // SAMPLE_SEQ_BLOCK,)
    xs, sre_s, sim_s = _s5_call(
        xs.reshape(len_s, nb_s, D_MODEL), seq_block, seq_map, seq_grid,
        state_ssm_re[0].reshape(nb_s, S_DIM), state_ssm_im[0].reshape(nb_s, S_DIM), lambda i: (i, 0),
        norm_g[0, 1], lb_re, lb_im, wb_re, wb_im, wc_re, wc_im, ssm_d[0], wglu,
        steps=len_s, sub_steps=len_s, n_seq=SAMPLE_SEQ_BLOCK, carry_over_grid=False)
    xs = xs.reshape(rows_s, D_MODEL)
    xs, w_bf_01 = ffn_sample(xs, 0, 1)
    xs, w_bf_10 = ffn_sample(xs, 1, 0)
    xs, buf_s = _conv_call(
        xs.reshape(len_s, nb_s, D_MODEL), seq_block, seq_map, seq_grid,
        cache_conv[0], lambda i: (i, 0, 0), norm_g[1, 1], w_in, conv_w[0], w_out,
        steps=len_s, sub_steps=len_s, n_seq=SAMPLE_SEQ_BLOCK, carry_over_grid=False)
    xs = xs.reshape(rows_s, D_MODEL)
    y_sample, w_bf_11 = ffn_sample(xs, 1, 1, out_spec=whole_s, final_g=final_norm_g)

    def ffn(x, out_shape, g, w_bf, final_g=None):
        return _ffn_resident_call(x, out_shape, row_tiles_p, g, w_bf, FFN_COLS, final_g=final_g)

    xp = ffn(x_prompt, (rows_p, D_MODEL), norm_g[0, 0], w_bf_00)
    zero_state = jnp.zeros((nb_p, S_DIM), F32)
    xp, sre_p, sim_p = _s5_call(
        xp, (S5_PROMPT_STEPS * nb_p, D_MODEL), flat_map, (len_p // S5_PROMPT_STEPS,),
        zero_state, zero_state, lambda i: (0, 0), norm_g[0, 1], lb_re, lb_im,
        wb_re, wb_im, wc_re, wc_im, ssm_d[0], wglu,
        steps=S5_PROMPT_STEPS, sub_steps=S5_PROMPT_SUB_STEPS, n_seq=nb_p, carry_over_grid=True)
    xp = ffn(xp, (rows_p, D_MODEL), norm_g[0, 2], w_bf_01)
    xp = ffn(xp, (rows_p, D_MODEL), norm_g[1, 0], w_bf_10)
    xp, buf_p = _conv_call(
        xp, (CONV_PROMPT_STEPS * nb_p, D_MODEL), flat_map, (len_p // CONV_PROMPT_STEPS,),
        jnp.zeros((nb_p, CONV_W - 1, D_MODEL), F32), lambda i: (0, 0, 0),
        norm_g[1, 1], w_in, conv_w[0], w_out,
        steps=CONV_PROMPT_STEPS, sub_steps=CONV_PROMPT_SUB_STEPS, n_seq=nb_p, carry_over_grid=True)
    y_prompt = ffn(xp, x_prompt.shape, norm_g[1, 2], w_bf_11, final_g=final_norm_g)

    state4 = lambda s: s.reshape(1, -1, N_GROUPS, P_STATE)
    return (y_prompt, y_sample, state4(sre_p), state4(sim_p), buf_p[None],
            state4(sre_s), state4(sim_s), buf_s[None])
```

```python
import functools

import jax
import jax.numpy as jnp
from jax import lax
from jax.experimental import pallas as pl
from jax.experimental.pallas import tpu as pltpu

F32 = jnp.float32
BF16 = jnp.bfloat16

D_MODEL = 1024
D_FF = 4 * D_MODEL
GROUP_SIZE = 16
N_GROUPS = D_MODEL // GROUP_SIZE
P_STATE = 64
S_DIM = N_GROUPS * P_STATE
CONV_W = 3
EPS = 1e-6

SUBLANES = 8
FFN_ROWS = 1024
FFN_COLS = 1024
FFN_COLS_F32 = 512
FFN_ROW_BLOCK = 256
SSM_BLOCKS = 4
SSM_BLOCK_CH = D_MODEL // SSM_BLOCKS
SSM_BLOCK_ST = S_DIM // SSM_BLOCKS
SCAN_LANES = 256
VMEM_LIMIT = 56 * 1024 * 1024


def _rmsnorm(x, g):
    return x * lax.rsqrt(jnp.mean(x * x, axis=-1, keepdims=True) + EPS) * g


def _dot(a, b):
    return jnp.dot(a, b, preferred_element_type=F32)


def _ffn_kernel(x_ref, g_ref, wg_ref, wu_ref, wd_ref, *rest, n_chunks, final_norm,
                x_seq_major, out_seq_major, emit_bf16):
    rest = list(rest)
    gf_ref = rest.pop(0) if final_norm else None
    o_ref = rest.pop(0)
    if emit_bf16:
        wg_bf, wu_bf, wd_bf = rest[:3]
        del rest[:3]
    h_ref, acc_ref = rest[:2]
    xt_ref = rest[2] if x_seq_major else x_ref
    j = pl.program_id(1)

    @pl.when(j == 0)
    def _():
        if x_seq_major:
            n_seq = x_ref.shape[0]
            for t in range(x_ref.shape[1]):
                xt_ref[t * n_seq:(t + 1) * n_seq, :] = x_ref[:, t, :]
        h_ref[...] = _rmsnorm(xt_ref[...], g_ref[...]).astype(BF16)
        acc_ref[...] = jnp.zeros_like(acc_ref)

    if emit_bf16:
        wg_bf[...] = wg_ref[...].astype(BF16)
        wu_bf[...] = wu_ref[...].astype(BF16)
        wd_bf[...] = wd_ref[...].astype(BF16)
        wg_ref, wu_ref, wd_ref = wg_bf, wu_bf, wd_bf

    for r in range(0, FFN_ROWS, FFN_ROW_BLOCK):
        h = h_ref[r:r + FFN_ROW_BLOCK, :]
        gate = _dot(h, wg_ref[...])
        up = _dot(h, wu_ref[...])
        act = (jax.nn.silu(gate) * up).astype(BF16)
        acc_ref[r:r + FFN_ROW_BLOCK, :] += _dot(act, wd_ref[...])

    @pl.when(j == n_chunks - 1)
    def _():
        out = xt_ref[...] + 0.5 * acc_ref[...]
        if final_norm:
            out = _rmsnorm(out, gf_ref[...])
        if out_seq_major:
            n_seq = o_ref.shape[0]
            for t in range(o_ref.shape[1]):
                o_ref[:, t, :] = out[t * n_seq:(t + 1) * n_seq, :]
        else:
            o_ref[...] = out


def _ffn_call(x, x_block, x_map, out_shape, out_block, out_map, n_tiles, g, weights, cols,
              final_g=None):
    n_chunks = D_FF // cols
    emit_bf16 = isinstance(weights[-1], tuple)
    x_seq_major = len(x_block) == 3
    out_seq_major = len(out_block) == 3
    body = functools.partial(_ffn_kernel, n_chunks=n_chunks, final_norm=final_g is not None,
                             x_seq_major=x_seq_major, out_seq_major=out_seq_major, emit_bf16=emit_bf16)
    scratch = [pltpu.VMEM((FFN_ROWS, D_MODEL), BF16), pltpu.VMEM((FFN_ROWS, D_MODEL), F32)]
    if x_seq_major:
        scratch.append(pltpu.VMEM((FFN_ROWS, D_MODEL), F32))

    const2 = lambda i, j: (0, 0)
    gate_bf = pl.BlockSpec((D_MODEL, cols), lambda i, j: (0, j))
    down_bf = pl.BlockSpec((cols, D_MODEL), lambda i, j: (j, 0))
    out_specs = pl.BlockSpec(out_block, lambda i, j: out_map(i))
    out_shapes = jax.ShapeDtypeStruct(out_shape, F32)
    if emit_bf16:
        w_gu, w_down, (layer, half) = weights
        w_specs = [
            pl.BlockSpec((None, None, D_MODEL, cols), lambda i, j: (layer, half, 0, j)),
            pl.BlockSpec((None, None, D_MODEL, cols), lambda i, j: (layer, half, 0, j + n_chunks)),
            pl.BlockSpec((None, None, cols, D_MODEL), lambda i, j: (layer, half, j, 0)),
        ]
        w_args = [w_gu, w_gu, w_down]
        out_specs = (out_specs, gate_bf, gate_bf, down_bf)
        out_shapes = (out_shapes,
                      jax.ShapeDtypeStruct((D_MODEL, D_FF), BF16),
                      jax.ShapeDtypeStruct((D_MODEL, D_FF), BF16),
                      jax.ShapeDtypeStruct((D_FF, D_MODEL), BF16))
    else:
        w_specs = [gate_bf, gate_bf, down_bf]
        w_args = list(weights)
    in_specs = [pl.BlockSpec(x_block, lambda i, j: x_map(i)), pl.BlockSpec((1, D_MODEL), const2)] + w_specs
    args = [x, g.reshape(1, D_MODEL)] + w_args
    if final_g is not None:
        in_specs.append(pl.BlockSpec((1, D_MODEL), const2))
        args.append(final_g.reshape(1, D_MODEL))
    res = pl.pallas_call(
        body,
        grid=(n_tiles, n_chunks),
        in_specs=in_specs,
        out_specs=out_specs,
        out_shape=out_shapes,
        scratch_shapes=scratch,
        compiler_params=pltpu.CompilerParams(
            dimension_semantics=("arbitrary", "arbitrary"), vmem_limit_bytes=VMEM_LIMIT),
        name="ffn",
    )(*args)
    return (res[0], tuple(res[1:])) if emit_bf16 else res


def _ffn_resident_kernel(x_ref, g_ref, wg_ref, wu_ref, wd_ref, *rest, cols, final_norm,
                         x_seq_major, out_seq_major):
    rest = list(rest)
    gf_ref = rest.pop(0) if final_norm else None
    o_ref = rest.pop(0)
    if x_seq_major:
        xbuf, xsem = rest[:2]
        del rest[:2]
    if out_seq_major:
        obuf, osem = rest[:2]
    i = pl.program_id(0)
    n_tiles = pl.num_programs(0)
    slot = i % 2
    tile_steps = FFN_ROWS // SUBLANES
    steps_per_block = FFN_ROW_BLOCK // SUBLANES

    def x_copies(tile, slot_):
        return [pltpu.make_async_copy(x_ref.at[s, pl.ds(tile * tile_steps, tile_steps), :],
                                      xbuf.at[slot_, :, s, :], xsem.at[slot_]) for s in range(SUBLANES)]

    def o_copies(tile, slot_):
        return [pltpu.make_async_copy(obuf.at[slot_, :, s, :],
                                      o_ref.at[s, pl.ds(tile * tile_steps, tile_steps), :],
                                      osem.at[slot_]) for s in range(SUBLANES)]

    if x_seq_major:
        @pl.when(i == 0)
        def _():
            for cp in x_copies(0, 0):
                cp.start()

        @pl.when(i + 1 < n_tiles)
        def _():
            for cp in x_copies(i + 1, 1 - slot):
                cp.start()

        for cp in x_copies(i, slot):
            cp.wait()
    if out_seq_major:
        @pl.when(i >= 2)
        def _():
            for cp in o_copies(i - 2, slot):
                cp.wait()

    for r in range(0, FFN_ROWS, FFN_ROW_BLOCK):
        t0 = r // SUBLANES
        if x_seq_major:
            x = xbuf[slot, t0:t0 + steps_per_block].reshape(FFN_ROW_BLOCK, D_MODEL)
        else:
            x = x_ref[r:r + FFN_ROW_BLOCK, :]
        h = _rmsnorm(x, g_ref[...]).astype(BF16)
        acc = None
        for c in range(0, D_FF, cols):
            gate = _dot(h, wg_ref[:, c:c + cols])
            up = _dot(h, wu_ref[:, c:c + cols])
            act = (jax.nn.silu(gate) * up).astype(BF16)
            part = _dot(act, wd_ref[c:c + cols, :])
            acc = part if acc is None else acc + part
        out = x + 0.5 * acc
        if final_norm:
            out = _rmsnorm(out, gf_ref[...])
        if out_seq_major:
            obuf[slot, t0:t0 + steps_per_block] = out.reshape(steps_per_block, SUBLANES, D_MODEL)
        else:
            o_ref[r:r + FFN_ROW_BLOCK, :] = out

    if out_seq_major:
        for cp in o_copies(i, slot):
            cp.start()

        @pl.when(i == n_tiles - 1)
        def _():
            for cp in o_copies(i, slot):
                cp.wait()

        @pl.when(jnp.logical_and(i == n_tiles - 1, i >= 1))
        def _():
            for cp in o_copies(i - 1, 1 - slot):
                cp.wait()


def _ffn_resident_call(x, out_shape, n_tiles, g, weights, cols, final_g=None):
    x_seq_major = x.ndim == 3
    out_seq_major = len(out_shape) == 3
    body = functools.partial(_ffn_resident_kernel, cols=cols, final_norm=final_g is not None,
                             x_seq_major=x_seq_major, out_seq_major=out_seq_major)
    whole = lambda a: pl.BlockSpec(a.shape, lambda i: (0,) * a.ndim)
    row_tile = pl.BlockSpec((FFN_ROWS, D_MODEL), lambda i: (i, 0))
    in_hbm = pl.BlockSpec(memory_space=pl.ANY)
    g2 = g.reshape(1, D_MODEL)
    in_specs = [in_hbm if x_seq_major else row_tile, whole(g2)] + [whole(w) for w in weights]
    args = [x, g2, *weights]
    if final_g is not None:
        gf2 = final_g.reshape(1, D_MODEL)
        in_specs.append(whole(gf2))
        args.append(gf2)
    reorder_scratch = [pltpu.VMEM((2, FFN_ROWS // SUBLANES, SUBLANES, D_MODEL), F32),
                       pltpu.SemaphoreType.DMA((2,))]
    scratch = (reorder_scratch if x_seq_major else []) + (reorder_scratch if out_seq_major else [])
    return pl.pallas_call(
        body,
        grid=(n_tiles,),
        in_specs=in_specs,
        out_specs=in_hbm if out_seq_major else row_tile,
        out_shape=jax.ShapeDtypeStruct(out_shape, F32),
        scratch_shapes=scratch,
        compiler_params=pltpu.CompilerParams(
            dimension_semantics=("arbitrary",), vmem_limit_bytes=VMEM_LIMIT),
        name="ffn_resident",
    )(*args)


def _ssm_prep_kernel(lam_re_ref, lam_im_ref, ldt_ref, bt_re_ref, bt_im_ref, ct_re_ref, ct_im_ref,
                     lb_re_ref, lb_im_ref, wb_re_ref, wb_im_ref, wc_re_ref, wc_im_ref,
                     sb_re, sb_im, sc_re, sc_im):
    lam_re = lam_re_ref[...]
    lam_im = lam_im_ref[...]
    dt = jnp.exp(ldt_ref[...])
    mag = jnp.exp(lam_re * dt)
    lb_re = mag * jnp.cos(lam_im * dt)
    lb_im = mag * jnp.sin(lam_im * dt)
    lb_re_ref[...] = lb_re
    lb_im_ref[...] = lb_im
    den = lam_re * lam_re + lam_im * lam_im
    nr = lb_re - 1.0
    ni = lb_im
    f_re = (nr * lam_re + ni * lam_im) / den
    f_im = (ni * lam_re - nr * lam_im) / den
    gpb = N_GROUPS // SSM_BLOCKS
    for k in range(SSM_BLOCKS):
        for scratch in (sb_re, sb_im, sc_re, sc_im):
            scratch[...] = jnp.zeros_like(scratch)
        for j in range(gpb):
            grp = k * gpb + j
            fr = f_re[grp:grp + 1, :]
            fi = f_im[grp:grp + 1, :]
            b_re = bt_re_ref[grp]
            b_im = bt_im_ref[grp]
            rows = slice(j * GROUP_SIZE, (j + 1) * GROUP_SIZE)
            cols = slice(j * P_STATE, (j + 1) * P_STATE)
            sb_re[rows, cols] = fr * b_re - fi * b_im
            sb_im[rows, cols] = fr * b_im + fi * b_re
            sc_re[cols, rows] = ct_re_ref[grp]
            sc_im[cols, rows] = ct_im_ref[grp]
        wb_re_ref[k] = sb_re[...].astype(BF16)
        wb_im_ref[k] = sb_im[...].astype(BF16)
        wc_re_ref[k] = sc_re[...].astype(BF16)
        wc_im_ref[k] = sc_im[...].astype(BF16)


def _ssm_prep(lam_re, lam_im, log_dt, b_re, b_im, c_re, c_im):
    swap = lambda a: jnp.swapaxes(a, 1, 2)
    lb = jax.ShapeDtypeStruct((N_GROUPS, P_STATE), F32)
    wb = jax.ShapeDtypeStruct((SSM_BLOCKS, SSM_BLOCK_CH, SSM_BLOCK_ST), BF16)
    wc = jax.ShapeDtypeStruct((SSM_BLOCKS, SSM_BLOCK_ST, SSM_BLOCK_CH), BF16)
    return pl.pallas_call(
        _ssm_prep_kernel,
        out_shape=(lb, lb, wb, wb, wc, wc),
        scratch_shapes=[pltpu.VMEM((SSM_BLOCK_CH, SSM_BLOCK_ST), F32)] * 2
        + [pltpu.VMEM((SSM_BLOCK_ST, SSM_BLOCK_CH), F32)] * 2,
        name="ssm_prep",
    )(lam_re, lam_im, log_dt.reshape(N_GROUPS, 1), swap(b_re), swap(b_im), swap(c_re), swap(c_im))


def _s5_kernel(x_ref, g_ref, s0_re_ref, s0_im_ref, lb_re_ref, lb_im_ref,
               wb_re_ref, wb_im_ref, wc_re_ref, wc_im_ref, d_ref, wglu_ref,
               o_ref, new_re_ref, new_im_ref,
               st_re, st_im, bu_re, bu_im, *, steps, sub_steps, n_seq, carry_over_grid):
    sub_rows = sub_steps * n_seq
    if carry_over_grid:
        @pl.when(pl.program_id(0) == 0)
        def _():
            st_re[...] = s0_re_ref[...]
            st_im[...] = s0_im_ref[...]
    else:
        st_re[...] = s0_re_ref[...]
        st_im[...] = s0_im_ref[...]

    for s in range(steps // sub_steps):
        t0 = s * sub_steps
        if len(x_ref.shape) == 3:
            x = x_ref[t0:t0 + sub_steps].reshape(sub_rows, D_MODEL)
        else:
            x = x_ref[t0 * n_seq:t0 * n_seq + sub_rows, :]
        u = _rmsnorm(x, g_ref[...])
        ub = u.astype(BF16)

        def project_in(k):
            uk = ub[:, k * SSM_BLOCK_CH:(k + 1) * SSM_BLOCK_CH]
            bu_re[k % 2] = _dot(uk, wb_re_ref[k])
            bu_im[k % 2] = _dot(uk, wb_im_ref[k])

        def scan_block(k):
            b_re, b_im = bu_re.at[k % 2], bu_im.at[k % 2]
            for c in range(SSM_BLOCK_ST // SCAN_LANES):
                lo = c * SCAN_LANES
                glo = k * SSM_BLOCK_ST + lo
                lr = jnp.broadcast_to(lb_re_ref[:, glo:glo + SCAN_LANES], (SUBLANES, SCAN_LANES))
                li = jnp.broadcast_to(lb_im_ref[:, glo:glo + SCAN_LANES], (SUBLANES, SCAN_LANES))
                for r_state in range(0, n_seq, SUBLANES):
                    hr = st_re[r_state:r_state + SUBLANES, glo:glo + SCAN_LANES]
                    hi = st_im[r_state:r_state + SUBLANES, glo:glo + SCAN_LANES]
                    for t in range(sub_steps):
                        r = t * n_seq + r_state
                        br = b_re[r:r + SUBLANES, lo:lo + SCAN_LANES]
                        bi = b_im[r:r + SUBLANES, lo:lo + SCAN_LANES]
                        hr, hi = lr * hr - li * hi + br, lr * hi + li * hr + bi
                        b_re[r:r + SUBLANES, lo:lo + SCAN_LANES] = hr
                        b_im[r:r + SUBLANES, lo:lo + SCAN_LANES] = hi
                    st_re[r_state:r_state + SUBLANES, glo:glo + SCAN_LANES] = hr
                    st_im[r_state:r_state + SUBLANES, glo:glo + SCAN_LANES] = hi

        ys = []
        project_in(0)
        for k in range(SSM_BLOCKS):
            if k + 1 < SSM_BLOCKS:
                project_in(k + 1)
            scan_block(k)
            hr = bu_re[k % 2].astype(BF16)
            hi = bu_im[k % 2].astype(BF16)
            ys.append(_dot(hr, wc_re_ref[k]) - _dot(hi, wc_im_ref[k]))
        y = jnp.concatenate(ys, axis=-1) + d_ref[...] * u
        z = _dot(jax.nn.gelu(y).astype(BF16), wglu_ref[...])
        m = z[:, :D_MODEL] * jax.nn.sigmoid(z[:, D_MODEL:])
        if len(o_ref.shape) == 3:
            o_ref[t0:t0 + sub_steps] = (x + m).reshape(sub_steps, n_seq, D_MODEL)
        else:
            o_ref[t0 * n_seq:t0 * n_seq + sub_rows, :] = x + m
    new_re_ref[...] = st_re[...]
    new_im_ref[...] = st_im[...]


def _s5_call(x, x_block, x_map, grid, s0_re, s0_im, state_map, g, lb_re, lb_im,
             wb_re, wb_im, wc_re, wc_im, d_skip, wglu, *, steps, sub_steps, n_seq, carry_over_grid):
    sub_rows = sub_steps * n_seq
    n_state = s0_re.shape[0]
    whole = lambda a: pl.BlockSpec(a.shape, lambda i: (0,) * a.ndim)
    state_spec = pl.BlockSpec((n_seq, S_DIM), state_map)
    g2 = g.reshape(1, D_MODEL)
    d2 = d_skip.reshape(1, D_MODEL)
    lbr = lb_re.reshape(1, S_DIM)
    lbi = lb_im.reshape(1, S_DIM)
    body = functools.partial(_s5_kernel, steps=steps, sub_steps=sub_steps, n_seq=n_seq,
                             carry_over_grid=carry_over_grid)
    return pl.pallas_call(
        body,
        grid=grid,
        in_specs=[pl.BlockSpec(x_block, x_map), whole(g2), state_spec, state_spec,
                  whole(lbr), whole(lbi), whole(wb_re), whole(wb_im), whole(wc_re), whole(wc_im),
                  whole(d2), whole(wglu)],
        out_specs=(pl.BlockSpec(x_block, x_map), state_spec, state_spec),
        out_shape=(jax.ShapeDtypeStruct(x.shape, F32),
                   jax.ShapeDtypeStruct((n_state, S_DIM), F32),
                   jax.ShapeDtypeStruct((n_state, S_DIM), F32)),
        scratch_shapes=[pltpu.VMEM((n_seq, S_DIM), F32), pltpu.VMEM((n_seq, S_DIM), F32),
                        pltpu.VMEM((2, sub_rows, SSM_BLOCK_ST), F32),
                        pltpu.VMEM((2, sub_rows, SSM_BLOCK_ST), F32)],
        compiler_params=pltpu.CompilerParams(
            dimension_semantics=("arbitrary",), vmem_limit_bytes=VMEM_LIMIT,
            allow_input_fusion=[False] * 11 + [True]),
        name="s5_mixer",
    )(x, g2, s0_re, s0_im, lbr, lbi, wb_re, wb_im, wc_re, wc_im, d2, wglu)


def _conv_kernel(x_ref, g_ref, buf0_ref, win_ref, cw_ref, wout_ref, o_ref, newbuf_ref, zp,
                 *, steps, sub_steps, n_seq, carry_over_grid):
    rows = steps * n_seq
    halo = (CONV_W - 1) * n_seq

    def load_buf0():
        for k in range(CONV_W - 1):
            zp[k * n_seq:(k + 1) * n_seq, :] = buf0_ref[:, k, :]

    if carry_over_grid:
        pl.when(pl.program_id(0) == 0)(load_buf0)
    else:
        load_buf0()

    sub_rows = sub_steps * n_seq
    for s in range(steps // sub_steps):
        t0 = s * sub_steps
        r0 = t0 * n_seq
        if len(x_ref.shape) == 3:
            x = x_ref[t0:t0 + sub_steps].reshape(sub_rows, D_MODEL)
        else:
            x = x_ref[r0:r0 + sub_rows, :]
        h = _rmsnorm(x, g_ref[...]).astype(BF16)
        p = _dot(h, win_ref[...])
        gb = p[:, :D_MODEL]
        zp[halo + r0:halo + r0 + sub_rows, :] = p[:, D_MODEL:2 * D_MODEL] * p[:, 2 * D_MODEL:]
        conv = cw_ref[0:1, :] * zp[r0:r0 + sub_rows, :]
        for k in range(1, CONV_W):
            conv = conv + cw_ref[k:k + 1, :] * zp[r0 + k * n_seq:r0 + k * n_seq + sub_rows, :]
        m = _dot((gb * conv).astype(BF16), wout_ref[...])
        if len(o_ref.shape) == 3:
            o_ref[t0:t0 + sub_steps] = (x + m).reshape(sub_steps, n_seq, D_MODEL)
        else:
            o_ref[r0:r0 + sub_rows, :] = x + m
    tail = zp[rows:rows + halo, :]
    zp[0:halo, :] = tail
    for k in range(CONV_W - 1):
        newbuf_ref[:, k, :] = tail[k * n_seq:(k + 1) * n_seq, :]


def _conv_call(x, x_block, x_map, grid, buf0, buf_map, g, w_in, cw, w_out,
               *, steps, sub_steps, n_seq, carry_over_grid):
    rows = steps * n_seq
    halo = (CONV_W - 1) * n_seq
    whole = lambda a: pl.BlockSpec(a.shape, lambda i: (0,) * a.ndim)
    g2 = g.reshape(1, D_MODEL)
    buf_block = (n_seq, CONV_W - 1, D_MODEL)
    body = functools.partial(_conv_kernel, steps=steps, sub_steps=sub_steps, n_seq=n_seq,
                             carry_over_grid=carry_over_grid)
    return pl.pallas_call(
        body,
        grid=grid,
        in_specs=[pl.BlockSpec(x_block, x_map), whole(g2), pl.BlockSpec(buf_block, buf_map),
                  whole(w_in), whole(cw), whole(w_out)],
        out_specs=(pl.BlockSpec(x_block, x_map), pl.BlockSpec(buf_block, buf_map)),
        out_shape=(jax.ShapeDtypeStruct(x.shape, F32), jax.ShapeDtypeStruct(buf0.shape, F32)),
        scratch_shapes=[pltpu.VMEM((halo + rows, D_MODEL), F32)],
        compiler_params=pltpu.CompilerParams(
            dimension_semantics=("arbitrary",), vmem_limit_bytes=VMEM_LIMIT,
            allow_input_fusion=[False, False, False, True, False, True]),
        name="conv_mixer",
    )(x, g2, buf0, w_in, cw, w_out)


S5_PROMPT_STEPS = 128
S5_PROMPT_SUB_STEPS = 32
CONV_PROMPT_STEPS = 128
CONV_PROMPT_SUB_STEPS = 32
SAMPLE_SEQ_BLOCK = 32


def kernel(x_prompt, x_sample, state_ssm_re, state_ssm_im, cache_conv, norm_g, final_norm_g, ffn_w_gate_up, ffn_w_down, ssm_lam_re, ssm_lam_im, ssm_log_dt, ssm_b_re, ssm_b_im, ssm_c_re, ssm_c_im, ssm_d, ssm_w_glu, conv_w_in, conv_w, conv_w_out):
    nb_p, len_p, _ = x_prompt.shape
    nb_s, len_s, _ = x_sample.shape
    rows_p = nb_p * len_p
    rows_s = nb_s * len_s
    assert nb_p == SUBLANES and len_p % FFN_ROWS == 0 and rows_s == FFN_ROWS
    assert nb_s % SAMPLE_SEQ_BLOCK == 0

    lb_re, lb_im, wb_re, wb_im, wc_re, wc_im = _ssm_prep(
        ssm_lam_re[0], ssm_lam_im[0], ssm_log_dt[0], ssm_b_re[0], ssm_b_im[0], ssm_c_re[0], ssm_c_im[0])
    wglu = ssm_w_glu[0].astype(BF16)
    w_in = conv_w_in[0].astype(BF16)
    w_out = conv_w_out[0].astype(BF16)

    row_tiles_p = rows_p // FFN_ROWS
    flat_block = (FFN_ROWS, D_MODEL)
    flat_map = lambda i: (i, 0)

    whole_s = (x_sample.shape, lambda i: (0, 0, 0))
    flat_s = (flat_block, flat_map)

    def ffn_sample(xs, layer, half, x_spec=flat_s, out_spec=flat_s, final_g=None):
        out_shape = x_sample.shape if out_spec is whole_s else (rows_s, D_MODEL)
        return _ffn_call(xs, *x_spec, out_shape, *out_spec, 1,
                         norm_g[layer, 2 * half], (ffn_w_gate_up, ffn_w_down, (layer, half)),
                         FFN_COLS_F32, final_g=final_g)

    xs, w_bf_00 = ffn_sample(x_sample, 0, 0, x_spec=whole_s)
    seq_block = (len_s, SAMPLE_SEQ_BLOCK, D_MODEL)
    seq_map = lambda i: (0, i, 0)
    seq_grid = (nb_s // SAMPLE_SEQ_BLOCK,)
    xs, sre_s, sim_s = _s5_call(
        xs.reshape(len_s, nb_s, D_MODEL), seq_block, seq_map, seq_grid,
        state_ssm_re[0].reshape(nb_s, S_DIM), state_ssm_im[0].reshape(nb_s, S_DIM), lambda i: (i, 0),
        norm_g[0, 1], lb_re, lb_im, wb_re, wb_im, wc_re, wc_im, ssm_d[0], wglu,
        steps=len_s, sub_steps=len_s, n_seq=SAMPLE_SEQ_BLOCK, carry_over_grid=False)
    xs = xs.reshape(rows_s, D_MODEL)
    xs, w_bf_01 = ffn_sample(xs, 0, 1)
    xs, w_bf_10 = ffn_sample(xs, 1, 0)
    xs, buf_s = _conv_call(
        xs.reshape(len_s, nb_s, D_MODEL), seq_block, seq_map, seq_grid,
        cache_conv[0], lambda i: (i, 0, 0), norm_g[1, 1], w_in, conv_w[0], w_out,
        steps=len_s, sub_steps=len_s, n_seq=SAMPLE_SEQ_BLOCK, carry_over_grid=False)
    xs = xs.reshape(rows_s, D_MODEL)
    y_sample, w_bf_11 = ffn_sample(xs, 1, 1, out_spec=whole_s, final_g=final_norm_g)

    def ffn(x, out_shape, g, w_bf, final_g=None):
        return _ffn_resident_call(x, out_shape, row_tiles_p, g, w_bf, FFN_COLS, final_g=final_g)

    xp = ffn(x_prompt, (rows_p, D_MODEL), norm_g[0, 0], w_bf_00)
    zero_state = jnp.zeros((nb_p, S_DIM), F32)
    xp, sre_p, sim_p = _s5_call(
        xp, (S5_PROMPT_STEPS * nb_p, D_MODEL), flat_map, (len_p // S5_PROMPT_STEPS,),
        zero_state, zero_state, lambda i: (0, 0), norm_g[0, 1], lb_re, lb_im,
        wb_re, wb_im, wc_re, wc_im, ssm_d[0], wglu,
        steps=S5_PROMPT_STEPS, sub_steps=S5_PROMPT_SUB_STEPS, n_seq=nb_p, carry_over_grid=True)
    xp = ffn(xp, (rows_p, D_MODEL), norm_g[0, 2], w_bf_01)
    xp = ffn(xp, (rows_p, D_MODEL), norm_g[1, 0], w_bf_10)
    xp, buf_p = _conv_call(
        xp, (CONV_PROMPT_STEPS * nb_p, D_MODEL), flat_map, (len_p // CONV_PROMPT_STEPS,),
        jnp.zeros((nb_p, CONV_W - 1, D_MODEL), F32), lambda i: (0, 0, 0),
        norm_g[1, 1], w_in, conv_w[0], w_out,
        steps=CONV_PROMPT_STEPS, sub_steps=CONV_PROMPT_SUB_STEPS, n_seq=nb_p, carry_over_grid=True)
    y_prompt = ffn(xp, x_prompt.shape, norm_g[1, 2], w_bf_11, final_g=final_norm_g)

    state4 = lambda s: s.reshape(1, -1, N_GROUPS, P_STATE)
    return (y_prompt, y_sample, state4(sre_p), state4(sim_p), buf_p[None],
            state4(sre_s), state4(sim_s), buf_s[None])
```

```python
import functools

import jax
import jax.numpy as jnp
from jax import lax
from jax.experimental import pallas as pl
from jax.experimental.pallas import tpu as pltpu

F32 = jnp.float32
BF16 = jnp.bfloat16

D_MODEL = 1024
D_FF = 4 * D_MODEL
GROUP_SIZE = 16
N_GROUPS = D_MODEL // GROUP_SIZE
P_STATE = 64
S_DIM = N_GROUPS * P_STATE
CONV_W = 3
EPS = 1e-6

SUBLANES = 8
FFN_ROWS = 1024
FFN_COLS = 1024
FFN_COLS_F32 = 512
FFN_ROW_BLOCK = 256
SSM_BLOCKS = 4
SSM_BLOCK_CH = D_MODEL // SSM_BLOCKS
SSM_BLOCK_ST = S_DIM // SSM_BLOCKS
SCAN_LANES = 256
VMEM_LIMIT = 56 * 1024 * 1024


def _rmsnorm(x, g):
    return x * lax.rsqrt(jnp.mean(x * x, axis=-1, keepdims=True) + EPS) * g


def _dot(a, b):
    return jnp.dot(a, b, preferred_element_type=F32)


def _ffn_kernel(x_ref, g_ref, wg_ref, wu_ref, wd_ref, *rest, n_chunks, final_norm,
                x_seq_major, out_seq_major, emit_bf16, glu):
    rest = list(rest)
    gf_ref = rest.pop(0) if final_norm else None
    if glu:
        yg_ref, wglu_ref = rest[:2]
        del rest[:2]
    o_ref = rest.pop(0)
    if emit_bf16:
        wg_bf, wu_bf, wd_bf = rest[:3]
        del rest[:3]
    h_ref, acc_ref = rest[:2]
    xt_ref = rest[2] if (x_seq_major or glu) else x_ref
    j = pl.program_id(1)

    @pl.when(j == 0)
    def _():
        if x_seq_major:
            n_seq = x_ref.shape[0]
            for t in range(x_ref.shape[1]):
                xt_ref[t * n_seq:(t + 1) * n_seq, :] = x_ref[:, t, :]
        if glu:
            z = _dot(yg_ref[...], wglu_ref[...])
            xt_ref[...] = x_ref[...] + z[:, :D_MODEL] * jax.nn.sigmoid(z[:, D_MODEL:])
        h_ref[...] = _rmsnorm(xt_ref[...], g_ref[...]).astype(BF16)
        acc_ref[...] = jnp.zeros_like(acc_ref)

    if emit_bf16:
        wg_bf[...] = wg_ref[...].astype(BF16)
        wu_bf[...] = wu_ref[...].astype(BF16)
        wd_bf[...] = wd_ref[...].astype(BF16)
        wg_ref, wu_ref, wd_ref = wg_bf, wu_bf, wd_bf

    for r in range(0, FFN_ROWS, FFN_ROW_BLOCK):
        h = h_ref[r:r + FFN_ROW_BLOCK, :]
        gate = _dot(h, wg_ref[...])
        up = _dot(h, wu_ref[...])
        act = (jax.nn.silu(gate) * up).astype(BF16)
        acc_ref[r:r + FFN_ROW_BLOCK, :] += _dot(act, wd_ref[...])

    @pl.when(j == n_chunks - 1)
    def _():
        out = xt_ref[...] + 0.5 * acc_ref[...]
        if final_norm:
            out = _rmsnorm(out, gf_ref[...])
        if out_seq_major:
            n_seq = o_ref.shape[0]
            for t in range(o_ref.shape[1]):
                o_ref[:, t, :] = out[t * n_seq:(t + 1) * n_seq, :]
        else:
            o_ref[...] = out


def _ffn_call(x, x_block, x_map, out_shape, out_block, out_map, n_tiles, g, weights, cols,
              final_g=None, glu=None):
    n_chunks = D_FF // cols
    emit_bf16 = isinstance(weights[-1], tuple)
    x_seq_major = len(x_block) == 3
    out_seq_major = len(out_block) == 3
    body = functools.partial(_ffn_kernel, n_chunks=n_chunks, final_norm=final_g is not None,
                             x_seq_major=x_seq_major, out_seq_major=out_seq_major, emit_bf16=emit_bf16,
                             glu=glu is not None)
    scratch = [pltpu.VMEM((FFN_ROWS, D_MODEL), BF16), pltpu.VMEM((FFN_ROWS, D_MODEL), F32)]
    if x_seq_major or glu is not None:
        scratch.append(pltpu.VMEM((FFN_ROWS, D_MODEL), F32))

    const2 = lambda i, j: (0, 0)
    gate_bf = pl.BlockSpec((D_MODEL, cols), lambda i, j: (0, j))
    down_bf = pl.BlockSpec((cols, D_MODEL), lambda i, j: (j, 0))
    out_specs = pl.BlockSpec(out_block, lambda i, j: out_map(i))
    out_shapes = jax.ShapeDtypeStruct(out_shape, F32)
    if emit_bf16:
        w_gu, w_down, (layer, half) = weights
        w_specs = [
            pl.BlockSpec((None, None, D_MODEL, cols), lambda i, j: (layer, half, 0, j)),
            pl.BlockSpec((None, None, D_MODEL, cols), lambda i, j: (layer, half, 0, j + n_chunks)),
            pl.BlockSpec((None, None, cols, D_MODEL), lambda i, j: (layer, half, j, 0)),
        ]
        w_args = [w_gu, w_gu, w_down]
        out_specs = (out_specs, gate_bf, gate_bf, down_bf)
        out_shapes = (out_shapes,
                      jax.ShapeDtypeStruct((D_MODEL, D_FF), BF16),
                      jax.ShapeDtypeStruct((D_MODEL, D_FF), BF16),
                      jax.ShapeDtypeStruct((D_FF, D_MODEL), BF16))
    else:
        w_specs = [gate_bf, gate_bf, down_bf]
        w_args = list(weights)
    in_specs = [pl.BlockSpec(x_block, lambda i, j: x_map(i)), pl.BlockSpec((1, D_MODEL), const2)] + w_specs
    args = [x, g.reshape(1, D_MODEL)] + w_args
    if final_g is not None:
        in_specs.append(pl.BlockSpec((1, D_MODEL), const2))
        args.append(final_g.reshape(1, D_MODEL))
    if glu is not None:
        in_specs += [pl.BlockSpec((FFN_ROWS, D_MODEL), lambda i, j: (i, 0)), pl.BlockSpec(glu[1].shape, const2)]
        args += list(glu)
    res = pl.pallas_call(
        body,
        grid=(n_tiles, n_chunks),
        in_specs=in_specs,
        out_specs=out_specs,
        out_shape=out_shapes,
        scratch_shapes=scratch,
        compiler_params=pltpu.CompilerParams(
            dimension_semantics=("arbitrary", "arbitrary"), vmem_limit_bytes=VMEM_LIMIT),
        name="ffn",
    )(*args)
    return (res[0], tuple(res[1:])) if emit_bf16 else res


def _ffn_resident_kernel(x_ref, g_ref, wg_ref, wu_ref, wd_ref, *rest, cols, final_norm,
                         x_seq_major, out_seq_major, glu):
    rest = list(rest)
    gf_ref = rest.pop(0) if final_norm else None
    if glu:
        yg_ref, wglu_ref = rest[:2]
        del rest[:2]
    o_ref = rest.pop(0)
    if x_seq_major:
        xbuf, xsem = rest[:2]
        del rest[:2]
    if out_seq_major:
        obuf, osem = rest[:2]
    i = pl.program_id(0)
    n_tiles = pl.num_programs(0)
    slot = i % 2
    tile_steps = FFN_ROWS // SUBLANES
    steps_per_block = FFN_ROW_BLOCK // SUBLANES

    def x_copies(tile, slot_):
        return [pltpu.make_async_copy(x_ref.at[s, pl.ds(tile * tile_steps, tile_steps), :],
                                      xbuf.at[slot_, :, s, :], xsem.at[slot_]) for s in range(SUBLANES)]

    def o_copies(tile, slot_):
        return [pltpu.make_async_copy(obuf.at[slot_, :, s, :],
                                      o_ref.at[s, pl.ds(tile * tile_steps, tile_steps), :],
                                      osem.at[slot_]) for s in range(SUBLANES)]

    if x_seq_major:
        @pl.when(i == 0)
        def _():
            for cp in x_copies(0, 0):
                cp.start()

        @pl.when(i + 1 < n_tiles)
        def _():
            for cp in x_copies(i + 1, 1 - slot):
                cp.start()

        for cp in x_copies(i, slot):
            cp.wait()
    if out_seq_major:
        @pl.when(i >= 2)
        def _():
            for cp in o_copies(i - 2, slot):
                cp.wait()

    for r in range(0, FFN_ROWS, FFN_ROW_BLOCK):
        t0 = r // SUBLANES
        if x_seq_major:
            x = xbuf[slot, t0:t0 + steps_per_block].reshape(FFN_ROW_BLOCK, D_MODEL)
        else:
            x = x_ref[r:r + FFN_ROW_BLOCK, :]
        if glu:
            z = _dot(yg_ref[r:r + FFN_ROW_BLOCK, :], wglu_ref[...])
            x = x + z[:, :D_MODEL] * jax.nn.sigmoid(z[:, D_MODEL:])
        h = _rmsnorm(x, g_ref[...]).astype(BF16)
        acc = None
        for c in range(0, D_FF, cols):
            gate = _dot(h, wg_ref[:, c:c + cols])
            up = _dot(h, wu_ref[:, c:c + cols])
            act = (jax.nn.silu(gate) * up).astype(BF16)
            part = _dot(act, wd_ref[c:c + cols, :])
            acc = part if acc is None else acc + part
        out = x + 0.5 * acc
        if final_norm:
            out = _rmsnorm(out, gf_ref[...])
        if out_seq_major:
            obuf[slot, t0:t0 + steps_per_block] = out.reshape(steps_per_block, SUBLANES, D_MODEL)
        else:
            o_ref[r:r + FFN_ROW_BLOCK, :] = out

    if out_seq_major:
        for cp in o_copies(i, slot):
            cp.start()

        @pl.when(i == n_tiles - 1)
        def _():
            for cp in o_copies(i, slot):
                cp.wait()

        @pl.when(jnp.logical_and(i == n_tiles - 1, i >= 1))
        def _():
            for cp in o_copies(i - 1, 1 - slot):
                cp.wait()


def _ffn_resident_call(x, out_shape, n_tiles, g, weights, cols, final_g=None, glu=None):
    x_seq_major = x.ndim == 3
    out_seq_major = len(out_shape) == 3
    body = functools.partial(_ffn_resident_kernel, cols=cols, final_norm=final_g is not None,
                             x_seq_major=x_seq_major, out_seq_major=out_seq_major, glu=glu is not None)
    whole = lambda a: pl.BlockSpec(a.shape, lambda i: (0,) * a.ndim)
    row_tile = pl.BlockSpec((FFN_ROWS, D_MODEL), lambda i: (i, 0))
    in_hbm = pl.BlockSpec(memory_space=pl.ANY)
    g2 = g.reshape(1, D_MODEL)
    in_specs = [in_hbm if x_seq_major else row_tile, whole(g2)] + [whole(w) for w in weights]
    args = [x, g2, *weights]
    if final_g is not None:
        gf2 = final_g.reshape(1, D_MODEL)
        in_specs.append(whole(gf2))
        args.append(gf2)
    if glu is not None:
        in_specs += [row_tile, whole(glu[1])]
        args += list(glu)
    reorder_scratch = [pltpu.VMEM((2, FFN_ROWS // SUBLANES, SUBLANES, D_MODEL), F32),
                       pltpu.SemaphoreType.DMA((2,))]
    scratch = (reorder_scratch if x_seq_major else []) + (reorder_scratch if out_seq_major else [])
    return pl.pallas_call(
        body,
        grid=(n_tiles,),
        in_specs=in_specs,
        out_specs=in_hbm if out_seq_major else row_tile,
        out_shape=jax.ShapeDtypeStruct(out_shape, F32),
        scratch_shapes=scratch,
        compiler_params=pltpu.CompilerParams(
            dimension_semantics=("arbitrary",), vmem_limit_bytes=VMEM_LIMIT),
        name="ffn_resident",
    )(*args)


def _ssm_prep_kernel(lam_re_ref, lam_im_ref, ldt_ref, bt_re_ref, bt_im_ref, ct_re_ref, ct_im_ref,
                     lb_re_ref, lb_im_ref, wb_re_ref, wb_im_ref, wc_re_ref, wc_im_ref,
                     sb_re, sb_im, sc_re, sc_im):
    lam_re = lam_re_ref[...]
    lam_im = lam_im_ref[...]
    dt = jnp.exp(ldt_ref[...])
    mag = jnp.exp(lam_re * dt)
    lb_re = mag * jnp.cos(lam_im * dt)
    lb_im = mag * jnp.sin(lam_im * dt)
    lb_re_ref[...] = lb_re
    lb_im_ref[...] = lb_im
    den = lam_re * lam_re + lam_im * lam_im
    nr = lb_re - 1.0
    ni = lb_im
    f_re = (nr * lam_re + ni * lam_im) / den
    f_im = (ni * lam_re - nr * lam_im) / den
    gpb = N_GROUPS // SSM_BLOCKS
    for k in range(SSM_BLOCKS):
        for scratch in (sb_re, sb_im, sc_re, sc_im):
            scratch[...] = jnp.zeros_like(scratch)
        for j in range(gpb):
            grp = k * gpb + j
            fr = f_re[grp:grp + 1, :]
            fi = f_im[grp:grp + 1, :]
            b_re = bt_re_ref[grp]
            b_im = bt_im_ref[grp]
            rows = slice(j * GROUP_SIZE, (j + 1) * GROUP_SIZE)
            cols = slice(j * P_STATE, (j + 1) * P_STATE)
            sb_re[rows, cols] = fr * b_re - fi * b_im
            sb_im[rows, cols] = fr * b_im + fi * b_re
            sc_re[cols, rows] = ct_re_ref[grp]
            sc_im[cols, rows] = ct_im_ref[grp]
        wb_re_ref[k] = sb_re[...].astype(BF16)
        wb_im_ref[k] = sb_im[...].astype(BF16)
        wc_re_ref[k] = sc_re[...].astype(BF16)
        wc_im_ref[k] = sc_im[...].astype(BF16)


def _ssm_prep(lam_re, lam_im, log_dt, b_re, b_im, c_re, c_im):
    swap = lambda a: jnp.swapaxes(a, 1, 2)
    lb = jax.ShapeDtypeStruct((N_GROUPS, P_STATE), F32)
    wb = jax.ShapeDtypeStruct((SSM_BLOCKS, SSM_BLOCK_CH, SSM_BLOCK_ST), BF16)
    wc = jax.ShapeDtypeStruct((SSM_BLOCKS, SSM_BLOCK_ST, SSM_BLOCK_CH), BF16)
    return pl.pallas_call(
        _ssm_prep_kernel,
        out_shape=(lb, lb, wb, wb, wc, wc),
        scratch_shapes=[pltpu.VMEM((SSM_BLOCK_CH, SSM_BLOCK_ST), F32)] * 2
        + [pltpu.VMEM((SSM_BLOCK_ST, SSM_BLOCK_CH), F32)] * 2,
        name="ssm_prep",
    )(lam_re, lam_im, log_dt.reshape(N_GROUPS, 1), swap(b_re), swap(b_im), swap(c_re), swap(c_im))


def _s5_kernel(x_ref, g_ref, s0_re_ref, s0_im_ref, lb_re_ref, lb_im_ref,
               wb_re_ref, wb_im_ref, wc_re_ref, wc_im_ref, d_ref,
               o_ref, new_re_ref, new_im_ref,
               st_re, st_im, bu_re, bu_im, *, steps, sub_steps, n_seq, carry_over_grid):
    sub_rows = sub_steps * n_seq
    if carry_over_grid:
        @pl.when(pl.program_id(0) == 0)
        def _():
            st_re[...] = s0_re_ref[...]
            st_im[...] = s0_im_ref[...]
    else:
        st_re[...] = s0_re_ref[...]
        st_im[...] = s0_im_ref[...]

    for s in range(steps // sub_steps):
        t0 = s * sub_steps
        if len(x_ref.shape) == 3:
            x = x_ref[t0:t0 + sub_steps].reshape(sub_rows, D_MODEL)
        else:
            x = x_ref[t0 * n_seq:t0 * n_seq + sub_rows, :]
        u = _rmsnorm(x, g_ref[...])
        ub = u.astype(BF16)

        def project_in(k):
            uk = ub[:, k * SSM_BLOCK_CH:(k + 1) * SSM_BLOCK_CH]
            bu_re[k % 2] = _dot(uk, wb_re_ref[k])
            bu_im[k % 2] = _dot(uk, wb_im_ref[k])

        def scan_block(k):
            b_re, b_im = bu_re.at[k % 2], bu_im.at[k % 2]
            for c in range(SSM_BLOCK_ST // SCAN_LANES):
                lo = c * SCAN_LANES
                glo = k * SSM_BLOCK_ST + lo
                lr = jnp.broadcast_to(lb_re_ref[:, glo:glo + SCAN_LANES], (SUBLANES, SCAN_LANES))
                li = jnp.broadcast_to(lb_im_ref[:, glo:glo + SCAN_LANES], (SUBLANES, SCAN_LANES))
                for r_state in range(0, n_seq, SUBLANES):
                    hr = st_re[r_state:r_state + SUBLANES, glo:glo + SCAN_LANES]
                    hi = st_im[r_state:r_state + SUBLANES, glo:glo + SCAN_LANES]
                    for t in range(sub_steps):
                        r = t * n_seq + r_state
                        br = b_re[r:r + SUBLANES, lo:lo + SCAN_LANES]
                        bi = b_im[r:r + SUBLANES, lo:lo + SCAN_LANES]
                        hr, hi = lr * hr - li * hi + br, lr * hi + li * hr + bi
                        b_re[r:r + SUBLANES, lo:lo + SCAN_LANES] = hr
                        b_im[r:r + SUBLANES, lo:lo + SCAN_LANES] = hi
                    st_re[r_state:r_state + SUBLANES, glo:glo + SCAN_LANES] = hr
                    st_im[r_state:r_state + SUBLANES, glo:glo + SCAN_LANES] = hi

        ys = []
        project_in(0)
        for k in range(SSM_BLOCKS):
            if k + 1 < SSM_BLOCKS:
                project_in(k + 1)
            scan_block(k)
            hr = bu_re[k % 2].astype(BF16)
            hi = bu_im[k % 2].astype(BF16)
            ys.append(_dot(hr, wc_re_ref[k]) - _dot(hi, wc_im_ref[k]))
        y = jnp.concatenate(ys, axis=-1) + d_ref[...] * u
        yg = jax.nn.gelu(y)
        if len(o_ref.shape) == 3:
            o_ref[t0:t0 + sub_steps] = yg.reshape(sub_steps, n_seq, D_MODEL).astype(BF16)
        else:
            o_ref[t0 * n_seq:t0 * n_seq + sub_rows, :] = yg.astype(BF16)
    new_re_ref[...] = st_re[...]
    new_im_ref[...] = st_im[...]


def _s5_call(x, x_block, x_map, grid, s0_re, s0_im, state_map, g, lb_re, lb_im,
             wb_re, wb_im, wc_re, wc_im, d_skip, *, steps, sub_steps, n_seq, carry_over_grid):
    sub_rows = sub_steps * n_seq
    n_state = s0_re.shape[0]
    whole = lambda a: pl.BlockSpec(a.shape, lambda i: (0,) * a.ndim)
    state_spec = pl.BlockSpec((n_seq, S_DIM), state_map)
    g2 = g.reshape(1, D_MODEL)
    d2 = d_skip.reshape(1, D_MODEL)
    lbr = lb_re.reshape(1, S_DIM)
    lbi = lb_im.reshape(1, S_DIM)
    body = functools.partial(_s5_kernel, steps=steps, sub_steps=sub_steps, n_seq=n_seq,
                             carry_over_grid=carry_over_grid)
    return pl.pallas_call(
        body,
        grid=grid,
        in_specs=[pl.BlockSpec(x_block, x_map), whole(g2), state_spec, state_spec,
                  whole(lbr), whole(lbi), whole(wb_re), whole(wb_im), whole(wc_re), whole(wc_im),
                  whole(d2)],
        out_specs=(pl.BlockSpec(x_block, x_map), state_spec, state_spec),
        out_shape=(jax.ShapeDtypeStruct(x.shape, BF16),
                   jax.ShapeDtypeStruct((n_state, S_DIM), F32),
                   jax.ShapeDtypeStruct((n_state, S_DIM), F32)),
        scratch_shapes=[pltpu.VMEM((n_seq, S_DIM), F32), pltpu.VMEM((n_seq, S_DIM), F32),
                        pltpu.VMEM((2, sub_rows, SSM_BLOCK_ST), F32),
                        pltpu.VMEM((2, sub_rows, SSM_BLOCK_ST), F32)],
        compiler_params=pltpu.CompilerParams(
            dimension_semantics=("arbitrary",), vmem_limit_bytes=VMEM_LIMIT),
        name="s5_mixer",
    )(x, g2, s0_re, s0_im, lbr, lbi, wb_re, wb_im, wc_re, wc_im, d2)


def _conv_kernel(x_ref, g_ref, buf0_ref, win_ref, cw_ref, wout_ref, o_ref, newbuf_ref, zp,
                 *, steps, sub_steps, n_seq, carry_over_grid):
    rows = steps * n_seq
    halo = (CONV_W - 1) * n_seq

    def load_buf0():
        for k in range(CONV_W - 1):
            zp[k * n_seq:(k + 1) * n_seq, :] = buf0_ref[:, k, :]

    if carry_over_grid:
        pl.when(pl.program_id(0) == 0)(load_buf0)
    else:
        load_buf0()

    sub_rows = sub_steps * n_seq
    for s in range(steps // sub_steps):
        t0 = s * sub_steps
        r0 = t0 * n_seq
        if len(x_ref.shape) == 3:
            x = x_ref[t0:t0 + sub_steps].reshape(sub_rows, D_MODEL)
        else:
            x = x_ref[r0:r0 + sub_rows, :]
        h = _rmsnorm(x, g_ref[...]).astype(BF16)
        p = _dot(h, win_ref[...])
        gb = p[:, :D_MODEL]
        zp[halo + r0:halo + r0 + sub_rows, :] = p[:, D_MODEL:2 * D_MODEL] * p[:, 2 * D_MODEL:]
        conv = cw_ref[0:1, :] * zp[r0:r0 + sub_rows, :]
        for k in range(1, CONV_W):
            conv = conv + cw_ref[k:k + 1, :] * zp[r0 + k * n_seq:r0 + k * n_seq + sub_rows, :]
        m = _dot((gb * conv).astype(BF16), wout_ref[...])
        if len(o_ref.shape) == 3:
            o_ref[t0:t0 + sub_steps] = (x + m).reshape(sub_steps, n_seq, D_MODEL)
        else:
            o_ref[r0:r0 + sub_rows, :] = x + m
    tail = zp[rows:rows + halo, :]
    zp[0:halo, :] = tail
    for k in range(CONV_W - 1):
        newbuf_ref[:, k, :] = tail[k * n_seq:(k + 1) * n_seq, :]


def _conv_call(x, x_block, x_map, grid, buf0, buf_map, g, w_in, cw, w_out,
               *, steps, sub_steps, n_seq, carry_over_grid):
    rows = steps * n_seq
    halo = (CONV_W - 1) * n_seq
    whole = lambda a: pl.BlockSpec(a.shape, lambda i: (0,) * a.ndim)
    g2 = g.reshape(1, D_MODEL)
    buf_block = (n_seq, CONV_W - 1, D_MODEL)
    body = functools.partial(_conv_kernel, steps=steps, sub_steps=sub_steps, n_seq=n_seq,
                             carry_over_grid=carry_over_grid)
    return pl.pallas_call(
        body,
        grid=grid,
        in_specs=[pl.BlockSpec(x_block, x_map), whole(g2), pl.BlockSpec(buf_block, buf_map),
                  whole(w_in), whole(cw), whole(w_out)],
        out_specs=(pl.BlockSpec(x_block, x_map), pl.BlockSpec(buf_block, buf_map)),
        out_shape=(jax.ShapeDtypeStruct(x.shape, F32), jax.ShapeDtypeStruct(buf0.shape, F32)),
        scratch_shapes=[pltpu.VMEM((halo + rows, D_MODEL), F32)],
        compiler_params=pltpu.CompilerParams(
            dimension_semantics=("arbitrary",), vmem_limit_bytes=VMEM_LIMIT),
        name="conv_mixer",
    )(x, g2, buf0, w_in, cw, w_out)


S5_PROMPT_STEPS = 128
S5_PROMPT_SUB_STEPS = 32
CONV_PROMPT_STEPS = 128
CONV_PROMPT_SUB_STEPS = 32
SAMPLE_SEQ_BLOCK = 32


def kernel(x_prompt, x_sample, state_ssm_re, state_ssm_im, cache_conv, norm_g, final_norm_g, ffn_w_gate_up, ffn_w_down, ssm_lam_re, ssm_lam_im, ssm_log_dt, ssm_b_re, ssm_b_im, ssm_c_re, ssm_c_im, ssm_d, ssm_w_glu, conv_w_in, conv_w, conv_w_out):
    nb_p, len_p, _ = x_prompt.shape
    nb_s, len_s, _ = x_sample.shape
    rows_p = nb_p * len_p
    rows_s = nb_s * len_s
    assert nb_p == SUBLANES and len_p % FFN_ROWS == 0 and rows_s == FFN_ROWS
    assert nb_s % SAMPLE_SEQ_BLOCK == 0

    lb_re, lb_im, wb_re, wb_im, wc_re, wc_im = _ssm_prep(
        ssm_lam_re[0], ssm_lam_im[0], ssm_log_dt[0], ssm_b_re[0], ssm_b_im[0], ssm_c_re[0], ssm_c_im[0])
    wglu = ssm_w_glu[0].astype(BF16)
    w_in = conv_w_in[0].astype(BF16)
    w_out = conv_w_out[0].astype(BF16)

    row_tiles_p = rows_p // FFN_ROWS
    flat_block = (FFN_ROWS, D_MODEL)
    flat_map = lambda i: (i, 0)

    whole_s = (x_sample.shape, lambda i: (0, 0, 0))
    flat_s = (flat_block, flat_map)

    def ffn_sample(xs, layer, half, x_spec=flat_s, out_spec=flat_s, final_g=None, glu=None):
        out_shape = x_sample.shape if out_spec is whole_s else (rows_s, D_MODEL)
        return _ffn_call(xs, *x_spec, out_shape, *out_spec, 1,
                         norm_g[layer, 2 * half], (ffn_w_gate_up, ffn_w_down, (layer, half)),
                         FFN_COLS_F32, final_g=final_g, glu=glu)

    xs, w_bf_00 = ffn_sample(x_sample, 0, 0, x_spec=whole_s)
    seq_block = (len_s, SAMPLE_SEQ_BLOCK, D_MODEL)
    seq_map = lambda i: (0, i, 0)
    seq_grid = (nb_s // SAMPLE_SEQ_BLOCK,)
    yg_s, sre_s, sim_s = _s5_call(
        xs.reshape(len_s, nb_s, D_MODEL), seq_block, seq_map, seq_grid,
        state_ssm_re[0].reshape(nb_s, S_DIM), state_ssm_im[0].reshape(nb_s, S_DIM), lambda i: (i, 0),
        norm_g[0, 1], lb_re, lb_im, wb_re, wb_im, wc_re, wc_im, ssm_d[0],
        steps=len_s, sub_steps=len_s, n_seq=SAMPLE_SEQ_BLOCK, carry_over_grid=False)
    xs, w_bf_01 = ffn_sample(xs, 0, 1, glu=(yg_s.reshape(rows_s, D_MODEL), wglu))
    xs, w_bf_10 = ffn_sample(xs, 1, 0)
    xs, buf_s = _conv_call(
        xs.reshape(len_s, nb_s, D_MODEL), seq_block, seq_map, seq_grid,
        cache_conv[0], lambda i: (i, 0, 0), norm_g[1, 1], w_in, conv_w[0], w_out,
        steps=len_s, sub_steps=len_s, n_seq=SAMPLE_SEQ_BLOCK, carry_over_grid=False)
    xs = xs.reshape(rows_s, D_MODEL)
    y_sample, w_bf_11 = ffn_sample(xs, 1, 1, out_spec=whole_s, final_g=final_norm_g)

    def ffn(x, out_shape, g, w_bf, final_g=None, glu=None):
        return _ffn_resident_call(x, out_shape, row_tiles_p, g, w_bf, FFN_COLS, final_g=final_g, glu=glu)

    xp = ffn(x_prompt, (rows_p, D_MODEL), norm_g[0, 0], w_bf_00)
    zero_state = jnp.zeros((nb_p, S_DIM), F32)
    yg_p, sre_p, sim_p = _s5_call(
        xp, (S5_PROMPT_STEPS * nb_p, D_MODEL), flat_map, (len_p // S5_PROMPT_STEPS,),
        zero_state, zero_state, lambda i: (0, 0), norm_g[0, 1], lb_re, lb_im,
        wb_re, wb_im, wc_re, wc_im, ssm_d[0],
        steps=S5_PROMPT_STEPS, sub_steps=S5_PROMPT_SUB_STEPS, n_seq=nb_p, carry_over_grid=True)
    xp = ffn(xp, (rows_p, D_MODEL), norm_g[0, 2], w_bf_01, glu=(yg_p, wglu))
    xp = ffn(xp, (rows_p, D_MODEL), norm_g[1, 0], w_bf_10)
    xp, buf_p = _conv_call(
        xp, (CONV_PROMPT_STEPS * nb_p, D_MODEL), flat_map, (len_p // CONV_PROMPT_STEPS,),
        jnp.zeros((nb_p, CONV_W - 1, D_MODEL), F32), lambda i: (0, 0, 0),
        norm_g[1, 1], w_in, conv_w[0], w_out,
        steps=CONV_PROMPT_STEPS, sub_steps=CONV_PROMPT_SUB_STEPS, n_seq=nb_p, carry_over_grid=True)
    y_prompt = ffn(xp, x_prompt.shape, norm_g[1, 2], w_bf_11, final_g=final_norm_g)

    state4 = lambda s: s.reshape(1, -1, N_GROUPS, P_STATE)
    return (y_prompt, y_sample, state4(sre_p), state4(sim_p), buf_p[None],
            state4(sre_s), state4(sim_s), buf_s[None])
```

```python
import functools

import jax
import jax.numpy as jnp
from jax import lax
from jax.experimental import pallas as pl
from jax.experimental.pallas import tpu as pltpu

F32 = jnp.float32
BF16 = jnp.bfloat16

D_MODEL = 1024
D_FF = 4 * D_MODEL
GROUP_SIZE = 16
N_GROUPS = D_MODEL // GROUP_SIZE
P_STATE = 64
S_DIM = N_GROUPS * P_STATE
CONV_W = 3
EPS = 1e-6

SUBLANES = 8
FFN_ROWS = 1024
FFN_COLS = 1024
FFN_COLS_F32 = 512
FFN_ROW_BLOCK = 256
SSM_BLOCKS = 4
SSM_BLOCK_CH = D_MODEL // SSM_BLOCKS
SSM_BLOCK_ST = S_DIM // SSM_BLOCKS
SCAN_LANES = 256
VMEM_LIMIT = 56 * 1024 * 1024


def _rmsnorm(x, g):
    return x * lax.rsqrt(jnp.mean(x * x, axis=-1, keepdims=True) + EPS) * g


def _dot(a, b):
    return jnp.dot(a, b, preferred_element_type=F32)


def _ffn_kernel(x_ref, g_ref, wg_ref, wu_ref, wd_ref, *rest, n_chunks, final_norm,
                x_seq_major, out_seq_major, emit_bf16):
    rest = list(rest)
    gf_ref = rest.pop(0) if final_norm else None
    o_ref = rest.pop(0)
    if emit_bf16:
        wg_bf, wu_bf, wd_bf = rest[:3]
        del rest[:3]
    h_ref, acc_ref = rest[:2]
    xt_ref = rest[2] if x_seq_major else x_ref
    j = pl.program_id(1)

    @pl.when(j == 0)
    def _():
        if x_seq_major:
            n_seq = x_ref.shape[0]
            for t in range(x_ref.shape[1]):
                xt_ref[t * n_seq:(t + 1) * n_seq, :] = x_ref[:, t, :]
        h_ref[...] = _rmsnorm(xt_ref[...], g_ref[...]).astype(BF16)
        acc_ref[...] = jnp.zeros_like(acc_ref)

    if emit_bf16:
        wg_bf[...] = wg_ref[...].astype(BF16)
        wu_bf[...] = wu_ref[...].astype(BF16)
        wd_bf[...] = wd_ref[...].astype(BF16)
        wg_ref, wu_ref, wd_ref = wg_bf, wu_bf, wd_bf

    for r in range(0, FFN_ROWS, FFN_ROW_BLOCK):
        h = h_ref[r:r + FFN_ROW_BLOCK, :]
        gate = _dot(h, wg_ref[...])
        up = _dot(h, wu_ref[...])
        act = (jax.nn.silu(gate) * up).astype(BF16)
        acc_ref[r:r + FFN_ROW_BLOCK, :] += _dot(act, wd_ref[...])

    @pl.when(j == n_chunks - 1)
    def _():
        out = xt_ref[...] + 0.5 * acc_ref[...]
        if final_norm:
            out = _rmsnorm(out, gf_ref[...])
        if out_seq_major:
            n_seq = o_ref.shape[0]
            for t in range(o_ref.shape[1]):
                o_ref[:, t, :] = out[t * n_seq:(t + 1) * n_seq, :]
        else:
            o_ref[...] = out


def _ffn_call(x, x_block, x_map, out_shape, out_block, out_map, n_tiles, g, weights, cols,
              final_g=None):
    n_chunks = D_FF // cols
    emit_bf16 = isinstance(weights[-1], tuple)
    x_seq_major = len(x_block) == 3
    out_seq_major = len(out_block) == 3
    body = functools.partial(_ffn_kernel, n_chunks=n_chunks, final_norm=final_g is not None,
                             x_seq_major=x_seq_major, out_seq_major=out_seq_major, emit_bf16=emit_bf16)
    scratch = [pltpu.VMEM((FFN_ROWS, D_MODEL), BF16), pltpu.VMEM((FFN_ROWS, D_MODEL), F32)]
    if x_seq_major:
        scratch.append(pltpu.VMEM((FFN_ROWS, D_MODEL), F32))

    const2 = lambda i, j: (0, 0)
    gate_bf = pl.BlockSpec((D_MODEL, cols), lambda i, j: (0, j))
    down_bf = pl.BlockSpec((cols, D_MODEL), lambda i, j: (j, 0))
    out_specs = pl.BlockSpec(out_block, lambda i, j: out_map(i))
    out_shapes = jax.ShapeDtypeStruct(out_shape, F32)
    if emit_bf16:
        w_gu, w_down, (layer, half) = weights
        w_specs = [
            pl.BlockSpec((None, None, D_MODEL, cols), lambda i, j: (layer, half, 0, j)),
            pl.BlockSpec((None, None, D_MODEL, cols), lambda i, j: (layer, half, 0, j + n_chunks)),
            pl.BlockSpec((None, None, cols, D_MODEL), lambda i, j: (layer, half, j, 0)),
        ]
        w_args = [w_gu, w_gu, w_down]
        out_specs = (out_specs, gate_bf, gate_bf, down_bf)
        out_shapes = (out_shapes,
                      jax.ShapeDtypeStruct((D_MODEL, D_FF), BF16),
                      jax.ShapeDtypeStruct((D_MODEL, D_FF), BF16),
                      jax.ShapeDtypeStruct((D_FF, D_MODEL), BF16))
    else:
        w_specs = [gate_bf, gate_bf, down_bf]
        w_args = list(weights)
    in_specs = [pl.BlockSpec(x_block, lambda i, j: x_map(i)), pl.BlockSpec((1, D_MODEL), const2)] + w_specs
    args = [x, g.reshape(1, D_MODEL)] + w_args
    if final_g is not None:
        in_specs.append(pl.BlockSpec((1, D_MODEL), const2))
        args.append(final_g.reshape(1, D_MODEL))
    res = pl.pallas_call(
        body,
        grid=(n_tiles, n_chunks),
        in_specs=in_specs,
        out_specs=out_specs,
        out_shape=out_shapes,
        scratch_shapes=scratch,
        compiler_params=pltpu.CompilerParams(
            dimension_semantics=("arbitrary", "arbitrary"), vmem_limit_bytes=VMEM_LIMIT),
        name="ffn",
    )(*args)
    return (res[0], tuple(res[1:])) if emit_bf16 else res


def _ffn_resident_kernel(x_ref, g_ref, wg_ref, wu_ref, wd_ref, *rest, cols, final_norm,
                         x_seq_major, out_seq_major):
    rest = list(rest)
    gf_ref = rest.pop(0) if final_norm else None
    o_ref = rest.pop(0)
    if x_seq_major:
        xbuf, xsem = rest[:2]
        del rest[:2]
    if out_seq_major:
        obuf, osem = rest[:2]
    i = pl.program_id(0)
    n_tiles = pl.num_programs(0)
    slot = i % 2
    tile_steps = FFN_ROWS // SUBLANES
    steps_per_block = FFN_ROW_BLOCK // SUBLANES

    def x_copies(tile, slot_):
        return [pltpu.make_async_copy(x_ref.at[s, pl.ds(tile * tile_steps, tile_steps), :],
                                      xbuf.at[slot_, :, s, :], xsem.at[slot_]) for s in range(SUBLANES)]

    def o_copies(tile, slot_):
        return [pltpu.make_async_copy(obuf.at[slot_, :, s, :],
                                      o_ref.at[s, pl.ds(tile * tile_steps, tile_steps), :],
                                      osem.at[slot_]) for s in range(SUBLANES)]

    if x_seq_major:
        @pl.when(i == 0)
        def _():
            for cp in x_copies(0, 0):
                cp.start()

        @pl.when(i + 1 < n_tiles)
        def _():
            for cp in x_copies(i + 1, 1 - slot):
                cp.start()

        for cp in x_copies(i, slot):
            cp.wait()
    if out_seq_major:
        @pl.when(i >= 2)
        def _():
            for cp in o_copies(i - 2, slot):
                cp.wait()

    for r in range(0, FFN_ROWS, FFN_ROW_BLOCK):
        t0 = r // SUBLANES
        if x_seq_major:
            x = xbuf[slot, t0:t0 + steps_per_block].reshape(FFN_ROW_BLOCK, D_MODEL)
        else:
            x = x_ref[r:r + FFN_ROW_BLOCK, :]
        h = _rmsnorm(x, g_ref[...]).astype(BF16)
        acc = None
        for c in range(0, D_FF, cols):
            gate = _dot(h, wg_ref[:, c:c + cols])
            up = _dot(h, wu_ref[:, c:c + cols])
            act = (jax.nn.silu(gate) * up).astype(BF16)
            part = _dot(act, wd_ref[c:c + cols, :])
            acc = part if acc is None else acc + part
        out = x + 0.5 * acc
        if final_norm:
            out = _rmsnorm(out, gf_ref[...])
        if out_seq_major:
            obuf[slot, t0:t0 + steps_per_block] = out.reshape(steps_per_block, SUBLANES, D_MODEL)
        else:
            o_ref[r:r + FFN_ROW_BLOCK, :] = out

    if out_seq_major:
        for cp in o_copies(i, slot):
            cp.start()

        @pl.when(i == n_tiles - 1)
        def _():
            for cp in o_copies(i, slot):
                cp.wait()

        @pl.when(jnp.logical_and(i == n_tiles - 1, i >= 1))
        def _():
            for cp in o_copies(i - 1, 1 - slot):
                cp.wait()


def _ffn_resident_call(x, out_shape, n_tiles, g, weights, cols, final_g=None):
    x_seq_major = x.ndim == 3
    out_seq_major = len(out_shape) == 3
    body = functools.partial(_ffn_resident_kernel, cols=cols, final_norm=final_g is not None,
                             x_seq_major=x_seq_major, out_seq_major=out_seq_major)
    whole = lambda a: pl.BlockSpec(a.shape, lambda i: (0,) * a.ndim)
    row_tile = pl.BlockSpec((FFN_ROWS, D_MODEL), lambda i: (i, 0))
    in_hbm = pl.BlockSpec(memory_space=pl.ANY)
    g2 = g.reshape(1, D_MODEL)
    in_specs = [in_hbm if x_seq_major else row_tile, whole(g2)] + [whole(w) for w in weights]
    args = [x, g2, *weights]
    if final_g is not None:
        gf2 = final_g.reshape(1, D_MODEL)
        in_specs.append(whole(gf2))
        args.append(gf2)
    reorder_scratch = [pltpu.VMEM((2, FFN_ROWS // SUBLANES, SUBLANES, D_MODEL), F32),
                       pltpu.SemaphoreType.DMA((2,))]
    scratch = (reorder_scratch if x_seq_major else []) + (reorder_scratch if out_seq_major else [])
    return pl.pallas_call(
        body,
        grid=(n_tiles,),
        in_specs=in_specs,
        out_specs=in_hbm if out_seq_major else row_tile,
        out_shape=jax.ShapeDtypeStruct(out_shape, F32),
        scratch_shapes=scratch,
        compiler_params=pltpu.CompilerParams(
            dimension_semantics=("arbitrary",), vmem_limit_bytes=VMEM_LIMIT),
        name="ffn_resident",
    )(*args)


def _ssm_prep_kernel(lam_re_ref, lam_im_ref, ldt_ref, bt_re_ref, bt_im_ref, ct_re_ref, ct_im_ref,
                     lb_re_ref, lb_im_ref, wb_re_ref, wb_im_ref, wc_re_ref, wc_im_ref,
                     sb_re, sb_im, sc_re, sc_im):
    lam_re = lam_re_ref[...]
    lam_im = lam_im_ref[...]
    dt = jnp.exp(ldt_ref[...])
    mag = jnp.exp(lam_re * dt)
    lb_re = mag * jnp.cos(lam_im * dt)
    lb_im = mag * jnp.sin(lam_im * dt)
    lb_re_ref[...] = lb_re
    lb_im_ref[...] = lb_im
    den = lam_re * lam_re + lam_im * lam_im
    nr = lb_re - 1.0
    ni = lb_im
    f_re = (nr * lam_re + ni * lam_im) / den
    f_im = (ni * lam_re - nr * lam_im) / den
    gpb = N_GROUPS // SSM_BLOCKS
    for k in range(SSM_BLOCKS):
        for scratch in (sb_re, sb_im, sc_re, sc_im):
            scratch[...] = jnp.zeros_like(scratch)
        for j in range(gpb):
            grp = k * gpb + j
            fr = f_re[grp:grp + 1, :]
            fi = f_im[grp:grp + 1, :]
            b_re = bt_re_ref[grp]
            b_im = bt_im_ref[grp]
            rows = slice(j * GROUP_SIZE, (j + 1) * GROUP_SIZE)
            cols = slice(j * P_STATE, (j + 1) * P_STATE)
            sb_re[rows, cols] = fr * b_re - fi * b_im
            sb_im[rows, cols] = fr * b_im + fi * b_re
            sc_re[cols, rows] = ct_re_ref[grp]
            sc_im[cols, rows] = ct_im_ref[grp]
        wb_re_ref[k] = sb_re[...].astype(BF16)
        wb_im_ref[k] = sb_im[...].astype(BF16)
        wc_re_ref[k] = sc_re[...].astype(BF16)
        wc_im_ref[k] = sc_im[...].astype(BF16)


def _ssm_prep(lam_re, lam_im, log_dt, b_re, b_im, c_re, c_im):
    swap = lambda a: jnp.swapaxes(a, 1, 2)
    lb = jax.ShapeDtypeStruct((N_GROUPS, P_STATE), F32)
    wb = jax.ShapeDtypeStruct((SSM_BLOCKS, SSM_BLOCK_CH, SSM_BLOCK_ST), BF16)
    wc = jax.ShapeDtypeStruct((SSM_BLOCKS, SSM_BLOCK_ST, SSM_BLOCK_CH), BF16)
    return pl.pallas_call(
        _ssm_prep_kernel,
        out_shape=(lb, lb, wb, wb, wc, wc),
        scratch_shapes=[pltpu.VMEM((SSM_BLOCK_CH, SSM_BLOCK_ST), F32)] * 2
        + [pltpu.VMEM((SSM_BLOCK_ST, SSM_BLOCK_CH), F32)] * 2,
        name="ssm_prep",
    )(lam_re, lam_im, log_dt.reshape(N_GROUPS, 1), swap(b_re), swap(b_im), swap(c_re), swap(c_im))


def _s5_kernel(x_ref, g_ref, s0_re_ref, s0_im_ref, lb_re_ref, lb_im_ref,
               wb_re_ref, wb_im_ref, wc_re_ref, wc_im_ref, d_ref, wglu_ref,
               o_ref, new_re_ref, new_im_ref,
               st_re, st_im, bu_re, bu_im, *, steps, sub_steps, n_seq, carry_over_grid):
    sub_rows = sub_steps * n_seq
    if carry_over_grid:
        @pl.when(pl.program_id(0) == 0)
        def _():
            st_re[...] = s0_re_ref[...]
            st_im[...] = s0_im_ref[...]
    else:
        st_re[...] = s0_re_ref[...]
        st_im[...] = s0_im_ref[...]

    for s in range(steps // sub_steps):
        t0 = s * sub_steps
        if len(x_ref.shape) == 3:
            x = x_ref[t0:t0 + sub_steps].reshape(sub_rows, D_MODEL)
        else:
            x = x_ref[t0 * n_seq:t0 * n_seq + sub_rows, :]
        u = _rmsnorm(x, g_ref[...])
        ub = u.astype(BF16)

        def project_in(k):
            uk = ub[:, k * SSM_BLOCK_CH:(k + 1) * SSM_BLOCK_CH]
            bu_re[k % 2] = _dot(uk, wb_re_ref[k])
            bu_im[k % 2] = _dot(uk, wb_im_ref[k])

        def scan_block(k):
            b_re, b_im = bu_re.at[k % 2], bu_im.at[k % 2]
            for c in range(SSM_BLOCK_ST // SCAN_LANES):
                lo = c * SCAN_LANES
                glo = k * SSM_BLOCK_ST + lo
                lr = jnp.broadcast_to(lb_re_ref[:, glo:glo + SCAN_LANES], (SUBLANES, SCAN_LANES))
                li = jnp.broadcast_to(lb_im_ref[:, glo:glo + SCAN_LANES], (SUBLANES, SCAN_LANES))
                for r_state in range(0, n_seq, SUBLANES):
                    hr = st_re[r_state:r_state + SUBLANES, glo:glo + SCAN_LANES]
                    hi = st_im[r_state:r_state + SUBLANES, glo:glo + SCAN_LANES]
                    for t in range(sub_steps):
                        r = t * n_seq + r_state
                        br = b_re[r:r + SUBLANES, lo:lo + SCAN_LANES]
                        bi = b_im[r:r + SUBLANES, lo:lo + SCAN_LANES]
                        hr, hi = lr * hr - li * hi + br, lr * hi + li * hr + bi
                        b_re[r:r + SUBLANES, lo:lo + SCAN_LANES] = hr
                        b_im[r:r + SUBLANES, lo:lo + SCAN_LANES] = hi
                    st_re[r_state:r_state + SUBLANES, glo:glo + SCAN_LANES] = hr
                    st_im[r_state:r_state + SUBLANES, glo:glo + SCAN_LANES] = hi

        ys = []
        project_in(0)
        for k in range(SSM_BLOCKS):
            if k + 1 < SSM_BLOCKS:
                project_in(k + 1)
            scan_block(k)
            hr = bu_re[k % 2].astype(BF16)
            hi = bu_im[k % 2].astype(BF16)
            ys.append(_dot(hr, wc_re_ref[k]) - _dot(hi, wc_im_ref[k]))
        y = jnp.concatenate(ys, axis=-1) + d_ref[...] * u
        z = _dot(jax.nn.gelu(y).astype(BF16), wglu_ref[...])
        m = z[:, :D_MODEL] * jax.nn.sigmoid(z[:, D_MODEL:])
        if len(o_ref.shape) == 3:
            o_ref[t0:t0 + sub_steps] = (x + m).reshape(sub_steps, n_seq, D_MODEL)
        else:
            o_ref[t0 * n_seq:t0 * n_seq + sub_rows, :] = x + m
    new_re_ref[...] = st_re[...]
    new_im_ref[...] = st_im[...]


def _s5_call(x, x_block, x_map, grid, s0_re, s0_im, state_map, g, lb_re, lb_im,
             wb_re, wb_im, wc_re, wc_im, d_skip, wglu, *, steps, sub_steps, n_seq, carry_over_grid):
    sub_rows = sub_steps * n_seq
    n_state = s0_re.shape[0]
    whole = lambda a: pl.BlockSpec(a.shape, lambda i: (0,) * a.ndim)
    state_spec = pl.BlockSpec((n_seq, S_DIM), state_map)
    g2 = g.reshape(1, D_MODEL)
    d2 = d_skip.reshape(1, D_MODEL)
    lbr = lb_re.reshape(1, S_DIM)
    lbi = lb_im.reshape(1, S_DIM)
    body = functools.partial(_s5_kernel, steps=steps, sub_steps=sub_steps, n_seq=n_seq,
                             carry_over_grid=carry_over_grid)
    return pl.pallas_call(
        body,
        grid=grid,
        in_specs=[pl.BlockSpec(x_block, x_map), whole(g2), state_spec, state_spec,
                  whole(lbr), whole(lbi), whole(wb_re), whole(wb_im), whole(wc_re), whole(wc_im),
                  whole(d2), whole(wglu)],
        out_specs=(pl.BlockSpec(x_block, x_map), state_spec, state_spec),
        out_shape=(jax.ShapeDtypeStruct(x.shape, F32),
                   jax.ShapeDtypeStruct((n_state, S_DIM), F32),
                   jax.ShapeDtypeStruct((n_state, S_DIM), F32)),
        scratch_shapes=[pltpu.VMEM((n_seq, S_DIM), F32), pltpu.VMEM((n_seq, S_DIM), F32),
                        pltpu.VMEM((2, sub_rows, SSM_BLOCK_ST), F32),
                        pltpu.VMEM((2, sub_rows, SSM_BLOCK_ST), F32)],
        compiler_params=pltpu.CompilerParams(
            dimension_semantics=("arbitrary",), vmem_limit_bytes=VMEM_LIMIT),
        name="s5_mixer",
    )(x, g2, s0_re, s0_im, lbr, lbi, wb_re, wb_im, wc_re, wc_im, d2, wglu)


def _conv_kernel(x_ref, g_ref, buf0_ref, win_ref, cw_ref, wout_ref, o_ref, newbuf_ref, zp,
                 *, steps, sub_steps, n_seq, carry_over_grid):
    rows = steps * n_seq
    halo = (CONV_W - 1) * n_seq

    def load_buf0():
        for k in range(CONV_W - 1):
            zp[k * n_seq:(k + 1) * n_seq, :] = buf0_ref[:, k, :]

    if carry_over_grid:
        pl.when(pl.program_id(0) == 0)(load_buf0)
    else:
        load_buf0()

    sub_rows = sub_steps * n_seq
    for s in range(steps // sub_steps):
        t0 = s * sub_steps
        r0 = t0 * n_seq
        if len(x_ref.shape) == 3:
            x = x_ref[t0:t0 + sub_steps].reshape(sub_rows, D_MODEL)
        else:
            x = x_ref[r0:r0 + sub_rows, :]
        h = _rmsnorm(x, g_ref[...]).astype(BF16)
        gb = _dot(h, win_ref[:, :D_MODEL])
        gc = _dot(h, win_ref[:, D_MODEL:2 * D_MODEL])
        zp[halo + r0:halo + r0 + sub_rows, :] = gc * _dot(h, win_ref[:, 2 * D_MODEL:])
        conv = cw_ref[0:1, :] * zp[r0:r0 + sub_rows, :]
        for k in range(1, CONV_W):
            conv = conv + cw_ref[k:k + 1, :] * zp[r0 + k * n_seq:r0 + k * n_seq + sub_rows, :]
        m = _dot((gb * conv).astype(BF16), wout_ref[...])
        if len(o_ref.shape) == 3:
            o_ref[t0:t0 + sub_steps] = (x + m).reshape(sub_steps, n_seq, D_MODEL)
        else:
            o_ref[r0:r0 + sub_rows, :] = x + m
    tail = zp[rows:rows + halo, :]
    zp[0:halo, :] = tail
    for k in range(CONV_W - 1):
        newbuf_ref[:, k, :] = tail[k * n_seq:(k + 1) * n_seq, :]


def _conv_call(x, x_block, x_map, grid, buf0, buf_map, g, w_in, cw, w_out,
               *, steps, sub_steps, n_seq, carry_over_grid):
    rows = steps * n_seq
    halo = (CONV_W - 1) * n_seq
    whole = lambda a: pl.BlockSpec(a.shape, lambda i: (0,) * a.ndim)
    g2 = g.reshape(1, D_MODEL)
    buf_block = (n_seq, CONV_W - 1, D_MODEL)
    body = functools.partial(_conv_kernel, steps=steps, sub_steps=sub_steps, n_seq=n_seq,
                             carry_over_grid=carry_over_grid)
    return pl.pallas_call(
        body,
        grid=grid,
        in_specs=[pl.BlockSpec(x_block, x_map), whole(g2), pl.BlockSpec(buf_block, buf_map),
                  whole(w_in), whole(cw), whole(w_out)],
        out_specs=(pl.BlockSpec(x_block, x_map), pl.BlockSpec(buf_block, buf_map)),
        out_shape=(jax.ShapeDtypeStruct(x.shape, F32), jax.ShapeDtypeStruct(buf0.shape, F32)),
        scratch_shapes=[pltpu.VMEM((halo + rows, D_MODEL), F32)],
        compiler_params=pltpu.CompilerParams(
            dimension_semantics=("arbitrary",), vmem_limit_bytes=VMEM_LIMIT),
        name="conv_mixer",
    )(x, g2, buf0, w_in, cw, w_out)


S5_PROMPT_STEPS = 128
S5_PROMPT_SUB_STEPS = 32
CONV_PROMPT_STEPS = 128
CONV_PROMPT_SUB_STEPS = 32
SAMPLE_SEQ_BLOCK = 32


def kernel(x_prompt, x_sample, state_ssm_re, state_ssm_im, cache_conv, norm_g, final_norm_g, ffn_w_gate_up, ffn_w_down, ssm_lam_re, ssm_lam_im, ssm_log_dt, ssm_b_re, ssm_b_im, ssm_c_re, ssm_c_im, ssm_d, ssm_w_glu, conv_w_in, conv_w, conv_w_out):
    nb_p, len_p, _ = x_prompt.shape
    nb_s, len_s, _ = x_sample.shape
    rows_p = nb_p * len_p
    rows_s = nb_s * len_s
    assert nb_p == SUBLANES and len_p % FFN_ROWS == 0 and rows_s == FFN_ROWS
    assert nb_s % SAMPLE_SEQ_BLOCK == 0

    lb_re, lb_im, wb_re, wb_im, wc_re, wc_im = _ssm_prep(
        ssm_lam_re[0], ssm_lam_im[0], ssm_log_dt[0], ssm_b_re[0], ssm_b_im[0], ssm_c_re[0], ssm_c_im[0])
    wglu = ssm_w_glu[0].astype(BF16)
    w_in = conv_w_in[0].astype(BF16)
    w_out = conv_w_out[0].astype(BF16)

    row_tiles_p = rows_p // FFN_ROWS
    flat_block = (FFN_ROWS, D_MODEL)
    flat_map = lambda i: (i, 0)

    whole_s = (x_sample.shape, lambda i: (0, 0, 0))
    flat_s = (flat_block, flat_map)

    def ffn_sample(xs, layer, half, x_spec=flat_s, out_spec=flat_s, final_g=None):
        out_shape = x_sample.shape if out_spec is whole_s else (rows_s, D_MODEL)
        return _ffn_call(xs, *x_spec, out_shape, *out_spec, 1,
                         norm_g[layer, 2 * half], (ffn_w_gate_up, ffn_w_down, (layer, half)),
                         FFN_COLS_F32, final_g=final_g)

    xs, w_bf_00 = ffn_sample(x_sample, 0, 0, x_spec=whole_s)
    seq_block = (len_s, SAMPLE_SEQ_BLOCK, D_MODEL)
    seq_map = lambda i: (0, i, 0)
    seq_grid = (nb_s // SAMPLE_SEQ_BLOCK,)
    xs, sre_s, sim_s = _s5_call(
        xs.reshape(len_s, nb_s, D_MODEL), seq_block, seq_map, seq_grid,
        state_ssm_re[0].reshape(nb_s, S_DIM), state_ssm_im[0].reshape(nb_s, S_DIM), lambda i: (i, 0),
        norm_g[0, 1], lb_re, lb_im, wb_re, wb_im, wc_re, wc_im, ssm_d[0], wglu,
        steps=len_s, sub_steps=len_s, n_seq=SAMPLE_SEQ_BLOCK, carry_over_grid=False)
    xs = xs.reshape(rows_s, D_MODEL)
    xs, w_bf_01 = ffn_sample(xs, 0, 1)
    xs, w_bf_10 = ffn_sample(xs, 1, 0)
    xs, buf_s = _conv_call(
        xs.reshape(len_s, nb_s, D_MODEL), seq_block, seq_map, seq_grid,
        cache_conv[0], lambda i: (i, 0, 0), norm_g[1, 1], w_in, conv_w[0], w_out,
        steps=len_s, sub_steps=len_s, n_seq=SAMPLE_SEQ_BLOCK, carry_over_grid=False)
    xs = xs.reshape(rows_s, D_MODEL)
    y_sample, w_bf_11 = ffn_sample(xs, 1, 1, out_spec=whole_s, final_g=final_norm_g)

    def ffn(x, out_shape, g, w_bf, final_g=None):
        return _ffn_resident_call(x, out_shape, row_tiles_p, g, w_bf, FFN_COLS, final_g=final_g)

    xp = ffn(x_prompt, (rows_p, D_MODEL), norm_g[0, 0], w_bf_00)
    zero_state = jnp.zeros((nb_p, S_DIM), F32)
    xp, sre_p, sim_p = _s5_call(
        xp, (S5_PROMPT_STEPS * nb_p, D_MODEL), flat_map, (len_p // S5_PROMPT_STEPS,),
        zero_state, zero_state, lambda i: (0, 0), norm_g[0, 1], lb_re, lb_im,
        wb_re, wb_im, wc_re, wc_im, ssm_d[0], wglu,
        steps=S5_PROMPT_STEPS, sub_steps=S5_PROMPT_SUB_STEPS, n_seq=nb_p, carry_over_grid=True)
    xp = ffn(xp, (rows_p, D_MODEL), norm_g[0, 2], w_bf_01)
    xp = ffn(xp, (rows_p, D_MODEL), norm_g[1, 0], w_bf_10)
    xp, buf_p = _conv_call(
        xp, (CONV_PROMPT_STEPS * nb_p, D_MODEL), flat_map, (len_p // CONV_PROMPT_STEPS,),
        jnp.zeros((nb_p, CONV_W - 1, D_MODEL), F32), lambda i: (0, 0, 0),
        norm_g[1, 1], w_in, conv_w[0], w_out,
        steps=CONV_PROMPT_STEPS, sub_steps=CONV_PROMPT_SUB_STEPS, n_seq=nb_p, carry_over_grid=True)
    y_prompt = ffn(xp, x_prompt.shape, norm_g[1, 2], w_bf_11, final_g=final_norm_g)

    state4 = lambda s: s.reshape(1, -1, N_GROUPS, P_STATE)
    return (y_prompt, y_sample, state4(sre_p), state4(sim_p), buf_p[None],
            state4(sre_s), state4(sim_s), buf_s[None])
```
